```python
import jax
import jax.numpy as jnp
from jax import lax
import numpy as np

D_MODEL = 2048
BATCH = 8
SEQ = 4096
DEPTH = 4

GRID_W = 64
CTX_LEN = 256
EPS = 1e-6
NEG_BIG = -1e30
LB_FLOOR = 1e-30

A_WIDTH = D_MODEL // 4
A_GROUPS = 4
A_GROUP_DIM = A_WIDTH // A_GROUPS
A_CHUNK = 128

B_WIDTH = D_MODEL // 2
B_HEADS = 4
B_HEAD_DIM = B_WIDTH // B_HEADS
B_CHUNK = 128
CONV_W = 3

C_WIDTH = D_MODEL // 4
C_HEADS = 4
C_KEY_DIM = 128
C_KEY_WIDTH = C_HEADS * C_KEY_DIM
C_VAL_DIM = C_WIDTH // C_HEADS
C_CHUNK = 64

D_MIX = A_WIDTH + B_WIDTH + C_WIDTH
FORGET_BIAS_LO = 3.0
FORGET_BIAS_HI = 6.0

IN_LAYOUT = (
    ('a_u', A_WIDTH), ('a_v', A_WIDTH), ('a_z', A_WIDTH),
    ('b_q', B_WIDTH), ('b_k', B_WIDTH), ('b_v', B_WIDTH), ('b_o', B_WIDTH), ('b_z', B_WIDTH),
    ('b_i_fwd', B_HEADS), ('b_f_fwd', B_HEADS), ('b_i_bwd', B_HEADS), ('b_f_bwd', B_HEADS),
    ('c_q', C_KEY_WIDTH), ('c_f_fwd', C_KEY_WIDTH), ('c_f_bwd', C_KEY_WIDTH), ('c_i', C_WIDTH), ('c_g', C_WIDTH),
)
D_IN = sum(w for _, w in IN_LAYOUT)

kernel_name = 'hybrid_ctx_prefix_chunkmlp_mlstm_hgrn2'


def _col_start(name):
    start = 0
    for n, w in IN_LAYOUT:
        if n == name:
            return start
        start += w
    raise ValueError(name)


def _split_cols(p):
    out = {}
    start = 0
    for name, w in IN_LAYOUT:
        out[name] = p[..., start:start + w]
        start += w
    return out


def _rmsnorm(x, g):
    xf = x.astype(jnp.float32)
    y = xf * lax.rsqrt(jnp.mean(xf * xf, axis=-1, keepdims=True) + EPS)
    return (y * g.astype(jnp.float32)).astype(x.dtype)


def _head_rmsnorm(x, g, n_heads):
    b, t, w = x.shape
    xh = x.reshape(b, t, n_heads, w // n_heads).astype(jnp.float32)
    xh = xh * lax.rsqrt(jnp.mean(xh * xh, axis=-1, keepdims=True) + EPS)
    return (xh.reshape(b, t, w) * g.astype(jnp.float32)).astype(x.dtype)


def _to_heads(x, n_heads):
    b, t, w = x.shape
    return x.reshape(b, t, n_heads, w // n_heads).transpose(0, 2, 1, 3)


def _from_heads(x):
    b, h, t, d = x.shape
    return x.transpose(0, 2, 1, 3).reshape(b, t, h * d)


def _to_colmajor(x, rows):
    b, t, ch = x.shape
    return x.reshape(b, rows, GRID_W, ch).transpose(0, 2, 1, 3).reshape(b, t, ch)


def _from_colmajor(x, rows):
    b, t, ch = x.shape
    return x.reshape(b, GRID_W, rows, ch).transpose(0, 2, 1, 3).reshape(b, t, ch)


def _conv_centred(x, w):
    pad = CONV_W // 2
    t = x.shape[1]
    xp = jnp.pad(x, ((0, 0), (pad, pad), (0, 0)))
    out = w[0] * xp[:, 0:t]
    for j in range(1, CONV_W):
        out = out + w[j] * xp[:, j:j + t]
    return out


def _chunk_mlp(u, v, z, w_s, b_s):
    b, t, _ = v.shape
    n = t // A_CHUNK
    vf = v.astype(jnp.float32).reshape(b, n, A_CHUNK, A_GROUPS, A_GROUP_DIM)
    mu = jnp.mean(vf, axis=-1, keepdims=True)
    var = jnp.mean(jnp.square(vf - mu), axis=-1, keepdims=True)
    vn = ((vf - mu) * lax.rsqrt(var + EPS)).astype(v.dtype)
    mixed = jnp.einsum('gts,bnsgc->bntgc', w_s, vn) + b_s.T[:, :, None]
    return u * mixed.reshape(b, t, A_WIDTH) * jax.nn.silu(z)


def _mlstm_zero_state(b):
    return (jnp.zeros((b, B_HEADS, B_HEAD_DIM, B_HEAD_DIM), jnp.float32),
            jnp.zeros((b, B_HEADS, B_HEAD_DIM), jnp.float32),
            jnp.zeros((b, B_HEADS), jnp.float32))


def _mlstm_dir(q, k, v, ig, lf, state):
    b, h, t, d = q.shape
    L = B_CHUNK
    nc = t // L
    mask = jnp.tril(jnp.ones((L, L), bool))

    def chunks(a):
        return jnp.moveaxis(a.reshape((b, h, nc, L) + a.shape[3:]), 2, 0)

    def step(carry, inp):
        C, n, m = carry
        qc, kc, vc, ic, fc = (a.astype(jnp.float32) for a in inp)
        cb = jnp.cumsum(fc, axis=-1)
        dmat = jnp.where(mask, cb[..., :, None] - cb[..., None, :] + ic[..., None, :], NEG_BIG)
        inter = cb + m[..., None]
        mt = jnp.maximum(inter, jnp.max(dmat, axis=-1))
        scores = jnp.einsum('bhtd,bhsd->bhts', qc, kc) * jnp.exp(dmat - mt[..., None])
        e_inter = jnp.exp(inter - mt)
        num = jnp.einsum('bhts,bhse->bhte', scores, vc) + e_inter[..., None] * jnp.einsum('bhtd,bhde->bhte', qc, C)
        den = jnp.sum(scores, axis=-1) + e_inter * jnp.einsum('bhtd,bhd->bht', qc, n)
        hc = num / jnp.maximum(jnp.abs(den), jnp.exp(-mt))[..., None]
        cl = cb[..., -1]
        w_log = cl[..., None] - cb + ic
        m_new = jnp.maximum(cl + m, jnp.max(w_log, axis=-1))
        ws = jnp.exp(w_log - m_new[..., None])
        ec = jnp.exp(cl + m - m_new)
        C_new = ec[..., None, None] * C + jnp.einsum('bhs,bhsd,bhse->bhde', ws, kc, vc)
        n_new = ec[..., None] * n + jnp.einsum('bhs,bhsd->bhd', ws, kc)
        return (C_new, n_new, m_new), hc.astype(v.dtype)

    state, hs = lax.scan(step, state, tuple(chunks(a) for a in (q, k, v, ig, lf)))
    return jnp.moveaxis(hs, 0, 2).reshape(b, h, t, d), state


def _hgrn_dir(q, k, v, lf, S):
    b, h, t, dk = q.shape
    dv = v.shape[-1]
    L = C_CHUNK
    nc = t // L
    mask = jnp.tril(jnp.ones((L, L), bool))[:, :, None]

    def chunks(a):
        return jnp.moveaxis(a.reshape(b, h, nc, L, a.shape[-1]), 2, 0)

    def step(S, inp):
        qc, kc, vc, fc = (a.astype(jnp.float32) for a in inp)
        cb = jnp.cumsum(fc, axis=2)
        decay = jnp.exp(jnp.where(mask, cb[:, :, :, None, :] - cb[:, :, None, :, :], NEG_BIG))
        scores = jnp.einsum('bhtc,bhsc,bhtsc->bhts', qc, kc, decay)
        o = jnp.einsum('bhts,bhse->bhte', scores, vc) + jnp.einsum('bhtc,bhce->bhte', qc * jnp.exp(cb), S)
        cl = cb[:, :, -1:, :]
        S_new = jnp.exp(cl[:, :, 0, :])[..., None] * S + jnp.einsum('bhsc,bhse->bhce', kc * jnp.exp(cl - cb), vc)
        return S_new, o.astype(q.dtype)

    S, os_ = lax.scan(step, S, tuple(chunks(a) for a in (q, k, v, lf)))
    return jnp.moveaxis(os_, 0, 2).reshape(b, h, t, dv), S


def _mlstm_branch(pc, pl, conv_w, norm_g, last):
    def prep(p):
        qk = jax.nn.silu(_conv_centred(jnp.concatenate([p['b_q'], p['b_k']], axis=-1), conv_w))
        q = _to_heads(qk[..., :B_WIDTH], B_HEADS)
        k = _to_heads(qk[..., B_WIDTH:], B_HEADS) * (B_HEAD_DIM ** -0.5)
        v = _to_heads(p['b_v'], B_HEADS)
        gate = lambda name: jnp.swapaxes(p[name].astype(jnp.float32), 1, 2)
        fl = lambda a: jnp.flip(a, 2)
        fwd = (q, k, v, gate('b_i_fwd'), jax.nn.log_sigmoid(gate('b_f_fwd')))
        bwd = (fl(q), fl(k), fl(v), fl(gate('b_i_bwd')), fl(jax.nn.log_sigmoid(gate('b_f_bwd'))))
        return fwd, bwd

    ctx_f, ctx_b = prep(pc)
    lat_f, lat_b = prep(pl)
    s0 = _mlstm_zero_state(pl['b_q'].shape[0])
    hcf, scf = _mlstm_dir(*ctx_f, s0)
    hcb, scb = _mlstm_dir(*ctx_b, s0)
    hlf, _ = _mlstm_dir(*lat_f, scf)
    hlb, _ = _mlstm_dir(*lat_b, scb)

    def readout(hf, hb_rev, p):
        hsum = _from_heads(hf + jnp.flip(hb_rev, 2))
        return _head_rmsnorm(hsum, norm_g, B_HEADS) * jax.nn.sigmoid(p['b_o']) * jax.nn.silu(p['b_z'])

    y_lat = readout(hlf, hlb, pl)
    y_ctx = None if last else readout(hcf, hcb, pc)
    return y_ctx, y_lat


def _hgrn_branch(pc, pl, rows, lb, norm_g, last):
    lb_h = lb.reshape(C_HEADS, 1, C_KEY_DIM)
    log_lb = jnp.log(jnp.maximum(lb_h, LB_FLOOR))
    log_1m_lb = jnp.log1p(-lb_h)

    def gate(logit):
        z = _to_heads(logit, C_HEADS).astype(jnp.float32)
        return (1.0 - lb_h) * jax.nn.sigmoid(-z), jnp.logaddexp(log_lb, log_1m_lb + jax.nn.log_sigmoid(z))

    def prep(p, order):
        q = _to_heads(jax.nn.silu(order(p['c_q'])), C_HEADS)
        v = _to_heads(order(p['c_i']), C_HEADS)
        kf, lff = gate(order(p['c_f_fwd']))
        kb, lfb = gate(order(p['c_f_bwd']))
        fl = lambda a: jnp.flip(a, 2)
        return (q, kf, v, lff), (fl(q), fl(kb), fl(v), fl(lfb))

    ident = lambda a: a
    ctx_f, ctx_b = prep(pc, ident)
    lat_f, lat_b = prep(pl, lambda a: _to_colmajor(a, rows))
    S0 = jnp.zeros((pl['c_q'].shape[0], C_HEADS, C_KEY_DIM, C_VAL_DIM), jnp.float32)
    ocf, scf = _hgrn_dir(*ctx_f, S0)
    ocb, scb = _hgrn_dir(*ctx_b, S0)
    olf, _ = _hgrn_dir(*lat_f, scf)
    olb, _ = _hgrn_dir(*lat_b, scb)

    def readout(of, ob_rev, p, unorder):
        o = unorder(_from_heads(of + jnp.flip(ob_rev, 2)))
        return _head_rmsnorm(o, norm_g, C_HEADS) * jax.nn.silu(p['c_g'])

    y_lat = readout(olf, olb, pl, lambda a: _from_colmajor(a, rows))
    y_ctx = None if last else readout(ocf, ocb, pc, ident)
    return y_ctx, y_lat


def _mixer(hc, hl, rows, w_in, b_in, w_s, b_s, conv_w, m_norm, lb, h_norm, w_out, last):
    pc = _split_cols(hc @ w_in + b_in)
    pl = _split_cols(hl @ w_in + b_in)
    ya_l = _chunk_mlp(pl['a_u'], pl['a_v'], pl['a_z'], w_s, b_s)
    yb_c, yb_l = _mlstm_branch(pc, pl, conv_w, m_norm, last)
    yc_c, yc_l = _hgrn_branch(pc, pl, rows, lb, h_norm, last)
    y_lat = jnp.concatenate([ya_l, yb_l, yc_l], axis=-1) @ w_out
    if last:
        return None, y_lat
    ya_c = _chunk_mlp(pc['a_u'], pc['a_v'], pc['a_z'], w_s, b_s)
    y_ctx = jnp.concatenate([ya_c, yb_c, yc_c], axis=-1) @ w_out
    return y_ctx, y_lat


def setup_inputs(seed: int = 0) -> dict:
    key = jax.random.key(seed)
    ks = jax.random.split(key, 20)
    nrm = jax.random.normal
    x = nrm(ks[0], (BATCH, SEQ, D_MODEL), jnp.float32)
    c = nrm(ks[1], (BATCH, D_MODEL), jnp.float32)
    ctx = nrm(ks[2], (BATCH, CTX_LEN, D_MODEL), jnp.float32)
    c_ctx = nrm(ks[3], (D_MODEL,), jnp.float32)
    w_ada = nrm(ks[4], (DEPTH, D_MODEL, 3 * D_MODEL), jnp.float32) * (0.5 * D_MODEL ** -0.5)
    b_ada = 0.02 * nrm(ks[5], (DEPTH, 3 * D_MODEL), jnp.float32)
    norm_g = 1.0 + 0.1 * nrm(ks[6], (DEPTH, D_MODEL), jnp.float32)
    w_in = nrm(ks[7], (DEPTH, D_MODEL, D_IN), jnp.float32) * (D_MODEL ** -0.5)
    b_in = 0.02 * nrm(ks[8], (DEPTH, D_IN), jnp.float32)
    f_off = jnp.linspace(FORGET_BIAS_LO, FORGET_BIAS_HI, B_HEADS, dtype=jnp.float32)
    for name in ('b_f_fwd', 'b_f_bwd'):
        s = _col_start(name)
        b_in = b_in.at[:, s:s + B_HEADS].add(f_off)
    w_spatial = nrm(ks[9], (DEPTH, A_GROUPS, A_CHUNK, A_CHUNK), jnp.float32) * (A_CHUNK ** -0.5)
    b_spatial = 1.0 + 0.1 * nrm(ks[10], (DEPTH, A_GROUPS, A_CHUNK), jnp.float32)
    conv_qk = nrm(ks[11], (DEPTH, CONV_W, 2 * B_WIDTH), jnp.float32) * (CONV_W ** -0.5)
    mlstm_norm = 1.0 + 0.1 * nrm(ks[12], (DEPTH, B_WIDTH), jnp.float32)
    hgrn_lb_logits = 0.1 * nrm(ks[13], (DEPTH, C_KEY_WIDTH), jnp.float32)
    hgrn_norm = 1.0 + 0.1 * nrm(ks[14], (DEPTH, C_WIDTH), jnp.float32)
    w_out = nrm(ks[15], (DEPTH, D_MIX, D_MODEL), jnp.float32) * (D_MIX ** -0.5)
    final_norm = 1.0 + 0.1 * nrm(ks[16], (D_MODEL,), jnp.float32)
    return {'x': x, 'c': c, 'ctx': ctx, 'c_ctx': c_ctx, 'w_ada': w_ada, 'b_ada': b_ada,
            'norm_g': norm_g, 'w_in': w_in, 'b_in': b_in, 'w_spatial': w_spatial,
            'b_spatial': b_spatial, 'conv_qk': conv_qk, 'mlstm_norm': mlstm_norm,
            'hgrn_lb_logits': hgrn_lb_logits, 'hgrn_norm': hgrn_norm, 'w_out': w_out,
            'final_norm': final_norm}


def reference(x, c, ctx, c_ctx, w_ada, b_ada, norm_g, w_in, b_in, w_spatial, b_spatial,
              conv_qk, mlstm_norm, hgrn_lb_logits, hgrn_norm, w_out, final_norm):
    rows = x.shape[1] // GRID_W
    p_lb = jax.nn.softmax(hgrn_lb_logits.astype(jnp.float32), axis=0)
    lower_bounds = jnp.cumsum(p_lb, axis=0) - p_lb[0]
    s_lat = jax.nn.silu(c)
    s_ctx = jax.nn.silu(c_ctx)
    h_lat, h_ctx = x, ctx
    for l in range(DEPTH):
        last = l == DEPTH - 1
        mod_l = (s_lat @ w_ada[l] + b_ada[l])[:, None, :]
        mod_c = s_ctx @ w_ada[l] + b_ada[l]
        sh_l, sc_l, g_l = jnp.split(mod_l, 3, axis=-1)
        sh_c, sc_c, g_c = jnp.split(mod_c, 3, axis=-1)
        n_lat = _rmsnorm(h_lat, norm_g[l]) * (1 + sc_l) + sh_l
        n_ctx = _rmsnorm(h_ctx, norm_g[l]) * (1 + sc_c) + sh_c
        y_ctx, y_lat = _mixer(n_ctx, n_lat, rows, w_in[l], b_in[l], w_spatial[l], b_spatial[l],
                              conv_qk[l], mlstm_norm[l], lower_bounds[l], hgrn_norm[l], w_out[l], last)
        h_lat = h_lat + g_l * y_lat
        if not last:
            h_ctx = h_ctx + g_c * y_ctx
    return _rmsnorm(h_lat, final_norm)
```

```python
import functools

import jax
import jax.numpy as jnp
from jax import lax
from jax.experimental import pallas as pl
from jax.experimental.pallas import tpu as pltpu

EPS = 1e-6
NEG_BIG = -1e30
LB_FLOOR = 1e-30
GRID_W = 64
CONV_W = 3

A_GROUPS = 4
A_CHUNK = 128
B_HEADS = 4
B_CHUNK = 128
C_HEADS = 4
C_KEY_DIM = 128
C_CHUNK = 64
C_SUB = 8
GATE_PAD = 128

V7X_VMEM_LIMIT = 56 * 1024 * 1024

F32 = jnp.float32
BF16 = jnp.bfloat16


def _params(*sem):
    return pltpu.CompilerParams(dimension_semantics=sem, vmem_limit_bytes=V7X_VMEM_LIMIT)


def _sigmoid(x):
    return 1.0 / (1.0 + jnp.exp(-x))


def _silu(x):
    return x * _sigmoid(x)


def _log_sigmoid(x):
    return jnp.minimum(x, 0.0) - jnp.log1p(jnp.exp(-jnp.abs(x)))


def _dot(a, b):
    return jnp.dot(a, b, preferred_element_type=F32)


def _dot_nt(a, b):
    return lax.dot_general(a, b, (((1,), (1,)), ((), ())), preferred_element_type=F32)


def _dot_tn(a, b):
    return lax.dot_general(a, b, (((0,), (0,)), ((), ())), preferred_element_type=F32)


def _lb_kernel(x_ref, lbf_ref, oml_ref):
    x = x_ref[...]
    depth = x.shape[0]
    e = jnp.exp(x - jnp.max(x, axis=0, keepdims=True))
    p = e / jnp.sum(e, axis=0, keepdims=True)
    rows = lax.broadcasted_iota(jnp.int32, x.shape, 0)
    lb = jnp.zeros_like(x)
    for j in range(1, depth):
        lb = lb + jnp.where(rows >= j, p[j:j + 1, :], 0.0)
    lbf_ref[...] = jnp.maximum(lb, LB_FLOOR)
    oml_ref[...] = 1.0 - lb


def _lower_bounds(logits):
    shp = jax.ShapeDtypeStruct(logits.shape, F32)
    return pl.pallas_call(_lb_kernel, out_shape=(shp, shp), name="hgrn_lower_bounds")(logits)


def _ada_kernel(c_ref, w_ref, b_ref, o_ref):
    s = _silu(c_ref[...]).astype(BF16)
    o_ref[0] = _dot(s, w_ref[0].astype(BF16)) + b_ref[0]


def _ada_mod(cond, w_ada, b_ada, tn=768):
    depth, d, n3 = w_ada.shape
    r = cond.shape[0]
    return pl.pallas_call(
        _ada_kernel,
        grid=(depth, n3 // tn),
        in_specs=[pl.BlockSpec((r, d), lambda l, j: (0, 0)),
                  pl.BlockSpec((1, d, tn), lambda l, j: (l, 0, j)),
                  pl.BlockSpec((1, 1, tn), lambda l, j: (l, 0, j))],
        out_specs=pl.BlockSpec((1, r, tn), lambda l, j: (l, 0, j)),
        out_shape=jax.ShapeDtypeStruct((depth, r, n3), F32),
        compiler_params=_params("parallel", "parallel"),
        name="adaln_mod",
    )(cond, w_ada, b_ada.reshape(depth, 1, n3))


def _modulated_norm(h, g, mod, d):
    y = h * lax.rsqrt(jnp.mean(h * h, axis=-1, keepdims=True) + EPS) * g
    return y * (1.0 + mod[:, d:2 * d]) + mod[:, 0:d]


def _norm_kernel(h_ref, g_ref, mod_ref, n_ref):
    d = h_ref.shape[-1]
    n_ref[0] = _modulated_norm(h_ref[0], g_ref[...], mod_ref[0], d).astype(n_ref.dtype)


def _mod_index(tiles_lat):
    return lambda b, j: (2 * b + jnp.where(j >= tiles_lat, 1, 0), 0, 0)


def _norm_mod(h, g, mod, t_lat, tm=256):
    b, tt, d = h.shape
    return pl.pallas_call(
        _norm_kernel,
        grid=(b, tt // tm),
        in_specs=[pl.BlockSpec((1, tm, d), lambda i, j: (i, j, 0)),
                  pl.BlockSpec((1, d), lambda i, j: (0, 0)),
                  pl.BlockSpec((1, 1, 3 * d), _mod_index(t_lat // tm))],
        out_specs=pl.BlockSpec((1, tm, d), lambda i, j: (i, j, 0)),
        out_shape=jax.ShapeDtypeStruct((b, tt, d), BF16),
        compiler_params=_params("parallel", "parallel"),
        name="norm_mod",
    )(h, g, mod)


def _matmul_bias_kernel(x_ref, w_ref, b_ref, o_ref):
    o_ref[...] = (_dot(x_ref[...], w_ref[...]) + b_ref[...]).astype(o_ref.dtype)


def _tile(m, pref, unit=128):
    t = min(pref, m) // unit * unit
    while m % t:
        t -= unit
    return t


def _in_proj(n2d, w, bias, tm, tn, name):
    m, d = n2d.shape
    n = w.shape[1]
    tm = _tile(m, tm)
    return pl.pallas_call(
        _matmul_bias_kernel,
        grid=(n // tn, m // tm),
        in_specs=[pl.BlockSpec((tm, d), lambda j, i: (i, 0)),
                  pl.BlockSpec((d, tn), lambda j, i: (0, j)),
                  pl.BlockSpec((1, tn), lambda j, i: (0, j))],
        out_specs=pl.BlockSpec((tm, tn), lambda j, i: (i, j)),
        out_shape=jax.ShapeDtypeStruct((m, n), F32),
        compiler_params=_params("parallel", "parallel"),
        name=name,
    )(n2d, w, bias)


def _chunk_mlp_kernel(u_ref, v_ref, z_ref, ws_ref, bs_ref, y_ref):
    tm, width = v_ref.shape
    gd = width // A_GROUPS
    for c in range(tm // A_CHUNK):
        rows = slice(c * A_CHUNK, (c + 1) * A_CHUNK)
        for g in range(A_GROUPS):
            cols = slice(g * gd, (g + 1) * gd)
            v = v_ref[rows, cols]
            mu = jnp.mean(v, axis=-1, keepdims=True)
            vc = v - mu
            var = jnp.mean(vc * vc, axis=-1, keepdims=True)
            vn = (vc * lax.rsqrt(var + EPS)).astype(BF16)
            mixed = _dot(ws_ref[g], vn) + bs_ref[:, g:g + 1]
            y_ref[rows, cols] = (u_ref[rows, cols] * mixed * _silu(z_ref[rows, cols])).astype(y_ref.dtype)


def _chunk_mlp(p2d, ws, bs_t, col_u, width, tm=1024):
    m = p2d.shape[0]
    tm = _tile(m, tm)
    cb = col_u // width
    spec = lambda k: pl.BlockSpec((tm, width), lambda i, k=k: (i, cb + k))
    return pl.pallas_call(
        _chunk_mlp_kernel,
        grid=(m // tm,),
        in_specs=[spec(0), spec(1), spec(2),
                  pl.BlockSpec(ws.shape, lambda i: (0, 0, 0)),
                  pl.BlockSpec(bs_t.shape, lambda i: (0, 0))],
        out_specs=pl.BlockSpec((tm, width), lambda i: (i, 0)),
        out_shape=jax.ShapeDtypeStruct((m, width), BF16),
        compiler_params=_params("parallel"),
        name="chunk_mlp",
    )(p2d, p2d, p2d, ws, bs_t)


def _conv_kernel(x_ref, w_ref, o_ref, *, t_lat, k_blocks_from, k_scale):
    x = x_ref[0]
    tt = x.shape[0]
    rows = lax.broadcasted_iota(jnp.int32, x.shape, 0)
    prev = jnp.where((rows == 0) | (rows == t_lat), 0.0, pltpu.roll(x, 1, axis=0))
    nxt = jnp.where((rows == t_lat - 1) | (rows == tt - 1), 0.0, pltpu.roll(x, tt - 1, axis=0))
    w = w_ref[...]
    y = _silu(w[0:1] * prev + w[1:2] * x + w[2:3] * nxt)
    scale = jnp.where(pl.program_id(1) >= k_blocks_from, k_scale, 1.0)
    o_ref[0] = (y * scale).astype(o_ref.dtype)


def _conv_qk(p3d, conv_w, t_lat, head_dim, tc=256):
    b, tt, _ = p3d.shape
    width = conv_w.shape[1]
    kern = functools.partial(_conv_kernel, t_lat=t_lat, k_blocks_from=(width // 2) // tc,
                             k_scale=head_dim ** -0.5)
    return pl.pallas_call(
        kern,
        grid=(b, width // tc),
        in_specs=[pl.BlockSpec((1, tt, tc), lambda i, j: (i, 0, j)),
                  pl.BlockSpec((CONV_W, tc), lambda i, j: (0, j))],
        out_specs=pl.BlockSpec((1, tt, tc), lambda i, j: (i, 0, j)),
        out_shape=jax.ShapeDtypeStruct((b, tt, width), BF16),
        compiler_params=_params("parallel", "parallel"),
        name="conv_qk",
    )(p3d, conv_w)


def _mlstm_kernel(*refs, reverse, readout, hd):
    if readout:
        (q_ref, k_ref, v_ref, gc_ref, gr_ref, hf_ref, o_ref, z_ref, g_ref,
         out_ref, c_scr, n_scr, m_scr) = refs
    else:
        q_ref, k_ref, v_ref, gc_ref, gr_ref, out_ref, c_scr, n_scr, m_scr = refs

    @pl.when(pl.program_id(1) == 0)
    def _():
        c_scr[...] = jnp.zeros_like(c_scr)
        n_scr[...] = jnp.zeros_like(n_scr)
        m_scr[...] = jnp.zeros_like(m_scr)

    L = B_CHUNK
    row = lax.broadcasted_iota(jnp.int32, (L, L), 0)
    col = lax.broadcasted_iota(jnp.int32, (L, L), 1)
    seen = (col >= row) if reverse else (col <= row)
    seen_t = (row >= col) if reverse else (row <= col)
    last = 0 if reverse else L - 1
    gi, gf = (2 * B_HEADS, 3 * B_HEADS) if reverse else (0, B_HEADS)
    gc = gc_ref[0]
    gr = gr_ref[0]
    for h in range(B_HEADS):
        cols = slice(h * hd, (h + 1) * hd)
        q = q_ref[0, :, cols]
        k = k_ref[0, :, cols]
        v = v_ref[0, :, cols].astype(BF16)
        i_col = gc[:, gi + h:gi + h + 1]
        f_col = _log_sigmoid(gc[:, gf + h:gf + h + 1])
        i_row = gr[gi + h:gi + h + 1, :]
        f_row = _log_sigmoid(gr[gf + h:gf + h + 1, :])
        cb_col = jnp.sum(jnp.where(seen, f_row, 0.0), axis=1, keepdims=True)
        cb_row = jnp.sum(jnp.where(seen_t, f_col, 0.0), axis=0, keepdims=True)
        dmat = jnp.where(seen, cb_col - cb_row + i_row, NEG_BIG)
        m = m_scr[h, 0:1, 0:1]
        inter = cb_col + m
        mt = jnp.maximum(inter, jnp.max(dmat, axis=1, keepdims=True))
        scores = _dot_nt(q, k) * jnp.exp(dmat - mt)
        e_inter = jnp.exp(inter - mt)
        c_state = c_scr[h]
        n_state = n_scr[h]
        num = _dot(scores.astype(BF16), v) + e_inter * _dot(q, c_state.astype(BF16))
        den = (jnp.sum(scores, axis=1, keepdims=True)
               + e_inter * jnp.sum(q.astype(F32) * n_state, axis=1, keepdims=True))
        hc = num / jnp.maximum(jnp.abs(den), jnp.exp(-mt))
        cl = cb_col[last:last + 1, :]
        w_log = cl - cb_col + i_col
        m_new = jnp.maximum(cl + m, jnp.max(w_log, axis=0, keepdims=True))
        ec = jnp.exp(cl + m - m_new)
        kw = k.astype(F32) * jnp.exp(w_log - m_new)
        c_scr[h] = ec * c_state + _dot_tn(kw.astype(BF16), v)
        n_scr[h] = ec * n_state + jnp.sum(kw, axis=0, keepdims=True)
        m_scr[h] = jnp.broadcast_to(m_new, m_scr.shape[1:])
        if readout:
            hs = hf_ref[0, :, cols] + hc
            hn = hs * lax.rsqrt(jnp.mean(hs * hs, axis=-1, keepdims=True) + EPS) * g_ref[:, cols]
            y = hn * _sigmoid(o_ref[0, :, cols]) * _silu(z_ref[0, :, cols])
            out_ref[0, :, cols] = y.astype(out_ref.dtype)
        else:
            out_ref[0, :, cols] = hc.astype(out_ref.dtype)


def _mlstm_dir(qk, p3d, gates_col, gates_row, t_lat, cols, reverse, hf=None, norm_g=None):
    b, tt, _ = p3d.shape
    width = qk.shape[-1] // 2
    hd = width // B_HEADS
    L = B_CHUNK
    nch, nlat = tt // L, t_lat // L
    if reverse:
        chunk = lambda i: nch - 1 - i
    else:
        chunk = lambda i: lax.rem(i + nlat, nch)
    wb = lambda name: cols[name] // width
    tok = lambda cb: pl.BlockSpec((1, L, width), lambda bi, i, cb=cb: (bi, chunk(i), cb))
    in_specs = [tok(0), tok(1), tok(wb("b_v")),
                pl.BlockSpec((1, L, GATE_PAD), lambda bi, i: (bi, chunk(i), 0)),
                pl.BlockSpec((1, gates_row.shape[1], L), lambda bi, i: (bi, 0, chunk(i)))]
    args = [qk, qk, p3d, gates_col, gates_row]
    readout = hf is not None
    if readout:
        in_specs += [tok(0), tok(wb("b_o")), tok(wb("b_z")), pl.BlockSpec((1, width), lambda bi, i: (0, 0))]
        args += [hf, p3d, p3d, norm_g]
    kern = functools.partial(_mlstm_kernel, reverse=reverse, readout=readout, hd=hd)
    return pl.pallas_call(
        kern,
        grid=(b, nch),
        in_specs=in_specs,
        out_specs=tok(0),
        out_shape=jax.ShapeDtypeStruct((b, tt, width), BF16 if readout else F32),
        scratch_shapes=[pltpu.VMEM((B_HEADS, hd, hd), F32),
                        pltpu.VMEM((B_HEADS, 1, hd), F32),
                        pltpu.VMEM((B_HEADS, 8, 128), F32)],
        compiler_params=_params("parallel", "arbitrary"),
        name="mlstm_bwd_readout" if readout else "mlstm_fwd",
    )(*args)


def _hgrn_kernel(*refs, reverse, readout, dv):
    if readout:
        (q_ref, z_ref, v_ref, lbf_ref, oml_ref, s0_ref, e_ref, of_ref, cg_ref, g_ref,
         out_ref, s_out_ref, s_scr) = refs
    else:
        q_ref, z_ref, v_ref, lbf_ref, oml_ref, s0_ref, e_ref, out_ref, s_out_ref, s_scr = refs
    step = pl.program_id(1)

    @pl.when(step == 0)
    def _():
        s_scr[...] = s0_ref[0]

    L, dk, nb = C_CHUNK, C_KEY_DIM, C_CHUNK // C_SUB
    row = lax.broadcasted_iota(jnp.int32, (L, L), 0)
    col = lax.broadcasted_iota(jnp.int32, (L, L), 1)
    rb, cbk = row // C_SUB, col // C_SUB
    if reverse:
        seen, blk_before, last = col >= row, cbk > rb, 0
    else:
        seen, blk_before, last = col <= row, cbk < rb, L - 1
    sum_mat = jnp.concatenate([jnp.where(seen, 1.0, 0.0), jnp.where(blk_before, 1.0, 0.0)], axis=0).astype(BF16)
    same_blk = rb == cbk
    sub = lax.broadcasted_iota(jnp.int32, (L, dk), 0) % C_SUB

    def bcast_sub(x, j):
        x3 = x.reshape(nb, C_SUB, x.shape[-1])
        return jnp.broadcast_to(x3[:, j:j + 1, :], x3.shape).reshape(x.shape)

    for h in range(C_HEADS):
        kc = slice(h * dk, (h + 1) * dk)
        vcols = slice(h * dv, (h + 1) * dv)
        q = _silu(q_ref[0, :, kc])
        z = z_ref[0, :, kc]
        v = v_ref[0, :, vcols].astype(BF16)
        a = jnp.exp(-jnp.abs(z))
        r = 1.0 / (1.0 + a)
        pos = z >= 0.0
        oml = oml_ref[:, kc]
        f = lbf_ref[:, kc] + oml * jnp.where(pos, r, a * r)
        k = oml * jnp.where(pos, a * r, r)
        lf = jnp.log(f)
        hi = lf.astype(BF16)
        r1 = lf - hi.astype(F32)
        mid = r1.astype(BF16)
        lo = (r1 - mid.astype(F32)).astype(BF16)
        c3 = _dot(sum_mat, jnp.concatenate([hi, mid, lo], axis=1))
        c1 = c3[:, 0:dk] + c3[:, dk:2 * dk] + c3[:, 2 * dk:3 * dk]
        cb, entry = c1[0:L], c1[L:2 * L]
        qd = (q * jnp.exp(cb - entry)).astype(BF16)
        parts = []
        for i in range(nb):
            kd = (k * jnp.exp(jnp.minimum(entry[i * C_SUB:i * C_SUB + 1, :] - cb, 0.0))).astype(BF16)
            parts.append(_dot_nt(qd[i * C_SUB:(i + 1) * C_SUB], kd))
        a_off = jnp.concatenate(parts, axis=0)
        qk_parts = []
        for j in range(C_SUB):
            ok = (sub <= j) if reverse else (sub >= j)
            dec = jnp.exp(jnp.where(ok, cb - bcast_sub(cb, j), NEG_BIG))
            qk_parts.append((q * bcast_sub(k, j) * dec).astype(BF16))
        a_diag = _dot(jnp.concatenate(qk_parts, axis=1), e_ref[...])
        scores = jnp.where(blk_before, a_off, 0.0) + jnp.where(same_blk, a_diag, 0.0)
        st = s_scr[h]
        o = _dot(scores.astype(BF16), v) + _dot_nt((q * jnp.exp(cb)).astype(BF16), st.astype(BF16))
        cl = cb[last:last + 1, :]
        kdec = (k * jnp.exp(cl - cb)).astype(BF16)
        s_scr[h] = st * jnp.exp(cl) + _dot_tn(v, kdec)
        if readout:
            os_ = of_ref[0, :, vcols] + o
            on = os_ * lax.rsqrt(jnp.mean(os_ * os_, axis=-1, keepdims=True) + EPS) * g_ref[:, vcols]
            out_ref[0, :, vcols] = (on * _silu(cg_ref[0, :, vcols])).astype(out_ref.dtype)
        else:
            out_ref[0, :, vcols] = o.astype(out_ref.dtype)

    @pl.when(step == pl.num_programs(1) - 1)
    def _():
        s_out_ref[0] = s_scr[...]


def _hgrn_dir(p3d, lbf, oml, s0, e_mat, cols, segment, t_lat, reverse, of=None, norm_g=None):
    b, tt, n_cols = p3d.shape
    kw, L = C_HEADS * C_KEY_DIM, C_CHUNK
    vw = cols["c_g"] - cols["c_i"]
    dv = vw // C_HEADS
    if segment == "lat":
        assert t_lat // GRID_W == L and tt % GRID_W == 0
        steps, t_seg = GRID_W, t_lat
        order = (lambda i: steps - 1 - i) if reverse else (lambda i: i)
        view = lambda x: x.reshape(b, x.shape[1] // GRID_W, GRID_W * x.shape[-1])
        spec = lambda feat, c0, w, full: pl.BlockSpec(
            (1, L, w), lambda bi, i: (bi, 0, order(i) * (feat // w) + c0 // w))
    else:
        t_seg = tt - t_lat
        steps, first = t_seg // L, t_lat // L
        order = (lambda i: steps - 1 - i) if reverse else (lambda i: i)
        view = lambda x: x
        spec = lambda feat, c0, w, full: pl.BlockSpec(
            (1, L, w), lambda bi, i: (bi, order(i) + (first if full else 0), c0 // w))
    f_name = "c_f_bwd" if reverse else "c_f_fwd"
    const2 = lambda x: pl.BlockSpec(x.shape, lambda bi, i: (0, 0))
    state_spec = pl.BlockSpec((1,) + s0.shape[1:], lambda bi, i: (bi, 0, 0, 0))
    in_specs = [spec(n_cols, cols["c_q"], kw, True), spec(n_cols, cols[f_name], kw, True),
                spec(n_cols, cols["c_i"], vw, True), const2(lbf), const2(oml), state_spec, const2(e_mat)]
    args = [view(p3d), view(p3d), view(p3d), lbf, oml, s0, e_mat]
    readout = of is not None
    if readout:
        in_specs += [spec(vw, 0, vw, False), spec(n_cols, cols["c_g"], vw, True), const2(norm_g)]
        args += [view(of), view(p3d), norm_g]
    out_dtype = BF16 if readout else F32
    seg_shape = (b, t_seg // GRID_W, GRID_W * vw) if segment == "lat" else (b, t_seg, vw)
    kern = functools.partial(_hgrn_kernel, reverse=reverse, readout=readout, dv=dv)
    o, s_out = pl.pallas_call(
        kern,
        grid=(b, steps),
        in_specs=in_specs,
        out_specs=(spec(vw, 0, vw, False), state_spec),
        out_shape=(jax.ShapeDtypeStruct(seg_shape, out_dtype), jax.ShapeDtypeStruct(s0.shape, F32)),
        scratch_shapes=[pltpu.VMEM(s0.shape[1:], F32)],
        compiler_params=_params("parallel", "arbitrary"),
        name=f"hgrn_{segment}_{'bwd' if reverse else 'fwd'}",
    )(*args)
    return o.reshape(b, t_seg, vw), s_out


def _out_kernel(ya_ref, yb_ref, yc_ref, w_ref, h_ref, mod_ref, *rest, final, wa, wb):
    d = h_ref.shape[-1]
    w = w_ref
    y = (_dot(ya_ref[0], w[0:wa, :]) + _dot(yb_ref[0], w[wa:wa + wb, :]) + _dot(yc_ref[0], w[wa + wb:, :]))
    h_new = h_ref[0] + mod_ref[0][:, 2 * d:3 * d] * y
    if final:
        g_ref, out_ref = rest
        out_ref[0] = h_new * lax.rsqrt(jnp.mean(h_new * h_new, axis=-1, keepdims=True) + EPS) * g_ref[...]
    else:
        g_ref, modn_ref, h_out_ref, n_ref = rest
        h_out_ref[0] = h_new
        n_ref[0] = _modulated_norm(h_new, g_ref[...], modn_ref[0], d).astype(n_ref.dtype)


def _out_proj(ya, yb, yc, w_out, h, mod, g_next, mod_next, t_lat, final, tm=256):
    b, tt, d = h.shape
    wa, wb, wc = ya.shape[-1], yb.shape[-1], yc.shape[-1]
    tok = lambda w: pl.BlockSpec((1, tm, w), lambda i, j: (i, j, 0))
    mod_spec = pl.BlockSpec((1, 1, 3 * d), _mod_index(t_lat // tm))
    in_specs = [tok(wa), tok(wb), tok(wc), pl.BlockSpec(w_out.shape, lambda i, j: (0, 0)), tok(d), mod_spec,
                pl.BlockSpec((1, d), lambda i, j: (0, 0))]
    args = [ya, yb, yc, w_out, h, mod, g_next]
    kern = functools.partial(_out_kernel, final=final, wa=wa, wb=wb)
    if final:
        return pl.pallas_call(
            kern, grid=(b, t_lat // tm), in_specs=in_specs, out_specs=tok(d),
            out_shape=jax.ShapeDtypeStruct((b, t_lat, d), F32),
            compiler_params=_params("parallel", "parallel"), name="out_proj_final",
        )(*args)
    return pl.pallas_call(
        kern, grid=(b, tt // tm), in_specs=in_specs + [mod_spec], out_specs=(tok(d), tok(d)),
        out_shape=(jax.ShapeDtypeStruct((b, tt, d), F32), jax.ShapeDtypeStruct((b, tt, d), BF16)),
        compiler_params=_params("parallel", "parallel"), name="out_proj",
    )(*args, mod_next)


def _packed_layout(d):
    a, bw, c, kq = d // 4, d // 2, d // 4, C_HEADS * C_KEY_DIM
    ref_order = (("a_u", a), ("a_v", a), ("a_z", a), ("b_q", bw), ("b_k", bw), ("b_v", bw), ("b_o", bw),
                 ("b_z", bw), ("gates", 4 * B_HEADS), ("c_q", kq), ("c_f_fwd", kq), ("c_f_bwd", kq),
                 ("c_i", c), ("c_g", c))
    src, start = {}, 0
    for name, w in ref_order:
        src[name] = (start, w)
        start += w
    packed = ("b_q", "b_k", "b_v", "b_o", "b_z", "a_u", "a_v", "a_z", "c_q", "c_f_fwd", "c_f_bwd", "c_i", "c_g")
    cols, pos = {}, 0
    for name in packed:
        cols[name] = pos
        pos += src[name][1]
    return src, packed, cols, pos


def kernel(x, c, ctx, c_ctx, w_ada, b_ada, norm_g, w_in, b_in, w_spatial, b_spatial, conv_qk, mlstm_norm,
           hgrn_lb_logits, hgrn_norm, w_out, final_norm):
    b, t_lat, d = x.shape
    t_ctx = ctx.shape[1]
    tt = t_lat + t_ctx
    depth = w_ada.shape[0]
    src, packed, cols, n_main = _packed_layout(d)
    a_width, b_width = d // 4, d // 2
    dv = (d // 4) // C_HEADS

    r_pad = -(-(b + 1) // 8) * 8
    cond = jnp.concatenate([c, c_ctx[None, :], jnp.zeros((r_pad - b - 1, d), F32)], axis=0)
    mod_all = _ada_mod(cond, w_ada, b_ada)
    mods = [jnp.stack([mod_all[l, :b], jnp.broadcast_to(mod_all[l, b], (b, 3 * d))], axis=1).reshape(2 * b, 1, 3 * d)
            for l in range(depth)]

    lbf, oml = _lower_bounds(hgrn_lb_logits.astype(F32))

    e_rows = lax.broadcasted_iota(jnp.int32, (C_SUB * C_KEY_DIM, C_CHUNK), 0) // C_KEY_DIM
    e_cols = lax.broadcasted_iota(jnp.int32, (C_SUB * C_KEY_DIM, C_CHUNK), 1) % C_SUB
    e_mat = (e_rows == e_cols).astype(BF16)

    h = jnp.concatenate([x, ctx], axis=1)
    n = _norm_mod(h, norm_g[0:1], mods[0], t_lat)
    out = None
    for l in range(depth):
        last = l == depth - 1
        w_main = jnp.concatenate([w_in[l][:, src[k][0]:src[k][0] + src[k][1]] for k in packed], axis=1).astype(BF16)
        b_main = jnp.concatenate([b_in[l][src[k][0]:src[k][0] + src[k][1]] for k in packed])[None, :]
        g0, gw = src["gates"]
        w_gate = jnp.pad(w_in[l][:, g0:g0 + gw], ((0, 0), (0, GATE_PAD - gw))).astype(BF16)
        b_gate = jnp.pad(b_in[l][g0:g0 + gw], (0, GATE_PAD - gw))[None, :]

        n2d = n.reshape(b * tt, d)
        p2d = _in_proj(n2d, w_main, b_main, 1024, 1024, "in_proj")
        gates = _in_proj(n2d, w_gate, b_gate, 2048, GATE_PAD, "in_proj_gates").reshape(b, tt, GATE_PAD)
        p3d = p2d.reshape(b, tt, n_main)

        ya = _chunk_mlp(p2d, w_spatial[l].astype(BF16), b_spatial[l].T, cols["a_u"], a_width).reshape(b, tt, a_width)

        qk = _conv_qk(p3d, conv_qk[l], t_lat, b_width // B_HEADS)
        gates_row = jnp.swapaxes(gates[:, :, :4 * B_HEADS], 1, 2)
        hf = _mlstm_dir(qk, p3d, gates, gates_row, t_lat, cols, reverse=False)
        yb = _mlstm_dir(qk, p3d, gates, gates_row, t_lat, cols, reverse=True, hf=hf, norm_g=mlstm_norm[l:l + 1])

        lbf_l, oml_l, hg = lbf[l:l + 1], oml[l:l + 1], hgrn_norm[l:l + 1]
        s_zero = jnp.zeros((b, C_HEADS, dv, C_KEY_DIM), F32)
        of_c, s_f = _hgrn_dir(p3d, lbf_l, oml_l, s_zero, e_mat, cols, "ctx", t_lat, False)
        of_l, _ = _hgrn_dir(p3d, lbf_l, oml_l, s_f, e_mat, cols, "lat", t_lat, False)
        yc_c, s_b = _hgrn_dir(p3d, lbf_l, oml_l, s_zero, e_mat, cols, "ctx", t_lat, True, of=of_c, norm_g=hg)
        yc_l, _ = _hgrn_dir(p3d, lbf_l, oml_l, s_b, e_mat, cols, "lat", t_lat, True, of=of_l, norm_g=hg)
        yc = jnp.concatenate([yc_l, yc_c], axis=1)

        w_o = w_out[l].astype(BF16)
        if last:
            out = _out_proj(ya, yb, yc, w_o, h, mods[l], final_norm[None, :], None, t_lat, final=True)
        else:
            h, n = _out_proj(ya, yb, yc, w_o, h, mods[l], norm_g[l + 1:l + 2], mods[l + 1], t_lat, final=False)
    return out
```

```python
import functools

import jax
import jax.numpy as jnp
from jax import lax
from jax.experimental import pallas as pl
from jax.experimental.pallas import tpu as pltpu

EPS = 1e-6
NEG_BIG = -1e30
LB_FLOOR = 1e-30
GRID_W = 64
CONV_W = 3

A_GROUPS = 4
A_CHUNK = 128
B_HEADS = 4
B_CHUNK = 128
C_HEADS = 4
C_KEY_DIM = 128
C_CHUNK = 64
C_SUB = 8
GATE_PAD = 128

V7X_VMEM_LIMIT = 56 * 1024 * 1024

F32 = jnp.float32
BF16 = jnp.bfloat16


def _params(*sem):
    return pltpu.CompilerParams(dimension_semantics=sem, vmem_limit_bytes=V7X_VMEM_LIMIT)


def _sigmoid(x):
    return 1.0 / (1.0 + jnp.exp(-x))


def _silu(x):
    return x * _sigmoid(x)


def _log_sigmoid(x):
    return jnp.minimum(x, 0.0) - jnp.log1p(jnp.exp(-jnp.abs(x)))


def _dot(a, b):
    return jnp.dot(a, b, preferred_element_type=F32)


def _dot_nt(a, b):
    return lax.dot_general(a, b, (((1,), (1,)), ((), ())), preferred_element_type=F32)


def _dot_tn(a, b):
    return lax.dot_general(a, b, (((0,), (0,)), ((), ())), preferred_element_type=F32)


def _lb_kernel(x_ref, lbf_ref, oml_ref):
    x = x_ref[...]
    depth = x.shape[0]
    e = jnp.exp(x - jnp.max(x, axis=0, keepdims=True))
    p = e / jnp.sum(e, axis=0, keepdims=True)
    rows = lax.broadcasted_iota(jnp.int32, x.shape, 0)
    lb = jnp.zeros_like(x)
    for j in range(1, depth):
        lb = lb + jnp.where(rows >= j, p[j:j + 1, :], 0.0)
    lbf_ref[...] = jnp.maximum(lb, LB_FLOOR)
    oml_ref[...] = 1.0 - lb


def _lower_bounds(logits):
    shp = jax.ShapeDtypeStruct(logits.shape, F32)
    return pl.pallas_call(_lb_kernel, out_shape=(shp, shp), name="hgrn_lower_bounds")(logits)


def _ada_kernel(c_ref, w_ref, b_ref, o_ref):
    s = _silu(c_ref[...]).astype(BF16)
    o_ref[0] = _dot(s, w_ref[0].astype(BF16)) + b_ref[0]


def _ada_mod(cond, w_ada, b_ada, tn=768):
    depth, d, n3 = w_ada.shape
    r = cond.shape[0]
    return pl.pallas_call(
        _ada_kernel,
        grid=(depth, n3 // tn),
        in_specs=[pl.BlockSpec((r, d), lambda l, j: (0, 0)),
                  pl.BlockSpec((1, d, tn), lambda l, j: (l, 0, j)),
                  pl.BlockSpec((1, 1, tn), lambda l, j: (l, 0, j))],
        out_specs=pl.BlockSpec((1, r, tn), lambda l, j: (l, 0, j)),
        out_shape=jax.ShapeDtypeStruct((depth, r, n3), F32),
        compiler_params=_params("parallel", "parallel"),
        name="adaln_mod",
    )(cond, w_ada, b_ada.reshape(depth, 1, n3))


def _modulated_norm(h, g, mod, d):
    y = h * lax.rsqrt(jnp.mean(h * h, axis=-1, keepdims=True) + EPS) * g
    return y * (1.0 + mod[:, d:2 * d]) + mod[:, 0:d]


def _norm_kernel(h_ref, g_ref, mod_ref, n_ref):
    d = h_ref.shape[-1]
    n_ref[0] = _modulated_norm(h_ref[0], g_ref[...], mod_ref[0], d).astype(n_ref.dtype)


def _mod_index(tiles_lat):
    return lambda b, j: (2 * b + jnp.where(j >= tiles_lat, 1, 0), 0, 0)


def _norm_mod(h, g, mod, t_lat, tm=256):
    b, tt, d = h.shape
    return pl.pallas_call(
        _norm_kernel,
        grid=(b, tt // tm),
        in_specs=[pl.BlockSpec((1, tm, d), lambda i, j: (i, j, 0)),
                  pl.BlockSpec((1, d), lambda i, j: (0, 0)),
                  pl.BlockSpec((1, 1, 3 * d), _mod_index(t_lat // tm))],
        out_specs=pl.BlockSpec((1, tm, d), lambda i, j: (i, j, 0)),
        out_shape=jax.ShapeDtypeStruct((b, tt, d), BF16),
        compiler_params=_params("parallel", "parallel"),
        name="norm_mod",
    )(h, g, mod)


def _matmul_bias_kernel(x_ref, w_ref, b_ref, o_ref):
    o_ref[...] = (_dot(x_ref[...], w_ref[...]) + b_ref[...]).astype(o_ref.dtype)


def _tile(m, pref, unit=128):
    t = min(pref, m) // unit * unit
    while m % t:
        t -= unit
    return t


def _in_proj(n2d, w, bias, tm, tn, name):
    m, d = n2d.shape
    n = w.shape[1]
    tm = _tile(m, tm)
    return pl.pallas_call(
        _matmul_bias_kernel,
        grid=(n // tn, m // tm),
        in_specs=[pl.BlockSpec((tm, d), lambda j, i: (i, 0)),
                  pl.BlockSpec((d, tn), lambda j, i: (0, j)),
                  pl.BlockSpec((1, tn), lambda j, i: (0, j))],
        out_specs=pl.BlockSpec((tm, tn), lambda j, i: (i, j)),
        out_shape=jax.ShapeDtypeStruct((m, n), F32),
        compiler_params=_params("parallel", "parallel"),
        name=name,
    )(n2d, w, bias)


def _chunk_mlp_kernel(u_ref, v_ref, z_ref, ws_ref, bs_ref, y_ref):
    tm, width = v_ref.shape
    gd = width // A_GROUPS
    for c in range(tm // A_CHUNK):
        rows = slice(c * A_CHUNK, (c + 1) * A_CHUNK)
        for g in range(A_GROUPS):
            cols = slice(g * gd, (g + 1) * gd)
            v = v_ref[rows, cols]
            mu = jnp.mean(v, axis=-1, keepdims=True)
            vc = v - mu
            var = jnp.mean(vc * vc, axis=-1, keepdims=True)
            vn = (vc * lax.rsqrt(var + EPS)).astype(BF16)
            mixed = _dot(ws_ref[g], vn) + bs_ref[:, g:g + 1]
            y_ref[rows, cols] = (u_ref[rows, cols] * mixed * _silu(z_ref[rows, cols])).astype(y_ref.dtype)


def _chunk_mlp(p2d, ws, bs_t, col_u, width, tm=1024):
    m = p2d.shape[0]
    tm = _tile(m, tm)
    cb = col_u // width
    spec = lambda k: pl.BlockSpec((tm, width), lambda i, k=k: (i, cb + k))
    return pl.pallas_call(
        _chunk_mlp_kernel,
        grid=(m // tm,),
        in_specs=[spec(0), spec(1), spec(2),
                  pl.BlockSpec(ws.shape, lambda i: (0, 0, 0)),
                  pl.BlockSpec(bs_t.shape, lambda i: (0, 0))],
        out_specs=pl.BlockSpec((tm, width), lambda i: (i, 0)),
        out_shape=jax.ShapeDtypeStruct((m, width), BF16),
        compiler_params=_params("parallel"),
        name="chunk_mlp",
    )(p2d, p2d, p2d, ws, bs_t)


def _conv_kernel(x_ref, w_ref, o_ref, *, t_lat, k_blocks_from, k_scale):
    x = x_ref[0]
    tt = x.shape[0]
    rows = lax.broadcasted_iota(jnp.int32, x.shape, 0)
    prev = jnp.where((rows == 0) | (rows == t_lat), 0.0, pltpu.roll(x, 1, axis=0))
    nxt = jnp.where((rows == t_lat - 1) | (rows == tt - 1), 0.0, pltpu.roll(x, tt - 1, axis=0))
    w = w_ref[...]
    y = _silu(w[0:1] * prev + w[1:2] * x + w[2:3] * nxt)
    scale = jnp.where(pl.program_id(1) >= k_blocks_from, k_scale, 1.0)
    o_ref[0] = (y * scale).astype(o_ref.dtype)


def _conv_qk(p3d, conv_w, t_lat, head_dim, tc=256):
    b, tt, _ = p3d.shape
    width = conv_w.shape[1]
    kern = functools.partial(_conv_kernel, t_lat=t_lat, k_blocks_from=(width // 2) // tc,
                             k_scale=head_dim ** -0.5)
    return pl.pallas_call(
        kern,
        grid=(b, width // tc),
        in_specs=[pl.BlockSpec((1, tt, tc), lambda i, j: (i, 0, j)),
                  pl.BlockSpec((CONV_W, tc), lambda i, j: (0, j))],
        out_specs=pl.BlockSpec((1, tt, tc), lambda i, j: (i, 0, j)),
        out_shape=jax.ShapeDtypeStruct((b, tt, width), BF16),
        compiler_params=_params("parallel", "parallel"),
        name="conv_qk",
    )(p3d, conv_w)


def _mlstm_kernel(*refs, reverse, readout, hd):
    if readout:
        (q_ref, k_ref, v_ref, gc_ref, gr_ref, hf_ref, o_ref, z_ref, g_ref,
         out_ref, c_scr, n_scr, m_scr) = refs
    else:
        q_ref, k_ref, v_ref, gc_ref, gr_ref, out_ref, c_scr, n_scr, m_scr = refs

    @pl.when(pl.program_id(1) == 0)
    def _():
        c_scr[...] = jnp.zeros_like(c_scr)
        n_scr[...] = jnp.zeros_like(n_scr)
        m_scr[...] = jnp.zeros_like(m_scr)

    L = B_CHUNK
    row = lax.broadcasted_iota(jnp.int32, (L, L), 0)
    col = lax.broadcasted_iota(jnp.int32, (L, L), 1)
    seen = (col >= row) if reverse else (col <= row)
    seen_t = (row >= col) if reverse else (row <= col)
    last = 0 if reverse else L - 1
    gi, gf = (2 * B_HEADS, 3 * B_HEADS) if reverse else (0, B_HEADS)
    gc = gc_ref[0]
    gr = gr_ref[0]
    for h in range(B_HEADS):
        cols = slice(h * hd, (h + 1) * hd)
        q = q_ref[0, :, cols]
        k = k_ref[0, :, cols]
        v = v_ref[0, :, cols].astype(BF16)
        i_col = gc[:, gi + h:gi + h + 1]
        f_col = _log_sigmoid(gc[:, gf + h:gf + h + 1])
        i_row = gr[gi + h:gi + h + 1, :]
        f_row = _log_sigmoid(gr[gf + h:gf + h + 1, :])
        cb_col = jnp.sum(jnp.where(seen, f_row, 0.0), axis=1, keepdims=True)
        cb_row = jnp.sum(jnp.where(seen_t, f_col, 0.0), axis=0, keepdims=True)
        dmat = jnp.where(seen, cb_col - cb_row + i_row, NEG_BIG)
        m = m_scr[h, 0:1, 0:1]
        inter = cb_col + m
        mt = jnp.maximum(inter, jnp.max(dmat, axis=1, keepdims=True))
        scores = _dot_nt(q, k) * jnp.exp(dmat - mt)
        e_inter = jnp.exp(inter - mt)
        c_state = c_scr[h]
        n_state = n_scr[h]
        num = _dot(scores.astype(BF16), v) + e_inter * _dot(q, c_state.astype(BF16))
        den = (jnp.sum(scores, axis=1, keepdims=True)
               + e_inter * jnp.sum(q.astype(F32) * n_state, axis=1, keepdims=True))
        hc = num / jnp.maximum(jnp.abs(den), jnp.exp(-mt))
        cl = cb_col[last:last + 1, :]
        w_log = cl - cb_col + i_col
        m_new = jnp.maximum(cl + m, jnp.max(w_log, axis=0, keepdims=True))
        ec = jnp.exp(cl + m - m_new)
        kw = k.astype(F32) * jnp.exp(w_log - m_new)
        c_scr[h] = ec * c_state + _dot_tn(kw.astype(BF16), v)
        n_scr[h] = ec * n_state + jnp.sum(kw, axis=0, keepdims=True)
        m_scr[h] = jnp.broadcast_to(m_new, m_scr.shape[1:])
        if readout:
            hs = hf_ref[0, :, cols] + hc
            hn = hs * lax.rsqrt(jnp.mean(hs * hs, axis=-1, keepdims=True) + EPS) * g_ref[:, cols]
            y = hn * _sigmoid(o_ref[0, :, cols]) * _silu(z_ref[0, :, cols])
            out_ref[0, :, cols] = y.astype(out_ref.dtype)
        else:
            out_ref[0, :, cols] = hc.astype(out_ref.dtype)


def _mlstm_dir(qk, p3d, gates_col, gates_row, t_lat, cols, reverse, hf=None, norm_g=None):
    b, tt, _ = p3d.shape
    width = qk.shape[-1] // 2
    hd = width // B_HEADS
    L = B_CHUNK
    nch, nlat = tt // L, t_lat // L
    if reverse:
        chunk = lambda i: nch - 1 - i
    else:
        chunk = lambda i: lax.rem(i + nlat, nch)
    wb = lambda name: cols[name] // width
    tok = lambda cb: pl.BlockSpec((1, L, width), lambda bi, i, cb=cb: (bi, chunk(i), cb))
    in_specs = [tok(0), tok(1), tok(wb("b_v")),
                pl.BlockSpec((1, L, GATE_PAD), lambda bi, i: (bi, chunk(i), 0)),
                pl.BlockSpec((1, gates_row.shape[1], L), lambda bi, i: (bi, 0, chunk(i)))]
    args = [qk, qk, p3d, gates_col, gates_row]
    readout = hf is not None
    if readout:
        in_specs += [tok(0), tok(wb("b_o")), tok(wb("b_z")), pl.BlockSpec((1, width), lambda bi, i: (0, 0))]
        args += [hf, p3d, p3d, norm_g]
    kern = functools.partial(_mlstm_kernel, reverse=reverse, readout=readout, hd=hd)
    return pl.pallas_call(
        kern,
        grid=(b, nch),
        in_specs=in_specs,
        out_specs=tok(0),
        out_shape=jax.ShapeDtypeStruct((b, tt, width), BF16 if readout else F32),
        scratch_shapes=[pltpu.VMEM((B_HEADS, hd, hd), F32),
                        pltpu.VMEM((B_HEADS, 1, hd), F32),
                        pltpu.VMEM((B_HEADS, 8, 128), F32)],
        compiler_params=_params("parallel", "arbitrary"),
        name="mlstm_bwd_readout" if readout else "mlstm_fwd",
    )(*args)


def _hgrn_kernel(*refs, reverse, readout, dv):
    if readout:
        q_ref, z_ref, v_ref, lbf_ref, oml_ref, e_ref, of_ref, cg_ref, g_ref, out_ref, s_scr = refs
    else:
        q_ref, z_ref, v_ref, lbf_ref, oml_ref, e_ref, out_ref, s_scr = refs

    @pl.when(pl.program_id(1) == 0)
    def _():
        s_scr[...] = jnp.zeros_like(s_scr)

    L, dk, nb = C_CHUNK, C_KEY_DIM, C_CHUNK // C_SUB
    row = lax.broadcasted_iota(jnp.int32, (L, L), 0)
    col = lax.broadcasted_iota(jnp.int32, (L, L), 1)
    rb, cbk = row // C_SUB, col // C_SUB
    if reverse:
        seen, blk_before, last = col >= row, cbk > rb, 0
    else:
        seen, blk_before, last = col <= row, cbk < rb, L - 1
    sum_mat = jnp.concatenate([jnp.where(seen, 1.0, 0.0), jnp.where(blk_before, 1.0, 0.0)], axis=0).astype(BF16)
    same_blk = rb == cbk
    sub = lax.broadcasted_iota(jnp.int32, (L, dk), 0) % C_SUB

    def bcast_sub(x, j):
        x3 = x.reshape(nb, C_SUB, x.shape[-1])
        return jnp.broadcast_to(x3[:, j:j + 1, :], x3.shape).reshape(x.shape)

    for h in range(C_HEADS):
        kc = slice(h * dk, (h + 1) * dk)
        vcols = slice(h * dv, (h + 1) * dv)
        q = _silu(q_ref[0, :, kc])
        z = z_ref[0, :, kc]
        v = v_ref[0, :, vcols].astype(BF16)
        a = jnp.exp(-jnp.abs(z))
        r = 1.0 / (1.0 + a)
        pos = z >= 0.0
        oml = oml_ref[:, kc]
        f = lbf_ref[:, kc] + oml * jnp.where(pos, r, a * r)
        k = oml * jnp.where(pos, a * r, r)
        lf = jnp.log(f)
        hi = lf.astype(BF16)
        r1 = lf - hi.astype(F32)
        mid = r1.astype(BF16)
        lo = (r1 - mid.astype(F32)).astype(BF16)
        c3 = _dot(sum_mat, jnp.concatenate([hi, mid, lo], axis=1))
        c1 = c3[:, 0:dk] + c3[:, dk:2 * dk] + c3[:, 2 * dk:3 * dk]
        cb, entry = c1[0:L], c1[L:2 * L]
        qd = (q * jnp.exp(cb - entry)).astype(BF16)
        parts = []
        for i in range(nb):
            kd = (k * jnp.exp(jnp.minimum(entry[i * C_SUB:i * C_SUB + 1, :] - cb, 0.0))).astype(BF16)
            parts.append(_dot_nt(qd[i * C_SUB:(i + 1) * C_SUB], kd))
        a_off = jnp.concatenate(parts, axis=0)
        qk_parts = []
        for j in range(C_SUB):
            ok = (sub <= j) if reverse else (sub >= j)
            dec = jnp.exp(jnp.where(ok, cb - bcast_sub(cb, j), NEG_BIG))
            qk_parts.append((q * bcast_sub(k, j) * dec).astype(BF16))
        a_diag = _dot(jnp.concatenate(qk_parts, axis=1), e_ref[...])
        scores = jnp.where(blk_before, a_off, 0.0) + jnp.where(same_blk, a_diag, 0.0)
        st = s_scr[h]
        o = _dot(scores.astype(BF16), v) + _dot_nt((q * jnp.exp(cb)).astype(BF16), st.astype(BF16))
        cl = cb[last:last + 1, :]
        kdec = (k * jnp.exp(cl - cb)).astype(BF16)
        s_scr[h] = st * jnp.exp(cl) + _dot_tn(v, kdec)
        if readout:
            os_ = of_ref[0, :, vcols] + o
            on = os_ * lax.rsqrt(jnp.mean(os_ * os_, axis=-1, keepdims=True) + EPS) * g_ref[:, vcols]
            out_ref[0, :, vcols] = (on * _silu(cg_ref[0, :, vcols])).astype(out_ref.dtype)
        else:
            out_ref[0, :, vcols] = o.astype(out_ref.dtype)


def _hgrn_dir(pc, lbf, oml, e_mat, cols, t_lat, reverse, of=None, norm_g=None):
    b, tt, _ = pc.shape
    kw, L = C_HEADS * C_KEY_DIM, C_CHUNK
    vw = cols["c_g"] - cols["c_i"]
    dv = vw // C_HEADS
    nch, nlat = tt // L, t_lat // L
    if reverse:
        chunk = lambda i: nch - 1 - i
    else:
        chunk = lambda i: lax.rem(i + nlat, nch)
    tok = lambda c0, w: pl.BlockSpec((1, L, w), lambda bi, i: (bi, chunk(i), c0 // w))
    const2 = lambda x: pl.BlockSpec(x.shape, lambda bi, i: (0, 0))
    f_name = "c_f_bwd" if reverse else "c_f_fwd"
    in_specs = [tok(cols["c_q"], kw), tok(cols[f_name], kw), tok(cols["c_i"], vw),
                const2(lbf), const2(oml), const2(e_mat)]
    args = [pc, pc, pc, lbf, oml, e_mat]
    readout = of is not None
    if readout:
        in_specs += [tok(0, vw), tok(cols["c_g"], vw), const2(norm_g)]
        args += [of, pc, norm_g]
    kern = functools.partial(_hgrn_kernel, reverse=reverse, readout=readout, dv=dv)
    return pl.pallas_call(
        kern,
        grid=(b, nch),
        in_specs=in_specs,
        out_specs=tok(0, vw),
        out_shape=jax.ShapeDtypeStruct((b, tt, vw), BF16 if readout else F32),
        scratch_shapes=[pltpu.VMEM((C_HEADS, dv, C_KEY_DIM), F32)],
        compiler_params=_params("parallel", "arbitrary"),
        name="hgrn_bwd_readout" if readout else "hgrn_fwd",
    )(*args)


def _out_kernel(ya_ref, yb_ref, yc_ref, w_ref, h_ref, mod_ref, *rest, final, wa, wb):
    d = h_ref.shape[-1]
    w = w_ref
    y = (_dot(ya_ref[0], w[0:wa, :]) + _dot(yb_ref[0], w[wa:wa + wb, :]) + _dot(yc_ref[0], w[wa + wb:, :]))
    h_new = h_ref[0] + mod_ref[0][:, 2 * d:3 * d] * y
    if final:
        g_ref, out_ref = rest
        out_ref[0] = h_new * lax.rsqrt(jnp.mean(h_new * h_new, axis=-1, keepdims=True) + EPS) * g_ref[...]
    else:
        g_ref, modn_ref, h_out_ref, n_ref = rest
        h_out_ref[0] = h_new
        n_ref[0] = _modulated_norm(h_new, g_ref[...], modn_ref[0], d).astype(n_ref.dtype)


def _out_proj(ya, yb, yc, w_out, h, mod, g_next, mod_next, t_lat, final, tm=256):
    b, tt, d = h.shape
    wa, wb, wc = ya.shape[-1], yb.shape[-1], yc.shape[-1]
    tok = lambda w: pl.BlockSpec((1, tm, w), lambda i, j: (i, j, 0))
    mod_spec = pl.BlockSpec((1, 1, 3 * d), _mod_index(t_lat // tm))
    in_specs = [tok(wa), tok(wb), tok(wc), pl.BlockSpec(w_out.shape, lambda i, j: (0, 0)), tok(d), mod_spec,
                pl.BlockSpec((1, d), lambda i, j: (0, 0))]
    args = [ya, yb, yc, w_out, h, mod, g_next]
    kern = functools.partial(_out_kernel, final=final, wa=wa, wb=wb)
    if final:
        return pl.pallas_call(
            kern, grid=(b, t_lat // tm), in_specs=in_specs, out_specs=tok(d),
            out_shape=jax.ShapeDtypeStruct((b, t_lat, d), F32),
            compiler_params=_params("parallel", "parallel"), name="out_proj_final",
        )(*args)
    return pl.pallas_call(
        kern, grid=(b, tt // tm), in_specs=in_specs + [mod_spec], out_specs=(tok(d), tok(d)),
        out_shape=(jax.ShapeDtypeStruct((b, tt, d), F32), jax.ShapeDtypeStruct((b, tt, d), BF16)),
        compiler_params=_params("parallel", "parallel"), name="out_proj",
    )(*args, mod_next)


def _packed_layout(d):
    a, bw, c, kq = d // 4, d // 2, d // 4, C_HEADS * C_KEY_DIM
    ref_order = (("a_u", a), ("a_v", a), ("a_z", a), ("b_q", bw), ("b_k", bw), ("b_v", bw), ("b_o", bw),
                 ("b_z", bw), ("gates", 4 * B_HEADS), ("c_q", kq), ("c_f_fwd", kq), ("c_f_bwd", kq),
                 ("c_i", c), ("c_g", c))
    src, start = {}, 0
    for name, w in ref_order:
        src[name] = (start, w)
        start += w
    groups = {"ab": ("b_q", "b_k", "b_v", "b_o", "b_z", "a_u", "a_v", "a_z"),
              "c": ("c_q", "c_f_fwd", "c_f_bwd", "c_i", "c_g")}
    cols = {}
    for names in groups.values():
        pos = 0
        for name in names:
            cols[name] = pos
            pos += src[name][1]
    return src, groups, cols


def _pack_cols(w, bias, src, names):
    pick = lambda a: jnp.concatenate([a[..., src[k][0]:src[k][0] + src[k][1]] for k in names], axis=-1)
    return pick(w).astype(BF16), pick(bias)[None, :]


def _grid_transpose(x, rows, width):
    b, _, f = x.shape
    return x.reshape(b, rows, width, f).swapaxes(1, 2).reshape(b, rows * width, f)


def kernel(x, c, ctx, c_ctx, w_ada, b_ada, norm_g, w_in, b_in, w_spatial, b_spatial, conv_qk, mlstm_norm,
           hgrn_lb_logits, hgrn_norm, w_out, final_norm):
    b, t_lat, d = x.shape
    t_ctx = ctx.shape[1]
    tt = t_lat + t_ctx
    depth = w_ada.shape[0]
    src, groups, cols = _packed_layout(d)
    a_width, b_width = d // 4, d // 2
    rows = t_lat // GRID_W

    r_pad = -(-(b + 1) // 8) * 8
    cond = jnp.concatenate([c, c_ctx[None, :], jnp.zeros((r_pad - b - 1, d), F32)], axis=0)
    mod_all = _ada_mod(cond, w_ada, b_ada)
    mods = [jnp.stack([mod_all[l, :b], jnp.broadcast_to(mod_all[l, b], (b, 3 * d))], axis=1).reshape(2 * b, 1, 3 * d)
            for l in range(depth)]

    lbf, oml = _lower_bounds(hgrn_lb_logits.astype(F32))

    e_rows = lax.broadcasted_iota(jnp.int32, (C_SUB * C_KEY_DIM, C_CHUNK), 0) // C_KEY_DIM
    e_cols = lax.broadcasted_iota(jnp.int32, (C_SUB * C_KEY_DIM, C_CHUNK), 1) % C_SUB
    e_mat = (e_rows == e_cols).astype(BF16)

    h = jnp.concatenate([x, ctx], axis=1)
    n = _norm_mod(h, norm_g[0:1], mods[0], t_lat)
    out = None
    for l in range(depth):
        last = l == depth - 1
        w_ab, b_ab = _pack_cols(w_in[l], b_in[l], src, groups["ab"])
        w_c, b_c = _pack_cols(w_in[l], b_in[l], src, groups["c"])
        g0, gw = src["gates"]
        w_gate = jnp.pad(w_in[l][:, g0:g0 + gw], ((0, 0), (0, GATE_PAD - gw))).astype(BF16)
        b_gate = jnp.pad(b_in[l][g0:g0 + gw], (0, GATE_PAD - gw))[None, :]

        n_cm = jnp.concatenate([_grid_transpose(n[:, :t_lat], rows, GRID_W), n[:, t_lat:]], axis=1)
        n2d = n.reshape(b * tt, d)
        p2d = _in_proj(n2d, w_ab, b_ab, 1024, 1664, "in_proj_ab")
        pc = _in_proj(n_cm.reshape(b * tt, d), w_c, b_c, 1024, 1280, "in_proj_c").reshape(b, tt, -1)
        gates = _in_proj(n2d, w_gate, b_gate, 2048, GATE_PAD, "in_proj_gates").reshape(b, tt, GATE_PAD)
        p3d = p2d.reshape(b, tt, -1)

        ya = _chunk_mlp(p2d, w_spatial[l].astype(BF16), b_spatial[l].T, cols["a_u"], a_width).reshape(b, tt, a_width)

        qk = _conv_qk(p3d, conv_qk[l], t_lat, b_width // B_HEADS)
        gates_row = jnp.swapaxes(gates[:, :, :4 * B_HEADS], 1, 2)
        hf = _mlstm_dir(qk, p3d, gates, gates_row, t_lat, cols, reverse=False)
        yb = _mlstm_dir(qk, p3d, gates, gates_row, t_lat, cols, reverse=True, hf=hf, norm_g=mlstm_norm[l:l + 1])

        lbf_l, oml_l = lbf[l:l + 1], oml[l:l + 1]
        of = _hgrn_dir(pc, lbf_l, oml_l, e_mat, cols, t_lat, False)
        yc_cm = _hgrn_dir(pc, lbf_l, oml_l, e_mat, cols, t_lat, True, of=of, norm_g=hgrn_norm[l:l + 1])
        yc = jnp.concatenate([_grid_transpose(yc_cm[:, :t_lat], GRID_W, rows), yc_cm[:, t_lat:]], axis=1)

        w_o = w_out[l].astype(BF16)
        if last:
            out = _out_proj(ya, yb, yc, w_o, h, mods[l], final_norm[None, :], None, t_lat, final=True)
        else:
            h, n = _out_proj(ya, yb, yc, w_o, h, mods[l], norm_g[l + 1:l + 2], mods[l + 1], t_lat, final=False)
    return out
```

```python
import functools

import jax
import jax.numpy as jnp
from jax import lax
from jax.experimental import pallas as pl
from jax.experimental.pallas import tpu as pltpu

EPS = 1e-6
NEG_BIG = -1e30
LB_FLOOR = 1e-30
GRID_W = 64
CONV_W = 3

A_GROUPS = 4
A_CHUNK = 128
B_HEADS = 4
B_CHUNK = 128
C_HEADS = 4
C_KEY_DIM = 128
C_CHUNK = 64
C_SUB = 8
GATE_PAD = 128
MLSTM_AUG = 128

V7X_VMEM_LIMIT = 56 * 1024 * 1024

F32 = jnp.float32
BF16 = jnp.bfloat16


def _params(*sem):
    return pltpu.CompilerParams(dimension_semantics=sem, vmem_limit_bytes=V7X_VMEM_LIMIT)


def _sigmoid(x):
    return 1.0 / (1.0 + jnp.exp(-x))


def _silu(x):
    return x * _sigmoid(x)


def _log_sigmoid(x):
    return jnp.minimum(x, 0.0) - jnp.log1p(jnp.exp(-jnp.abs(x)))


def _dot(a, b):
    return jnp.dot(a, b, preferred_element_type=F32)


def _dot_nt(a, b):
    return lax.dot_general(a, b, (((1,), (1,)), ((), ())), preferred_element_type=F32)


def _dot_tn(a, b):
    return lax.dot_general(a, b, (((0,), (0,)), ((), ())), preferred_element_type=F32)


def _lb_kernel(x_ref, lbf_ref, oml_ref):
    x = x_ref[...]
    depth = x.shape[0]
    e = jnp.exp(x - jnp.max(x, axis=0, keepdims=True))
    p = e / jnp.sum(e, axis=0, keepdims=True)
    rows = lax.broadcasted_iota(jnp.int32, x.shape, 0)
    lb = jnp.zeros_like(x)
    for j in range(1, depth):
        lb = lb + jnp.where(rows >= j, p[j:j + 1, :], 0.0)
    lbf_ref[...] = jnp.maximum(lb, LB_FLOOR)
    oml_ref[...] = 1.0 - lb


def _lower_bounds(logits):
    shp = jax.ShapeDtypeStruct(logits.shape, F32)
    return pl.pallas_call(_lb_kernel, out_shape=(shp, shp), name="hgrn_lower_bounds")(logits)


def _ada_kernel(c_ref, w_ref, b_ref, o_ref):
    s = _silu(c_ref[...]).astype(BF16)
    o_ref[0] = _dot(s, w_ref[0].astype(BF16)) + b_ref[0]


def _ada_mod(cond, w_ada, b_ada, tn=768):
    depth, d, n3 = w_ada.shape
    r = cond.shape[0]
    return pl.pallas_call(
        _ada_kernel,
        grid=(depth, n3 // tn),
        in_specs=[pl.BlockSpec((r, d), lambda l, j: (0, 0)),
                  pl.BlockSpec((1, d, tn), lambda l, j: (l, 0, j)),
                  pl.BlockSpec((1, 1, tn), lambda l, j: (l, 0, j))],
        out_specs=pl.BlockSpec((1, r, tn), lambda l, j: (l, 0, j)),
        out_shape=jax.ShapeDtypeStruct((depth, r, n3), F32),
        compiler_params=_params("parallel", "parallel"),
        name="adaln_mod",
    )(cond, w_ada, b_ada.reshape(depth, 1, n3))


def _modulated_norm(h, g, mod, d):
    y = h * lax.rsqrt(jnp.mean(h * h, axis=-1, keepdims=True) + EPS) * g
    return y * (1.0 + mod[:, d:2 * d]) + mod[:, 0:d]


def _norm_kernel(h_ref, g_ref, mod_ref, n_ref):
    d = h_ref.shape[-1]
    n_ref[0] = _modulated_norm(h_ref[0], g_ref[...], mod_ref[0], d).astype(n_ref.dtype)


def _mod_index(tiles_lat):
    return lambda b, j: (2 * b + jnp.where(j >= tiles_lat, 1, 0), 0, 0)


def _norm_mod(h, g, mod, t_lat, tm=256):
    b, tt, d = h.shape
    return pl.pallas_call(
        _norm_kernel,
        grid=(b, tt // tm),
        in_specs=[pl.BlockSpec((1, tm, d), lambda i, j: (i, j, 0)),
                  pl.BlockSpec((1, d), lambda i, j: (0, 0)),
                  pl.BlockSpec((1, 1, 3 * d), _mod_index(t_lat // tm))],
        out_specs=pl.BlockSpec((1, tm, d), lambda i, j: (i, j, 0)),
        out_shape=jax.ShapeDtypeStruct((b, tt, d), BF16),
        compiler_params=_params("parallel", "parallel"),
        name="norm_mod",
    )(h, g, mod)


def _matmul_bias_kernel(x_ref, w_ref, b_ref, o_ref):
    o_ref[...] = (_dot(x_ref[...], w_ref[...]) + b_ref[...]).astype(o_ref.dtype)


def _tile(m, pref, unit=128):
    t = min(pref, m) // unit * unit
    while m % t:
        t -= unit
    return t


def _in_proj(n2d, w, bias, tm, tn, name):
    m, d = n2d.shape
    n = w.shape[1]
    tm = _tile(m, tm)
    return pl.pallas_call(
        _matmul_bias_kernel,
        grid=(n // tn, m // tm),
        in_specs=[pl.BlockSpec((tm, d), lambda j, i: (i, 0)),
                  pl.BlockSpec((d, tn), lambda j, i: (0, j)),
                  pl.BlockSpec((1, tn), lambda j, i: (0, j))],
        out_specs=pl.BlockSpec((tm, tn), lambda j, i: (i, j)),
        out_shape=jax.ShapeDtypeStruct((m, n), F32),
        compiler_params=_params("parallel", "parallel"),
        name=name,
    )(n2d, w, bias)


def _chunk_mlp_kernel(u_ref, v_ref, z_ref, ws_ref, bs_ref, y_ref):
    tm, width = v_ref.shape
    gd = width // A_GROUPS
    for c in range(tm // A_CHUNK):
        rows = slice(c * A_CHUNK, (c + 1) * A_CHUNK)
        for g in range(A_GROUPS):
            cols = slice(g * gd, (g + 1) * gd)
            v = v_ref[rows, cols]
            mu = jnp.mean(v, axis=-1, keepdims=True)
            vc = v - mu
            var = jnp.mean(vc * vc, axis=-1, keepdims=True)
            vn = (vc * lax.rsqrt(var + EPS)).astype(BF16)
            mixed = _dot(ws_ref[g], vn) + bs_ref[:, g:g + 1]
            y_ref[rows, cols] = (u_ref[rows, cols] * mixed * _silu(z_ref[rows, cols])).astype(y_ref.dtype)


def _chunk_mlp(p2d, ws, bs_t, col_u, width, tm=1024):
    m = p2d.shape[0]
    tm = _tile(m, tm)
    cb = col_u // width
    spec = lambda k: pl.BlockSpec((tm, width), lambda i, k=k: (i, cb + k))
    return pl.pallas_call(
        _chunk_mlp_kernel,
        grid=(m // tm,),
        in_specs=[spec(0), spec(1), spec(2),
                  pl.BlockSpec(ws.shape, lambda i: (0, 0, 0)),
                  pl.BlockSpec(bs_t.shape, lambda i: (0, 0))],
        out_specs=pl.BlockSpec((tm, width), lambda i: (i, 0)),
        out_shape=jax.ShapeDtypeStruct((m, width), BF16),
        compiler_params=_params("parallel"),
        name="chunk_mlp",
    )(p2d, p2d, p2d, ws, bs_t)


def _conv_kernel(x_ref, w_ref, o_ref, *, t_lat, scale, transpose):
    x = x_ref[0]
    tt = x.shape[0]
    rows = lax.broadcasted_iota(jnp.int32, x.shape, 0)
    prev = jnp.where((rows == 0) | (rows == t_lat), 0.0, pltpu.roll(x, 1, axis=0))
    nxt = jnp.where((rows == t_lat - 1) | (rows == tt - 1), 0.0, pltpu.roll(x, tt - 1, axis=0))
    w = w_ref[...]
    y = _silu(w[0:1] * prev + w[1:2] * x + w[2:3] * nxt) * scale
    o_ref[0] = (y.T if transpose else y).astype(o_ref.dtype)


def _conv_silu(p3d, conv_w, t_lat, col0, wcol0, width, scale, transpose, tc=256):
    b, tt, _ = p3d.shape
    kern = functools.partial(_conv_kernel, t_lat=t_lat, scale=scale, transpose=transpose)
    if transpose:
        out_spec = pl.BlockSpec((1, tc, tt), lambda i, j: (i, j, 0))
        out_shape = jax.ShapeDtypeStruct((b, width, tt), BF16)
    else:
        out_spec = pl.BlockSpec((1, tt, tc), lambda i, j: (i, 0, j))
        out_shape = jax.ShapeDtypeStruct((b, tt, width), BF16)
    return pl.pallas_call(
        kern,
        grid=(b, width // tc),
        in_specs=[pl.BlockSpec((1, tt, tc), lambda i, j: (i, 0, col0 // tc + j)),
                  pl.BlockSpec((CONV_W, tc), lambda i, j: (0, wcol0 // tc + j))],
        out_specs=out_spec,
        out_shape=out_shape,
        compiler_params=_params("parallel", "parallel"),
        name="conv_k_t" if transpose else "conv_q",
    )(p3d, conv_w)


def _mlstm_kernel(*refs, reverse, readout, hd):
    if readout:
        q_ref, kt_ref, v_ref, gr_ref, hf_ref, o_ref, z_ref, g_ref, out_ref, c_scr, m_scr = refs
    else:
        q_ref, kt_ref, v_ref, gr_ref, out_ref, c_scr, m_scr = refs

    @pl.when(pl.program_id(1) == 0)
    def _():
        c_scr[...] = jnp.zeros_like(c_scr)
        m_scr[...] = jnp.zeros_like(m_scr)

    L = B_CHUNK
    row = lax.broadcasted_iota(jnp.int32, (L, L), 0)
    col = lax.broadcasted_iota(jnp.int32, (L, L), 1)
    seen = (col >= row) if reverse else (col <= row)
    tri_t = jnp.where((row >= col) if reverse else (row <= col), 1.0, 0.0).astype(BF16)
    last = 0 if reverse else L - 1
    gi, gf = (2 * B_HEADS, 3 * B_HEADS) if reverse else (0, B_HEADS)
    heads = range(B_HEADS)
    cols = [slice(h * hd, (h + 1) * hd) for h in heads]
    ones_blk = jnp.ones((L, MLSTM_AUG), BF16)
    gr = gr_ref[0]

    i_rows = gr[gi:gi + B_HEADS, :]
    f_rows = _log_sigmoid(gr[gf:gf + B_HEADS, :])
    hi = f_rows.astype(BF16).astype(F32)
    r1 = f_rows - hi
    mid = r1.astype(BF16).astype(F32)
    lo = (r1 - mid).astype(BF16).astype(F32)
    terms = jnp.concatenate([hi, mid, lo, jnp.zeros_like(hi)], axis=0).astype(BF16)
    sums = _dot(terms, tri_t)
    cb_rows = sums[0:B_HEADS] + sums[B_HEADS:2 * B_HEADS] + sums[2 * B_HEADS:3 * B_HEADS]

    q, v_aug, s_bf, e_inter, emt, ws, ec, m_new = [], [], [], [], [], [], [], []
    for h in heads:
        q.append(q_ref[0, :, cols[h]])
        v_aug.append(jnp.concatenate([v_ref[0, :, cols[h]].astype(BF16), ones_blk], axis=1))
        f_row, i_row, cb_row = f_rows[h:h + 1], i_rows[h:h + 1], cb_rows[h:h + 1]
        cb_col = jnp.sum(jnp.where(seen, f_row, 0.0), axis=1, keepdims=True)
        dmat = jnp.where(seen, cb_col - cb_row + i_row, NEG_BIG)
        m = m_scr[h, 0:1, 0:1]
        inter = cb_col + m
        mt = jnp.maximum(inter, jnp.max(dmat, axis=1, keepdims=True))
        s_bf.append((_dot(q[h], kt_ref[0, cols[h], :]) * jnp.exp(dmat - mt)).astype(BF16))
        e_inter.append(jnp.exp(inter - mt))
        emt.append(jnp.exp(-mt))
        cl = cb_row[:, last:last + 1]
        w_log = cl - cb_row + i_row
        m_new.append(jnp.maximum(cl + m, jnp.max(w_log, axis=1, keepdims=True)))
        ec.append(jnp.exp(cl + m - m_new[h]))
        ws.append(jnp.exp(w_log - m_new[h]))

    for h in heads:
        c_state = c_scr[h]
        tot = _dot(s_bf[h], v_aug[h]) + e_inter[h] * _dot(q[h], c_state.astype(BF16))
        inv = 1.0 / jnp.maximum(jnp.abs(tot[:, hd:hd + MLSTM_AUG]), emt[h])
        hc = jnp.concatenate([tot[:, c0:c0 + MLSTM_AUG] * inv for c0 in range(0, hd, MLSTM_AUG)], axis=1)
        kw_t = (kt_ref[0, cols[h], :].astype(F32) * ws[h]).astype(BF16)
        c_scr[h] = ec[h] * c_state + _dot(kw_t, v_aug[h])
        m_scr[h] = jnp.broadcast_to(m_new[h], m_scr.shape[1:])
        if readout:
            hs = hf_ref[0, :, cols[h]] + hc
            hn = hs * lax.rsqrt(jnp.mean(hs * hs, axis=-1, keepdims=True) + EPS) * g_ref[:, cols[h]]
            y = hn * _sigmoid(o_ref[0, :, cols[h]]) * _silu(z_ref[0, :, cols[h]])
            out_ref[0, :, cols[h]] = y.astype(out_ref.dtype)
        else:
            out_ref[0, :, cols[h]] = hc.astype(out_ref.dtype)


def _mlstm_dir(q, k_t, p3d, gates_row, t_lat, cols, reverse, hf=None, norm_g=None):
    b, tt, width = q.shape
    hd = width // B_HEADS
    L = B_CHUNK
    nch, nlat = tt // L, t_lat // L
    if reverse:
        chunk = lambda i: nch - 1 - i
    else:
        chunk = lambda i: lax.rem(i + nlat, nch)
    wb = lambda name: cols[name] // width
    tok = lambda cb: pl.BlockSpec((1, L, width), lambda bi, i, cb=cb: (bi, chunk(i), cb))
    in_specs = [tok(0), pl.BlockSpec((1, width, L), lambda bi, i: (bi, 0, chunk(i))), tok(wb("b_v")),
                pl.BlockSpec((1, gates_row.shape[1], L), lambda bi, i: (bi, 0, chunk(i)))]
    args = [q, k_t, p3d, gates_row]
    readout = hf is not None
    if readout:
        in_specs += [tok(0), tok(wb("b_o")), tok(wb("b_z")), pl.BlockSpec((1, width), lambda bi, i: (0, 0))]
        args += [hf, p3d, p3d, norm_g]
    kern = functools.partial(_mlstm_kernel, reverse=reverse, readout=readout, hd=hd)
    return pl.pallas_call(
        kern,
        grid=(b, nch),
        in_specs=in_specs,
        out_specs=tok(0),
        out_shape=jax.ShapeDtypeStruct((b, tt, width), BF16 if readout else F32),
        scratch_shapes=[pltpu.VMEM((B_HEADS, hd, hd + MLSTM_AUG), F32),
                        pltpu.VMEM((B_HEADS, 8, 128), F32)],
        compiler_params=_params("parallel", "arbitrary"),
        name="mlstm_bwd_readout" if readout else "mlstm_fwd",
    )(*args)


def _hgrn_kernel(*refs, reverse, readout, dv):
    if readout:
        q_ref, z_ref, v_ref, lbf_ref, oml_ref, e_ref, of_ref, cg_ref, g_ref, out_ref, s_scr = refs
    else:
        q_ref, z_ref, v_ref, lbf_ref, oml_ref, e_ref, out_ref, s_scr = refs

    @pl.when(pl.program_id(1) == 0)
    def _():
        s_scr[...] = jnp.zeros_like(s_scr)

    L, dk, nb = C_CHUNK, C_KEY_DIM, C_CHUNK // C_SUB
    row = lax.broadcasted_iota(jnp.int32, (L, L), 0)
    col = lax.broadcasted_iota(jnp.int32, (L, L), 1)
    rb, cbk = row // C_SUB, col // C_SUB
    if reverse:
        seen, blk_before, last = col >= row, cbk > rb, 0
    else:
        seen, blk_before, last = col <= row, cbk < rb, L - 1
    sum_mat = jnp.concatenate([jnp.where(seen, 1.0, 0.0), jnp.where(blk_before, 1.0, 0.0)], axis=0).astype(BF16)
    same_blk = rb == cbk
    sub = lax.broadcasted_iota(jnp.int32, (L, dk), 0) % C_SUB

    def bcast_sub(x, j):
        x3 = x.reshape(nb, C_SUB, x.shape[-1])
        return jnp.broadcast_to(x3[:, j:j + 1, :], x3.shape).reshape(x.shape)

    heads = range(C_HEADS)
    kcs = [slice(h * dk, (h + 1) * dk) for h in heads]
    vcs = [slice(h * dv, (h + 1) * dv) for h in heads]

    q, k, v, c3 = [], [], [], []
    for h in heads:
        q.append(_silu(q_ref[0, :, kcs[h]]))
        z = z_ref[0, :, kcs[h]]
        v.append(v_ref[0, :, vcs[h]].astype(BF16))
        a = jnp.exp(-jnp.abs(z))
        r = 1.0 / (1.0 + a)
        pos = z >= 0.0
        oml = oml_ref[:, kcs[h]]
        f = lbf_ref[:, kcs[h]] + oml * jnp.where(pos, r, a * r)
        k.append(oml * jnp.where(pos, a * r, r))
        lf = jnp.log2(f)
        hi = lf.astype(BF16)
        r1 = lf - hi.astype(F32)
        mid = r1.astype(BF16)
        lo = (r1 - mid.astype(F32)).astype(BF16)
        c3.append(_dot(sum_mat, jnp.concatenate([hi, mid, lo], axis=1)))

    cb, a_off, a_diag = [], [], []
    for h in heads:
        c1 = c3[h][:, 0:dk] + c3[h][:, dk:2 * dk] + c3[h][:, 2 * dk:3 * dk]
        cbh, entry = c1[0:L], c1[L:2 * L]
        cb.append(cbh)
        qd = (q[h] * jnp.exp2(cbh - entry)).astype(BF16)
        parts = []
        for i in range(nb):
            lo_r, hi_r = ((i + 1) * C_SUB, L) if reverse else (0, i * C_SUB)
            if hi_r == lo_r:
                parts.append(jnp.zeros((C_SUB, L), F32))
                continue
            ent = entry[i * C_SUB:i * C_SUB + 1, :]
            kd = (k[h][lo_r:hi_r] * jnp.exp2(ent - cbh[lo_r:hi_r])).astype(BF16)
            pad = [jnp.zeros((n, dk), BF16) for n in (lo_r, L - hi_r)]
            kd = jnp.concatenate([p for p in (pad[0], kd, pad[1]) if p.shape[0]], axis=0)
            parts.append(_dot_nt(qd[i * C_SUB:(i + 1) * C_SUB], kd))
        a_off.append(jnp.concatenate(parts, axis=0))
        qk_parts = []
        for j in range(C_SUB):
            ok = (sub <= j) if reverse else (sub >= j)
            dec = jnp.exp2(jnp.where(ok, cbh - bcast_sub(cbh, j), NEG_BIG))
            qk_parts.append((q[h] * bcast_sub(k[h], j) * dec).astype(BF16))
        a_diag.append(_dot(jnp.concatenate(qk_parts, axis=1), e_ref[...]))

    for h in heads:
        scores = a_off[h] + jnp.where(same_blk, a_diag[h], 0.0)
        st = s_scr[h]
        o = _dot(scores.astype(BF16), v[h]) + _dot_nt((q[h] * jnp.exp2(cb[h])).astype(BF16), st.astype(BF16))
        cl = cb[h][last:last + 1, :]
        kdec = (k[h] * jnp.exp2(cl - cb[h])).astype(BF16)
        s_scr[h] = st * jnp.exp2(cl) + _dot_tn(v[h], kdec)
        if readout:
            os_ = of_ref[0, :, vcs[h]] + o
            on = os_ * lax.rsqrt(jnp.mean(os_ * os_, axis=-1, keepdims=True) + EPS) * g_ref[:, vcs[h]]
            out_ref[0, :, vcs[h]] = (on * _silu(cg_ref[0, :, vcs[h]])).astype(out_ref.dtype)
        else:
            out_ref[0, :, vcs[h]] = o.astype(out_ref.dtype)


def _hgrn_dir(pc, lbf, oml, e_mat, cols, t_lat, reverse, of=None, norm_g=None):
    b, tt, _ = pc.shape
    kw, L = C_HEADS * C_KEY_DIM, C_CHUNK
    vw = cols["c_g"] - cols["c_i"]
    dv = vw // C_HEADS
    nch, nlat = tt // L, t_lat // L
    if reverse:
        chunk = lambda i: nch - 1 - i
    else:
        chunk = lambda i: lax.rem(i + nlat, nch)
    tok = lambda c0, w: pl.BlockSpec((1, L, w), lambda bi, i: (bi, chunk(i), c0 // w))
    const2 = lambda x: pl.BlockSpec(x.shape, lambda bi, i: (0, 0))
    f_name = "c_f_bwd" if reverse else "c_f_fwd"
    in_specs = [tok(cols["c_q"], kw), tok(cols[f_name], kw), tok(cols["c_i"], vw),
                const2(lbf), const2(oml), const2(e_mat)]
    args = [pc, pc, pc, lbf, oml, e_mat]
    readout = of is not None
    if readout:
        in_specs += [tok(0, vw), tok(cols["c_g"], vw), const2(norm_g)]
        args += [of, pc, norm_g]
    kern = functools.partial(_hgrn_kernel, reverse=reverse, readout=readout, dv=dv)
    return pl.pallas_call(
        kern,
        grid=(b, nch),
        in_specs=in_specs,
        out_specs=tok(0, vw),
        out_shape=jax.ShapeDtypeStruct((b, tt, vw), BF16 if readout else F32),
        scratch_shapes=[pltpu.VMEM((C_HEADS, dv, C_KEY_DIM), F32)],
        compiler_params=_params("parallel", "arbitrary"),
        name="hgrn_bwd_readout" if readout else "hgrn_fwd",
    )(*args)


def _out_kernel(ya_ref, yb_ref, yc_ref, w_ref, h_ref, mod_ref, *rest, final, wa, wb):
    d = h_ref.shape[-1]
    w = w_ref
    y = (_dot(ya_ref[0], w[0:wa, :]) + _dot(yb_ref[0], w[wa:wa + wb, :]) + _dot(yc_ref[0], w[wa + wb:, :]))
    h_new = h_ref[0] + mod_ref[0][:, 2 * d:3 * d] * y
    if final:
        g_ref, out_ref = rest
        out_ref[0] = h_new * lax.rsqrt(jnp.mean(h_new * h_new, axis=-1, keepdims=True) + EPS) * g_ref[...]
    else:
        g_ref, modn_ref, h_out_ref, n_ref = rest
        h_out_ref[0] = h_new
        n_ref[0] = _modulated_norm(h_new, g_ref[...], modn_ref[0], d).astype(n_ref.dtype)


def _out_proj(ya, yb, yc, w_out, h, mod, g_next, mod_next, t_lat, final, tm=256):
    b, tt, d = h.shape
    wa, wb, wc = ya.shape[-1], yb.shape[-1], yc.shape[-1]
    tok = lambda w: pl.BlockSpec((1, tm, w), lambda i, j: (i, j, 0))
    mod_spec = pl.BlockSpec((1, 1, 3 * d), _mod_index(t_lat // tm))
    in_specs = [tok(wa), tok(wb), tok(wc), pl.BlockSpec(w_out.shape, lambda i, j: (0, 0)), tok(d), mod_spec,
                pl.BlockSpec((1, d), lambda i, j: (0, 0))]
    args = [ya, yb, yc, w_out, h, mod, g_next]
    kern = functools.partial(_out_kernel, final=final, wa=wa, wb=wb)
    if final:
        return pl.pallas_call(
            kern, grid=(b, t_lat // tm), in_specs=in_specs, out_specs=tok(d),
            out_shape=jax.ShapeDtypeStruct((b, t_lat, d), F32),
            compiler_params=_params("parallel", "parallel"), name="out_proj_final",
        )(*args)
    return pl.pallas_call(
        kern, grid=(b, tt // tm), in_specs=in_specs + [mod_spec], out_specs=(tok(d), tok(d)),
        out_shape=(jax.ShapeDtypeStruct((b, tt, d), F32), jax.ShapeDtypeStruct((b, tt, d), BF16)),
        compiler_params=_params("parallel", "parallel"), name="out_proj",
    )(*args, mod_next)


def _packed_layout(d):
    a, bw, c, kq = d // 4, d // 2, d // 4, C_HEADS * C_KEY_DIM
    ref_order = (("a_u", a), ("a_v", a), ("a_z", a), ("b_q", bw), ("b_k", bw), ("b_v", bw), ("b_o", bw),
                 ("b_z", bw), ("gates", 4 * B_HEADS), ("c_q", kq), ("c_f_fwd", kq), ("c_f_bwd", kq),
                 ("c_i", c), ("c_g", c))
    src, start = {}, 0
    for name, w in ref_order:
        src[name] = (start, w)
        start += w
    groups = {"ab": ("b_q", "b_k", "b_v", "b_o", "b_z", "a_u", "a_v", "a_z"),
              "c": ("c_q", "c_f_fwd", "c_f_bwd", "c_i", "c_g")}
    cols = {}
    for names in groups.values():
        pos = 0
        for name in names:
            cols[name] = pos
            pos += src[name][1]
    return src, groups, cols


def _pack_cols(w, bias, src, names):
    pick = lambda a: jnp.concatenate([a[..., src[k][0]:src[k][0] + src[k][1]] for k in names], axis=-1)
    return pick(w).astype(BF16), pick(bias)[None, :]


def _grid_transpose(x, rows, width):
    b, _, f = x.shape
    return x.reshape(b, rows, width, f).swapaxes(1, 2).reshape(b, rows * width, f)


def kernel(x, c, ctx, c_ctx, w_ada, b_ada, norm_g, w_in, b_in, w_spatial, b_spatial, conv_qk, mlstm_norm,
           hgrn_lb_logits, hgrn_norm, w_out, final_norm):
    b, t_lat, d = x.shape
    t_ctx = ctx.shape[1]
    tt = t_lat + t_ctx
    depth = w_ada.shape[0]
    src, groups, cols = _packed_layout(d)
    a_width, b_width = d // 4, d // 2
    rows = t_lat // GRID_W

    r_pad = -(-(b + 1) // 8) * 8
    cond = jnp.concatenate([c, c_ctx[None, :], jnp.zeros((r_pad - b - 1, d), F32)], axis=0)
    mod_all = _ada_mod(cond, w_ada, b_ada)
    mods = [jnp.stack([mod_all[l, :b], jnp.broadcast_to(mod_all[l, b], (b, 3 * d))], axis=1).reshape(2 * b, 1, 3 * d)
            for l in range(depth)]

    lbf, oml = _lower_bounds(hgrn_lb_logits.astype(F32))

    e_rows = lax.broadcasted_iota(jnp.int32, (C_SUB * C_KEY_DIM, C_CHUNK), 0) // C_KEY_DIM
    e_cols = lax.broadcasted_iota(jnp.int32, (C_SUB * C_KEY_DIM, C_CHUNK), 1) % C_SUB
    e_mat = (e_rows == e_cols).astype(BF16)

    h = jnp.concatenate([x, ctx], axis=1)
    n = _norm_mod(h, norm_g[0:1], mods[0], t_lat)
    out = None
    for l in range(depth):
        last = l == depth - 1
        w_ab, b_ab = _pack_cols(w_in[l], b_in[l], src, groups["ab"])
        w_c, b_c = _pack_cols(w_in[l], b_in[l], src, groups["c"])
        g0, gw = src["gates"]
        w_gate = jnp.pad(w_in[l][:, g0:g0 + gw], ((0, 0), (0, GATE_PAD - gw))).astype(BF16)
        b_gate = jnp.pad(b_in[l][g0:g0 + gw], (0, GATE_PAD - gw))[None, :]

        n_cm = jnp.concatenate([_grid_transpose(n[:, :t_lat], rows, GRID_W), n[:, t_lat:]], axis=1)
        n2d = n.reshape(b * tt, d)
        p2d = _in_proj(n2d, w_ab, b_ab, 1024, 1664, "in_proj_ab")
        pc = _in_proj(n_cm.reshape(b * tt, d), w_c, b_c, 1024, 1280, "in_proj_c").reshape(b, tt, -1)
        gates = _in_proj(n2d, w_gate, b_gate, 2048, GATE_PAD, "in_proj_gates").reshape(b, tt, GATE_PAD)
        p3d = p2d.reshape(b, tt, -1)

        ya = _chunk_mlp(p2d, w_spatial[l].astype(BF16), b_spatial[l].T, cols["a_u"], a_width).reshape(b, tt, a_width)

        q = _conv_silu(p3d, conv_qk[l], t_lat, cols["b_q"], 0, b_width, 1.0, transpose=False)
        k_t = _conv_silu(p3d, conv_qk[l], t_lat, cols["b_k"], b_width, b_width,
                         (b_width // B_HEADS) ** -0.5, transpose=True)
        gates_row = jnp.swapaxes(gates[:, :, :4 * B_HEADS], 1, 2)
        hf = _mlstm_dir(q, k_t, p3d, gates_row, t_lat, cols, reverse=False)
        yb = _mlstm_dir(q, k_t, p3d, gates_row, t_lat, cols, reverse=True, hf=hf, norm_g=mlstm_norm[l:l + 1])

        lbf_l, oml_l = lbf[l:l + 1], oml[l:l + 1]
        of = _hgrn_dir(pc, lbf_l, oml_l, e_mat, cols, t_lat, False)
        yc_cm = _hgrn_dir(pc, lbf_l, oml_l, e_mat, cols, t_lat, True, of=of, norm_g=hgrn_norm[l:l + 1])
        yc = jnp.concatenate([_grid_transpose(yc_cm[:, :t_lat], GRID_W, rows), yc_cm[:, t_lat:]], axis=1)

        w_o = w_out[l].astype(BF16)
        if last:
            out = _out_proj(ya, yb, yc, w_o, h, mods[l], final_norm[None, :], None, t_lat, final=True)
        else:
            h, n = _out_proj(ya, yb, yc, w_o, h, mods[l], norm_g[l + 1:l + 2], mods[l + 1], t_lat, final=False)
    return out
```

```python
import functools

import jax
import jax.numpy as jnp
from jax import lax
from jax.experimental import pallas as pl
from jax.experimental.pallas import tpu as pltpu

EPS = 1e-6
NEG_BIG = -1e30
LB_FLOOR = 1e-30
GRID_W = 64
CONV_W = 3

A_GROUPS = 4
A_CHUNK = 128
B_HEADS = 4
B_CHUNK = 128
C_HEADS = 4
C_KEY_DIM = 128
C_CHUNK = 64
C_SUB = 8
GATE_PAD = 128
MLSTM_AUG = 128

V7X_VMEM_LIMIT = 56 * 1024 * 1024

F32 = jnp.float32
BF16 = jnp.bfloat16


def _params(*sem):
    return pltpu.CompilerParams(dimension_semantics=sem, vmem_limit_bytes=V7X_VMEM_LIMIT)


def _sigmoid(x):
    return 1.0 / (1.0 + jnp.exp(-x))


def _silu(x):
    return x * _sigmoid(x)


def _log_sigmoid(x):
    return jnp.minimum(x, 0.0) - jnp.log1p(jnp.exp(-jnp.abs(x)))


def _dot(a, b):
    return jnp.dot(a, b, preferred_element_type=F32)


def _dot_nt(a, b):
    return lax.dot_general(a, b, (((1,), (1,)), ((), ())), preferred_element_type=F32)


def _dot_tn(a, b):
    return lax.dot_general(a, b, (((0,), (0,)), ((), ())), preferred_element_type=F32)


def _lb_kernel(x_ref, lbf_ref, oml_ref):
    x = x_ref[...]
    depth = x.shape[0]
    e = jnp.exp(x - jnp.max(x, axis=0, keepdims=True))
    p = e / jnp.sum(e, axis=0, keepdims=True)
    rows = lax.broadcasted_iota(jnp.int32, x.shape, 0)
    lb = jnp.zeros_like(x)
    for j in range(1, depth):
        lb = lb + jnp.where(rows >= j, p[j:j + 1, :], 0.0)
    lbf_ref[...] = jnp.maximum(lb, LB_FLOOR)
    oml_ref[...] = 1.0 - lb


def _lower_bounds(logits):
    shp = jax.ShapeDtypeStruct(logits.shape, F32)
    return pl.pallas_call(_lb_kernel, out_shape=(shp, shp), name="hgrn_lower_bounds")(logits)


def _ada_kernel(c_ref, w_ref, b_ref, o_ref):
    s = _silu(c_ref[...]).astype(BF16)
    o_ref[0] = _dot(s, w_ref[0].astype(BF16)) + b_ref[0]


def _ada_mod(cond, w_ada, b_ada, tn=768):
    depth, d, n3 = w_ada.shape
    r = cond.shape[0]
    return pl.pallas_call(
        _ada_kernel,
        grid=(depth, n3 // tn),
        in_specs=[pl.BlockSpec((r, d), lambda l, j: (0, 0)),
                  pl.BlockSpec((1, d, tn), lambda l, j: (l, 0, j)),
                  pl.BlockSpec((1, 1, tn), lambda l, j: (l, 0, j))],
        out_specs=pl.BlockSpec((1, r, tn), lambda l, j: (l, 0, j)),
        out_shape=jax.ShapeDtypeStruct((depth, r, n3), F32),
        compiler_params=_params("parallel", "parallel"),
        name="adaln_mod",
    )(cond, w_ada, b_ada.reshape(depth, 1, n3))


def _modulated_norm(h, g, mod, d):
    y = h * lax.rsqrt(jnp.mean(h * h, axis=-1, keepdims=True) + EPS) * g
    return y * (1.0 + mod[:, d:2 * d]) + mod[:, 0:d]


def _norm_kernel(h_ref, g_ref, mod_ref, n_ref):
    d = h_ref.shape[-1]
    n_ref[0] = _modulated_norm(h_ref[0], g_ref[...], mod_ref[0], d).astype(n_ref.dtype)


def _mod_index(tiles_lat):
    return lambda b, j: (2 * b + jnp.where(j >= tiles_lat, 1, 0), 0, 0)


def _norm_mod(h, g, mod, t_lat, tm=256):
    b, tt, d = h.shape
    return pl.pallas_call(
        _norm_kernel,
        grid=(b, tt // tm),
        in_specs=[pl.BlockSpec((1, tm, d), lambda i, j: (i, j, 0)),
                  pl.BlockSpec((1, d), lambda i, j: (0, 0)),
                  pl.BlockSpec((1, 1, 3 * d), _mod_index(t_lat // tm))],
        out_specs=pl.BlockSpec((1, tm, d), lambda i, j: (i, j, 0)),
        out_shape=jax.ShapeDtypeStruct((b, tt, d), BF16),
        compiler_params=_params("parallel", "parallel"),
        name="norm_mod",
    )(h, g, mod)


def _matmul_bias_kernel(x_ref, w_ref, b_ref, o_ref):
    o_ref[...] = (_dot(x_ref[...], w_ref[...]) + b_ref[...]).astype(o_ref.dtype)


def _tile(m, pref, unit=128):
    t = min(pref, m) // unit * unit
    while m % t:
        t -= unit
    return t


def _in_proj(n2d, w, bias, tm, tn, name, out_dtype=F32):
    m, d = n2d.shape
    n = w.shape[1]
    tm = _tile(m, tm)
    return pl.pallas_call(
        _matmul_bias_kernel,
        grid=(n // tn, m // tm),
        in_specs=[pl.BlockSpec((tm, d), lambda j, i: (i, 0)),
                  pl.BlockSpec((d, tn), lambda j, i: (0, j)),
                  pl.BlockSpec((1, tn), lambda j, i: (0, j))],
        out_specs=pl.BlockSpec((tm, tn), lambda j, i: (i, j)),
        out_shape=jax.ShapeDtypeStruct((m, n), out_dtype),
        compiler_params=_params("parallel", "parallel"),
        name=name,
    )(n2d, w, bias)


def _gates_kernel(x_ref, w_ref, b_ref, o_ref):
    o_ref[0] = _dot_nt(w_ref[...], x_ref[0]) + b_ref[...]


def _in_proj_gates_rows(n, w_t, bias_col):
    b, tt, d = n.shape
    ng = w_t.shape[0]
    tm = _tile(tt, 2560)
    return pl.pallas_call(
        _gates_kernel,
        grid=(b, tt // tm),
        in_specs=[pl.BlockSpec((1, tm, d), lambda i, j: (i, j, 0)),
                  pl.BlockSpec((ng, d), lambda i, j: (0, 0)),
                  pl.BlockSpec((ng, 1), lambda i, j: (0, 0))],
        out_specs=pl.BlockSpec((1, ng, tm), lambda i, j: (i, 0, j)),
        out_shape=jax.ShapeDtypeStruct((b, ng, tt), F32),
        compiler_params=_params("parallel", "parallel"),
        name="in_proj_gates",
    )(n, w_t, bias_col)


def _chunk_mlp_kernel(u_ref, v_ref, z_ref, ws_ref, bs_ref, y_ref):
    tm, width = v_ref.shape
    gd = width // A_GROUPS
    for c in range(tm // A_CHUNK):
        rows = slice(c * A_CHUNK, (c + 1) * A_CHUNK)
        for g in range(A_GROUPS):
            cols = slice(g * gd, (g + 1) * gd)
            v = v_ref[rows, cols].astype(F32)
            mu = jnp.mean(v, axis=-1, keepdims=True)
            vc = v - mu
            var = jnp.mean(vc * vc, axis=-1, keepdims=True)
            vn = (vc * lax.rsqrt(var + EPS)).astype(BF16)
            mixed = _dot(ws_ref[g], vn) + bs_ref[:, g:g + 1]
            gate = _silu(z_ref[rows, cols].astype(F32))
            y_ref[rows, cols] = (u_ref[rows, cols].astype(F32) * mixed * gate).astype(y_ref.dtype)


def _chunk_mlp(p2d, ws, bs_t, col_u, width, tm=1024):
    m = p2d.shape[0]
    tm = _tile(m, tm)
    cb = col_u // width
    spec = lambda k: pl.BlockSpec((tm, width), lambda i, k=k: (i, cb + k))
    return pl.pallas_call(
        _chunk_mlp_kernel,
        grid=(m // tm,),
        in_specs=[spec(0), spec(1), spec(2),
                  pl.BlockSpec(ws.shape, lambda i: (0, 0, 0)),
                  pl.BlockSpec(bs_t.shape, lambda i: (0, 0))],
        out_specs=pl.BlockSpec((tm, width), lambda i: (i, 0)),
        out_shape=jax.ShapeDtypeStruct((m, width), BF16),
        compiler_params=_params("parallel"),
        name="chunk_mlp",
    )(p2d, p2d, p2d, ws, bs_t)


def _conv_kernel(x_ref, w_ref, o_ref, *, t_lat, scale, transpose):
    x = x_ref[0].astype(F32)
    tt = x.shape[0]
    rows = lax.broadcasted_iota(jnp.int32, x.shape, 0)
    prev = jnp.where((rows == 0) | (rows == t_lat), 0.0, pltpu.roll(x, 1, axis=0))
    nxt = jnp.where((rows == t_lat - 1) | (rows == tt - 1), 0.0, pltpu.roll(x, tt - 1, axis=0))
    w = w_ref[...]
    y = _silu(w[0:1] * prev + w[1:2] * x + w[2:3] * nxt) * scale
    o_ref[0] = (y.T if transpose else y).astype(o_ref.dtype)


def _conv_silu(p3d, conv_w, t_lat, col0, wcol0, width, scale, transpose, tc=256):
    b, tt, _ = p3d.shape
    kern = functools.partial(_conv_kernel, t_lat=t_lat, scale=scale, transpose=transpose)
    if transpose:
        out_spec = pl.BlockSpec((1, tc, tt), lambda i, j: (i, j, 0))
        out_shape = jax.ShapeDtypeStruct((b, width, tt), BF16)
    else:
        out_spec = pl.BlockSpec((1, tt, tc), lambda i, j: (i, 0, j))
        out_shape = jax.ShapeDtypeStruct((b, tt, width), BF16)
    return pl.pallas_call(
        kern,
        grid=(b, width // tc),
        in_specs=[pl.BlockSpec((1, tt, tc), lambda i, j: (i, 0, col0 // tc + j)),
                  pl.BlockSpec((CONV_W, tc), lambda i, j: (0, wcol0 // tc + j))],
        out_specs=out_spec,
        out_shape=out_shape,
        compiler_params=_params("parallel", "parallel"),
        name="conv_k_t" if transpose else "conv_q",
    )(p3d, conv_w)


def _mlstm_kernel(*refs, reverse, readout, hd):
    if readout:
        q_ref, kt_ref, v_ref, gr_ref, hf_ref, o_ref, z_ref, g_ref, out_ref, c_scr, m_scr = refs
    else:
        q_ref, kt_ref, v_ref, gr_ref, out_ref, c_scr, m_scr = refs

    @pl.when(pl.program_id(1) == 0)
    def _():
        c_scr[...] = jnp.zeros_like(c_scr)
        m_scr[...] = jnp.zeros_like(m_scr)

    L = B_CHUNK
    row = lax.broadcasted_iota(jnp.int32, (L, L), 0)
    col = lax.broadcasted_iota(jnp.int32, (L, L), 1)
    seen = (col >= row) if reverse else (col <= row)
    tri_t = jnp.where((row >= col) if reverse else (row <= col), 1.0, 0.0).astype(BF16)
    last = 0 if reverse else L - 1
    gi, gf = (2 * B_HEADS, 3 * B_HEADS) if reverse else (0, B_HEADS)
    heads = range(B_HEADS)
    cols = [slice(h * hd, (h + 1) * hd) for h in heads]
    ones_blk = jnp.ones((L, MLSTM_AUG), BF16)
    gr = gr_ref[0]

    i_rows = gr[gi:gi + B_HEADS, :]
    f_rows = _log_sigmoid(gr[gf:gf + B_HEADS, :])
    hi = f_rows.astype(BF16).astype(F32)
    r1 = f_rows - hi
    mid = r1.astype(BF16).astype(F32)
    lo = (r1 - mid).astype(BF16).astype(F32)
    terms = jnp.concatenate([hi, mid, lo, jnp.zeros_like(hi)], axis=0).astype(BF16)
    sums = _dot(terms, tri_t)
    cb_rows = sums[0:B_HEADS] + sums[B_HEADS:2 * B_HEADS] + sums[2 * B_HEADS:3 * B_HEADS]

    q, v_aug, s_bf, e_inter, emt, ws, ec, m_new = [], [], [], [], [], [], [], []
    for h in heads:
        q.append(q_ref[0, :, cols[h]])
        v_aug.append(jnp.concatenate([v_ref[0, :, cols[h]].astype(BF16), ones_blk], axis=1))
        f_row, i_row, cb_row = f_rows[h:h + 1], i_rows[h:h + 1], cb_rows[h:h + 1]
        cb_col = jnp.sum(jnp.where(seen, f_row, 0.0), axis=1, keepdims=True)
        dmat = jnp.where(seen, cb_col - cb_row + i_row, NEG_BIG)
        m = m_scr[h, 0:1, 0:1]
        inter = cb_col + m
        mt = jnp.maximum(inter, jnp.max(dmat, axis=1, keepdims=True))
        s_bf.append((_dot(q[h], kt_ref[0, cols[h], :]) * jnp.exp(dmat - mt)).astype(BF16))
        e_inter.append(jnp.exp(inter - mt))
        emt.append(jnp.exp(-mt))
        cl = cb_row[:, last:last + 1]
        w_log = cl - cb_row + i_row
        m_new.append(jnp.maximum(cl + m, jnp.max(w_log, axis=1, keepdims=True)))
        ec.append(jnp.exp(cl + m - m_new[h]))
        ws.append(jnp.exp(w_log - m_new[h]))

    for h in heads:
        c_state = c_scr[h]
        tot = _dot(s_bf[h], v_aug[h]) + e_inter[h] * _dot(q[h], c_state.astype(BF16))
        inv = 1.0 / jnp.maximum(jnp.abs(tot[:, hd:hd + MLSTM_AUG]), emt[h])
        hc = jnp.concatenate([tot[:, c0:c0 + MLSTM_AUG] * inv for c0 in range(0, hd, MLSTM_AUG)], axis=1)
        kw_t = (kt_ref[0, cols[h], :].astype(F32) * ws[h]).astype(BF16)
        c_scr[h] = ec[h] * c_state + _dot(kw_t, v_aug[h])
        m_scr[h] = jnp.broadcast_to(m_new[h], m_scr.shape[1:])
        if readout:
            hs = hf_ref[0, :, cols[h]] + hc
            hn = hs * lax.rsqrt(jnp.mean(hs * hs, axis=-1, keepdims=True) + EPS) * g_ref[:, cols[h]]
            y = hn * _sigmoid(o_ref[0, :, cols[h]].astype(F32)) * _silu(z_ref[0, :, cols[h]].astype(F32))
            out_ref[0, :, cols[h]] = y.astype(out_ref.dtype)
        else:
            out_ref[0, :, cols[h]] = hc.astype(out_ref.dtype)


def _mlstm_dir(q, k_t, p3d, gates_row, t_lat, cols, reverse, hf=None, norm_g=None):
    b, tt, width = q.shape
    hd = width // B_HEADS
    L = B_CHUNK
    nch, nlat = tt // L, t_lat // L
    if reverse:
        chunk = lambda i: nch - 1 - i
    else:
        chunk = lambda i: lax.rem(i + nlat, nch)
    wb = lambda name: cols[name] // width
    tok = lambda cb: pl.BlockSpec((1, L, width), lambda bi, i, cb=cb: (bi, chunk(i), cb))
    in_specs = [tok(0), pl.BlockSpec((1, width, L), lambda bi, i: (bi, 0, chunk(i))), tok(wb("b_v")),
                pl.BlockSpec((1, gates_row.shape[1], L), lambda bi, i: (bi, 0, chunk(i)))]
    args = [q, k_t, p3d, gates_row]
    readout = hf is not None
    if readout:
        in_specs += [tok(0), tok(wb("b_o")), tok(wb("b_z")), pl.BlockSpec((1, width), lambda bi, i: (0, 0))]
        args += [hf, p3d, p3d, norm_g]
    kern = functools.partial(_mlstm_kernel, reverse=reverse, readout=readout, hd=hd)
    return pl.pallas_call(
        kern,
        grid=(b, nch),
        in_specs=in_specs,
        out_specs=tok(0),
        out_shape=jax.ShapeDtypeStruct((b, tt, width), BF16),
        scratch_shapes=[pltpu.VMEM((B_HEADS, hd, hd + MLSTM_AUG), F32),
                        pltpu.VMEM((B_HEADS, 8, 128), F32)],
        compiler_params=_params("parallel", "arbitrary"),
        name="mlstm_bwd_readout" if readout else "mlstm_fwd",
    )(*args)


def _hgrn_kernel(*refs, reverse, readout, dv):
    if readout:
        q_ref, z_ref, v_ref, lbf_ref, oml_ref, e_ref, of_ref, cg_ref, g_ref, out_ref, s_scr = refs
    else:
        q_ref, z_ref, v_ref, lbf_ref, oml_ref, e_ref, out_ref, s_scr = refs

    @pl.when(pl.program_id(1) == 0)
    def _():
        s_scr[...] = jnp.zeros_like(s_scr)

    L, dk, nb = C_CHUNK, C_KEY_DIM, C_CHUNK // C_SUB
    row = lax.broadcasted_iota(jnp.int32, (L, L), 0)
    col = lax.broadcasted_iota(jnp.int32, (L, L), 1)
    rb, cbk = row // C_SUB, col // C_SUB
    if reverse:
        seen, blk_before, last = col >= row, cbk > rb, 0
    else:
        seen, blk_before, last = col <= row, cbk < rb, L - 1
    sum_mat = jnp.concatenate([jnp.where(seen, 1.0, 0.0), jnp.where(blk_before, 1.0, 0.0)], axis=0).astype(BF16)
    same_blk = rb == cbk
    sub = lax.broadcasted_iota(jnp.int32, (L, dk), 0) % C_SUB

    def bcast_sub(x, j):
        x3 = x.reshape(nb, C_SUB, x.shape[-1])
        return jnp.broadcast_to(x3[:, j:j + 1, :], x3.shape).reshape(x.shape)

    heads = range(C_HEADS)
    kcs = [slice(h * dk, (h + 1) * dk) for h in heads]
    vcs = [slice(h * dv, (h + 1) * dv) for h in heads]

    q, k, v, c3 = [], [], [], []
    for h in heads:
        q.append(_silu(q_ref[0, :, kcs[h]].astype(F32)))
        z = z_ref[0, :, kcs[h]]
        v.append(v_ref[0, :, vcs[h]].astype(BF16))
        a = jnp.exp(-jnp.abs(z))
        r = 1.0 / (1.0 + a)
        pos = z >= 0.0
        oml = oml_ref[:, kcs[h]]
        f = lbf_ref[:, kcs[h]] + oml * jnp.where(pos, r, a * r)
        k.append(oml * jnp.where(pos, a * r, r))
        lf = jnp.log2(f)
        hi = lf.astype(BF16)
        r1 = lf - hi.astype(F32)
        mid = r1.astype(BF16)
        lo = (r1 - mid.astype(F32)).astype(BF16)
        c3.append(_dot(sum_mat, jnp.concatenate([hi, mid, lo], axis=1)))

    cb, a_off, a_diag = [], [], []
    for h in heads:
        c1 = c3[h][:, 0:dk] + c3[h][:, dk:2 * dk] + c3[h][:, 2 * dk:3 * dk]
        cbh, entry = c1[0:L], c1[L:2 * L]
        cb.append(cbh)
        qd = (q[h] * jnp.exp2(cbh - entry)).astype(BF16)
        parts = []
        for i in range(nb):
            lo_r, hi_r = ((i + 1) * C_SUB, L) if reverse else (0, i * C_SUB)
            if hi_r == lo_r:
                parts.append(jnp.zeros((C_SUB, L), F32))
                continue
            ent = entry[i * C_SUB:i * C_SUB + 1, :]
            kd = (k[h][lo_r:hi_r] * jnp.exp2(ent - cbh[lo_r:hi_r])).astype(BF16)
            pad = [jnp.zeros((n, dk), BF16) for n in (lo_r, L - hi_r)]
            kd = jnp.concatenate([p for p in (pad[0], kd, pad[1]) if p.shape[0]], axis=0)
            parts.append(_dot_nt(qd[i * C_SUB:(i + 1) * C_SUB], kd))
        a_off.append(jnp.concatenate(parts, axis=0))
        qk_parts = []
        for j in range(C_SUB):
            ok = (sub <= j) if reverse else (sub >= j)
            dec = jnp.exp2(jnp.where(ok, cbh - bcast_sub(cbh, j), NEG_BIG))
            qk_parts.append((q[h] * bcast_sub(k[h], j) * dec).astype(BF16))
        a_diag.append(_dot(jnp.concatenate(qk_parts, axis=1), e_ref[...]))

    for h in heads:
        scores = a_off[h] + jnp.where(same_blk, a_diag[h], 0.0)
        st = s_scr[h]
        o = _dot(scores.astype(BF16), v[h]) + _dot_nt((q[h] * jnp.exp2(cb[h])).astype(BF16), st.astype(BF16))
        cl = cb[h][last:last + 1, :]
        kdec = (k[h] * jnp.exp2(cl - cb[h])).astype(BF16)
        s_scr[h] = st * jnp.exp2(cl) + _dot_tn(v[h], kdec)
        if readout:
            os_ = of_ref[0, :, vcs[h]] + o
            on = os_ * lax.rsqrt(jnp.mean(os_ * os_, axis=-1, keepdims=True) + EPS) * g_ref[:, vcs[h]]
            out_ref[0, :, vcs[h]] = (on * _silu(cg_ref[0, :, vcs[h]].astype(F32))).astype(out_ref.dtype)
        else:
            out_ref[0, :, vcs[h]] = o.astype(out_ref.dtype)


def _hgrn_dir(pcq, pcf, lbf, oml, e_mat, cols, t_lat, reverse, of=None, norm_g=None):
    b, tt, _ = pcq.shape
    kw, L = C_HEADS * C_KEY_DIM, C_CHUNK
    vw = cols["c_g"] - cols["c_i"]
    dv = vw // C_HEADS
    nch, nlat = tt // L, t_lat // L
    if reverse:
        chunk = lambda i: nch - 1 - i
    else:
        chunk = lambda i: lax.rem(i + nlat, nch)
    tok = lambda c0, w: pl.BlockSpec((1, L, w), lambda bi, i: (bi, chunk(i), c0 // w))
    const2 = lambda x: pl.BlockSpec(x.shape, lambda bi, i: (0, 0))
    f_name = "c_f_bwd" if reverse else "c_f_fwd"
    in_specs = [tok(cols["c_q"], kw), tok(cols[f_name], kw), tok(cols["c_i"], vw),
                const2(lbf), const2(oml), const2(e_mat)]
    args = [pcq, pcf, pcq, lbf, oml, e_mat]
    readout = of is not None
    if readout:
        in_specs += [tok(0, vw), tok(cols["c_g"], vw), const2(norm_g)]
        args += [of, pcq, norm_g]
    kern = functools.partial(_hgrn_kernel, reverse=reverse, readout=readout, dv=dv)
    return pl.pallas_call(
        kern,
        grid=(b, nch),
        in_specs=in_specs,
        out_specs=tok(0, vw),
        out_shape=jax.ShapeDtypeStruct((b, tt, vw), BF16),
        scratch_shapes=[pltpu.VMEM((C_HEADS, dv, C_KEY_DIM), F32)],
        compiler_params=_params("parallel", "arbitrary"),
        name="hgrn_bwd_readout" if readout else "hgrn_fwd",
    )(*args)


def _out_kernel(ya_ref, yb_ref, yc_ref, w_ref, h_ref, mod_ref, *rest, final, wa, wb):
    d = h_ref.shape[-1]
    w = w_ref
    y = (_dot(ya_ref[0], w[0:wa, :]) + _dot(yb_ref[0], w[wa:wa + wb, :]) + _dot(yc_ref[0], w[wa + wb:, :]))
    h_new = h_ref[0] + mod_ref[0][:, 2 * d:3 * d] * y
    if final:
        g_ref, out_ref = rest
        out_ref[0] = h_new * lax.rsqrt(jnp.mean(h_new * h_new, axis=-1, keepdims=True) + EPS) * g_ref[...]
    else:
        g_ref, modn_ref, h_out_ref, n_ref = rest
        h_out_ref[0] = h_new
        n_ref[0] = _modulated_norm(h_new, g_ref[...], modn_ref[0], d).astype(n_ref.dtype)


def _out_proj(ya, yb, yc, w_out, h, mod, g_next, mod_next, t_lat, final, tm=256):
    b, tt, d = h.shape
    wa, wb, wc = ya.shape[-1], yb.shape[-1], yc.shape[-1]
    tok = lambda w: pl.BlockSpec((1, tm, w), lambda i, j: (i, j, 0))
    mod_spec = pl.BlockSpec((1, 1, 3 * d), _mod_index(t_lat // tm))
    in_specs = [tok(wa), tok(wb), tok(wc), pl.BlockSpec(w_out.shape, lambda i, j: (0, 0)), tok(d), mod_spec,
                pl.BlockSpec((1, d), lambda i, j: (0, 0))]
    args = [ya, yb, yc, w_out, h, mod, g_next]
    kern = functools.partial(_out_kernel, final=final, wa=wa, wb=wb)
    if final:
        return pl.pallas_call(
            kern, grid=(b, t_lat // tm), in_specs=in_specs, out_specs=tok(d),
            out_shape=jax.ShapeDtypeStruct((b, t_lat, d), F32),
            compiler_params=_params("parallel", "parallel"), name="out_proj_final",
        )(*args)
    return pl.pallas_call(
        kern, grid=(b, tt // tm), in_specs=in_specs + [mod_spec], out_specs=(tok(d), tok(d)),
        out_shape=(jax.ShapeDtypeStruct((b, tt, d), F32), jax.ShapeDtypeStruct((b, tt, d), BF16)),
        compiler_params=_params("parallel", "parallel"), name="out_proj",
    )(*args, mod_next)


def _packed_layout(d):
    a, bw, c, kq = d // 4, d // 2, d // 4, C_HEADS * C_KEY_DIM
    ref_order = (("a_u", a), ("a_v", a), ("a_z", a), ("b_q", bw), ("b_k", bw), ("b_v", bw), ("b_o", bw),
                 ("b_z", bw), ("gates", 4 * B_HEADS), ("c_q", kq), ("c_f_fwd", kq), ("c_f_bwd", kq),
                 ("c_i", c), ("c_g", c))
    src, start = {}, 0
    for name, w in ref_order:
        src[name] = (start, w)
        start += w
    groups = {"ab": ("b_q", "b_k", "b_v", "b_o", "b_z", "a_u", "a_v", "a_z"),
              "cq": ("c_q", "c_i", "c_g"), "cf": ("c_f_fwd", "c_f_bwd")}
    cols = {}
    for names in groups.values():
        pos = 0
        for name in names:
            cols[name] = pos
            pos += src[name][1]
    return src, groups, cols


def _pack_cols(w, bias, src, names):
    pick = lambda a: jnp.concatenate([a[..., src[k][0]:src[k][0] + src[k][1]] for k in names], axis=-1)
    return pick(w).astype(BF16), pick(bias)[None, :]


def _grid_transpose(x, rows, width):
    b, _, f = x.shape
    return x.reshape(b, rows, width, f).swapaxes(1, 2).reshape(b, rows * width, f)


def kernel(x, c, ctx, c_ctx, w_ada, b_ada, norm_g, w_in, b_in, w_spatial, b_spatial, conv_qk, mlstm_norm,
           hgrn_lb_logits, hgrn_norm, w_out, final_norm):
    b, t_lat, d = x.shape
    t_ctx = ctx.shape[1]
    tt = t_lat + t_ctx
    depth = w_ada.shape[0]
    src, groups, cols = _packed_layout(d)
    a_width, b_width = d // 4, d // 2
    rows = t_lat // GRID_W

    r_pad = -(-(b + 1) // 8) * 8
    cond = jnp.concatenate([c, c_ctx[None, :], jnp.zeros((r_pad - b - 1, d), F32)], axis=0)
    mod_all = _ada_mod(cond, w_ada, b_ada)
    mods = [jnp.stack([mod_all[l, :b], jnp.broadcast_to(mod_all[l, b], (b, 3 * d))], axis=1).reshape(2 * b, 1, 3 * d)
            for l in range(depth)]

    lbf, oml = _lower_bounds(hgrn_lb_logits.astype(F32))

    e_rows = lax.broadcasted_iota(jnp.int32, (C_SUB * C_KEY_DIM, C_CHUNK), 0) // C_KEY_DIM
    e_cols = lax.broadcasted_iota(jnp.int32, (C_SUB * C_KEY_DIM, C_CHUNK), 1) % C_SUB
    e_mat = (e_rows == e_cols).astype(BF16)

    h = jnp.concatenate([x, ctx], axis=1)
    n = _norm_mod(h, norm_g[0:1], mods[0], t_lat)
    out = None
    for l in range(depth):
        last = l == depth - 1
        w_ab, b_ab = _pack_cols(w_in[l], b_in[l], src, groups["ab"])
        w_cq, b_cq = _pack_cols(w_in[l], b_in[l], src, groups["cq"])
        w_cf, b_cf = _pack_cols(w_in[l], b_in[l], src, groups["cf"])
        g0, gw = src["gates"]
        w_gate_t = w_in[l][:, g0:g0 + gw].T.astype(BF16)
        b_gate = b_in[l][g0:g0 + gw][:, None]

        n_cm = jnp.concatenate([_grid_transpose(n[:, :t_lat], rows, GRID_W), n[:, t_lat:]], axis=1)
        n2d, n_cm2d = n.reshape(b * tt, d), n_cm.reshape(b * tt, d)
        p2d = _in_proj(n2d, w_ab, b_ab, 1024, 1664, "in_proj_ab", BF16)
        pcq = _in_proj(n_cm2d, w_cq, b_cq, 1024, 1536, "in_proj_cq", BF16).reshape(b, tt, -1)
        pcf = _in_proj(n_cm2d, w_cf, b_cf, 1024, 1024, "in_proj_cf").reshape(b, tt, -1)
        gates_row = _in_proj_gates_rows(n, w_gate_t, b_gate)
        p3d = p2d.reshape(b, tt, -1)

        ya = _chunk_mlp(p2d, w_spatial[l].astype(BF16), b_spatial[l].T, cols["a_u"], a_width).reshape(b, tt, a_width)

        q = _conv_silu(p3d, conv_qk[l], t_lat, cols["b_q"], 0, b_width, 1.0, transpose=False)
        k_t = _conv_silu(p3d, conv_qk[l], t_lat, cols["b_k"], b_width, b_width,
                         (b_width // B_HEADS) ** -0.5, transpose=True)
        hf = _mlstm_dir(q, k_t, p3d, gates_row, t_lat, cols, reverse=False)
        yb = _mlstm_dir(q, k_t, p3d, gates_row, t_lat, cols, reverse=True, hf=hf, norm_g=mlstm_norm[l:l + 1])

        lbf_l, oml_l = lbf[l:l + 1], oml[l:l + 1]
        of = _hgrn_dir(pcq, pcf, lbf_l, oml_l, e_mat, cols, t_lat, False)
        yc_cm = _hgrn_dir(pcq, pcf, lbf_l, oml_l, e_mat, cols, t_lat, True, of=of, norm_g=hgrn_norm[l:l + 1])
        yc = jnp.concatenate([_grid_transpose(yc_cm[:, :t_lat], GRID_W, rows), yc_cm[:, t_lat:]], axis=1)

        w_o = w_out[l].astype(BF16)
        if last:
            out = _out_proj(ya, yb, yc, w_o, h, mods[l], final_norm[None, :], None, t_lat, final=True)
        else:
            h, n = _out_proj(ya, yb, yc, w_o, h, mods[l], norm_g[l + 1:l + 2], mods[l + 1], t_lat, final=False)
    return out
```

```python
import functools

import jax
import jax.numpy as jnp
from jax import lax
from jax.experimental import pallas as pl
from jax.experimental.pallas import tpu as pltpu

EPS = 1e-6
NEG_BIG = -1e30
LB_FLOOR = 1e-30
GRID_W = 64
CONV_W = 3

A_GROUPS = 4
A_CHUNK = 128
B_HEADS = 4
B_CHUNK = 256
C_HEADS = 4
C_KEY_DIM = 128
C_CHUNK = 64
C_SUB = 8
GATE_PAD = 128
MLSTM_AUG = 128

V7X_VMEM_LIMIT = 56 * 1024 * 1024

F32 = jnp.float32
BF16 = jnp.bfloat16


def _params(*sem):
    return pltpu.CompilerParams(dimension_semantics=sem, vmem_limit_bytes=V7X_VMEM_LIMIT)


def _sigmoid(x):
    return 1.0 / (1.0 + jnp.exp(-x))


def _silu(x):
    return x * _sigmoid(x)


def _log_sigmoid(x):
    return jnp.minimum(x, 0.0) - jnp.log1p(jnp.exp(-jnp.abs(x)))


def _dot(a, b):
    return jnp.dot(a, b, preferred_element_type=F32)


def _dot_nt(a, b):
    return lax.dot_general(a, b, (((1,), (1,)), ((), ())), preferred_element_type=F32)


def _dot_tn(a, b):
    return lax.dot_general(a, b, (((0,), (0,)), ((), ())), preferred_element_type=F32)


def _lb_kernel(x_ref, lbf_ref, oml_ref):
    x = x_ref[...]
    depth = x.shape[0]
    e = jnp.exp(x - jnp.max(x, axis=0, keepdims=True))
    p = e / jnp.sum(e, axis=0, keepdims=True)
    rows = lax.broadcasted_iota(jnp.int32, x.shape, 0)
    lb = jnp.zeros_like(x)
    for j in range(1, depth):
        lb = lb + jnp.where(rows >= j, p[j:j + 1, :], 0.0)
    lbf_ref[...] = jnp.maximum(lb, LB_FLOOR)
    oml_ref[...] = 1.0 - lb


def _lower_bounds(logits):
    shp = jax.ShapeDtypeStruct(logits.shape, F32)
    return pl.pallas_call(_lb_kernel, out_shape=(shp, shp), name="hgrn_lower_bounds")(logits)


def _ada_kernel(c_ref, w_ref, b_ref, o_ref):
    s = _silu(c_ref[...]).astype(BF16)
    o_ref[0] = _dot(s, w_ref[0].astype(BF16)) + b_ref[0]


def _ada_mod(cond, w_ada, b_ada, tn=768):
    depth, d, n3 = w_ada.shape
    r = cond.shape[0]
    return pl.pallas_call(
        _ada_kernel,
        grid=(depth, n3 // tn),
        in_specs=[pl.BlockSpec((r, d), lambda l, j: (0, 0)),
                  pl.BlockSpec((1, d, tn), lambda l, j: (l, 0, j)),
                  pl.BlockSpec((1, 1, tn), lambda l, j: (l, 0, j))],
        out_specs=pl.BlockSpec((1, r, tn), lambda l, j: (l, 0, j)),
        out_shape=jax.ShapeDtypeStruct((depth, r, n3), F32),
        compiler_params=_params("parallel", "parallel"),
        name="adaln_mod",
    )(cond, w_ada, b_ada.reshape(depth, 1, n3))


def _modulated_norm(h, g, mod, d):
    y = h * lax.rsqrt(jnp.mean(h * h, axis=-1, keepdims=True) + EPS) * g
    return y * (1.0 + mod[:, d:2 * d]) + mod[:, 0:d]


def _norm_kernel(h_ref, g_ref, mod_ref, n_ref):
    d = h_ref.shape[-1]
    n_ref[0] = _modulated_norm(h_ref[0], g_ref[...], mod_ref[0], d).astype(n_ref.dtype)


def _mod_index(tiles_lat):
    return lambda b, j: (2 * b + jnp.where(j >= tiles_lat, 1, 0), 0, 0)


def _norm_mod(h, g, mod, t_lat, tm=256):
    b, tt, d = h.shape
    return pl.pallas_call(
        _norm_kernel,
        grid=(b, tt // tm),
        in_specs=[pl.BlockSpec((1, tm, d), lambda i, j: (i, j, 0)),
                  pl.BlockSpec((1, d), lambda i, j: (0, 0)),
                  pl.BlockSpec((1, 1, 3 * d), _mod_index(t_lat // tm))],
        out_specs=pl.BlockSpec((1, tm, d), lambda i, j: (i, j, 0)),
        out_shape=jax.ShapeDtypeStruct((b, tt, d), BF16),
        compiler_params=_params("parallel", "parallel"),
        name="norm_mod",
    )(h, g, mod)


def _matmul_bias_kernel(x_ref, w_ref, b_ref, o_ref):
    o_ref[...] = (_dot(x_ref[...], w_ref[...]) + b_ref[...]).astype(o_ref.dtype)


def _tile(m, pref, unit=128):
    t = min(pref, m) // unit * unit
    while m % t:
        t -= unit
    return t


def _in_proj(n2d, w, bias, tm, tn, name, out_dtype=F32):
    m, d = n2d.shape
    n = w.shape[1]
    tm = _tile(m, tm)
    return pl.pallas_call(
        _matmul_bias_kernel,
        grid=(n // tn, m // tm),
        in_specs=[pl.BlockSpec((tm, d), lambda j, i: (i, 0)),
                  pl.BlockSpec((d, tn), lambda j, i: (0, j)),
                  pl.BlockSpec((1, tn), lambda j, i: (0, j))],
        out_specs=pl.BlockSpec((tm, tn), lambda j, i: (i, j)),
        out_shape=jax.ShapeDtypeStruct((m, n), out_dtype),
        compiler_params=_params("parallel", "parallel"),
        name=name,
    )(n2d, w, bias)


def _gates_kernel(x_ref, w_ref, b_ref, o_ref):
    o_ref[0] = _dot_nt(w_ref[...], x_ref[0]) + b_ref[...]


def _in_proj_gates_rows(n, w_t, bias_col):
    b, tt, d = n.shape
    ng = w_t.shape[0]
    tm = _tile(tt, 2560)
    return pl.pallas_call(
        _gates_kernel,
        grid=(b, tt // tm),
        in_specs=[pl.BlockSpec((1, tm, d), lambda i, j: (i, j, 0)),
                  pl.BlockSpec((ng, d), lambda i, j: (0, 0)),
                  pl.BlockSpec((ng, 1), lambda i, j: (0, 0))],
        out_specs=pl.BlockSpec((1, ng, tm), lambda i, j: (i, 0, j)),
        out_shape=jax.ShapeDtypeStruct((b, ng, tt), F32),
        compiler_params=_params("parallel", "parallel"),
        name="in_proj_gates",
    )(n, w_t, bias_col)


def _chunk_mlp_kernel(u_ref, v_ref, z_ref, ws_ref, bs_ref, y_ref):
    tm, width = v_ref.shape
    gd = width // A_GROUPS
    for c in range(tm // A_CHUNK):
        rows = slice(c * A_CHUNK, (c + 1) * A_CHUNK)
        for g in range(A_GROUPS):
            cols = slice(g * gd, (g + 1) * gd)
            v = v_ref[rows, cols].astype(F32)
            mu = jnp.mean(v, axis=-1, keepdims=True)
            vc = v - mu
            var = jnp.mean(vc * vc, axis=-1, keepdims=True)
            vn = (vc * lax.rsqrt(var + EPS)).astype(BF16)
            mixed = _dot(ws_ref[g], vn) + bs_ref[:, g:g + 1]
            gate = _silu(z_ref[rows, cols].astype(F32))
            y_ref[rows, cols] = (u_ref[rows, cols].astype(F32) * mixed * gate).astype(y_ref.dtype)


def _chunk_mlp(p2d, ws, bs_t, col_u, width, tm=1024):
    m = p2d.shape[0]
    tm = _tile(m, tm)
    cb = col_u // width
    spec = lambda k: pl.BlockSpec((tm, width), lambda i, k=k: (i, cb + k))
    return pl.pallas_call(
        _chunk_mlp_kernel,
        grid=(m // tm,),
        in_specs=[spec(0), spec(1), spec(2),
                  pl.BlockSpec(ws.shape, lambda i: (0, 0, 0)),
                  pl.BlockSpec(bs_t.shape, lambda i: (0, 0))],
        out_specs=pl.BlockSpec((tm, width), lambda i: (i, 0)),
        out_shape=jax.ShapeDtypeStruct((m, width), BF16),
        compiler_params=_params("parallel"),
        name="chunk_mlp",
    )(p2d, p2d, p2d, ws, bs_t)


def _conv_kernel(x_ref, w_ref, o_ref, *, t_lat, scale, transpose):
    x = x_ref[0].astype(F32)
    tt = x.shape[0]
    rows = lax.broadcasted_iota(jnp.int32, x.shape, 0)
    prev = jnp.where((rows == 0) | (rows == t_lat), 0.0, pltpu.roll(x, 1, axis=0))
    nxt = jnp.where((rows == t_lat - 1) | (rows == tt - 1), 0.0, pltpu.roll(x, tt - 1, axis=0))
    w = w_ref[...]
    y = _silu(w[0:1] * prev + w[1:2] * x + w[2:3] * nxt) * scale
    o_ref[0] = (y.T if transpose else y).astype(o_ref.dtype)


def _conv_silu(p3d, conv_w, t_lat, col0, wcol0, width, scale, transpose, tc=256):
    b, tt, _ = p3d.shape
    kern = functools.partial(_conv_kernel, t_lat=t_lat, scale=scale, transpose=transpose)
    if transpose:
        out_spec = pl.BlockSpec((1, tc, tt), lambda i, j: (i, j, 0))
        out_shape = jax.ShapeDtypeStruct((b, width, tt), BF16)
    else:
        out_spec = pl.BlockSpec((1, tt, tc), lambda i, j: (i, 0, j))
        out_shape = jax.ShapeDtypeStruct((b, tt, width), BF16)
    return pl.pallas_call(
        kern,
        grid=(b, width // tc),
        in_specs=[pl.BlockSpec((1, tt, tc), lambda i, j: (i, 0, col0 // tc + j)),
                  pl.BlockSpec((CONV_W, tc), lambda i, j: (0, wcol0 // tc + j))],
        out_specs=out_spec,
        out_shape=out_shape,
        compiler_params=_params("parallel", "parallel"),
        name="conv_k_t" if transpose else "conv_q",
    )(p3d, conv_w)


def _mlstm_kernel(*refs, reverse, readout, hd):
    refs, c_scr, m_scr = refs[:-2 * B_HEADS], refs[-2 * B_HEADS:-B_HEADS], refs[-B_HEADS:]
    if readout:
        q_ref, kt_ref, v_ref, gr_ref, hf_ref, o_ref, z_ref, g_ref, out_ref = refs
    else:
        q_ref, kt_ref, v_ref, gr_ref, out_ref = refs

    @pl.when(pl.program_id(1) == 0)
    def _():
        for scr in c_scr + m_scr:
            scr[...] = jnp.zeros_like(scr)

    L = B_CHUNK
    row = lax.broadcasted_iota(jnp.int32, (L, L), 0)
    col = lax.broadcasted_iota(jnp.int32, (L, L), 1)
    seen = (col >= row) if reverse else (col <= row)
    tri_t = jnp.where((row >= col) if reverse else (row <= col), 1.0, 0.0).astype(BF16)
    last = 0 if reverse else L - 1
    gi, gf = (2 * B_HEADS, 3 * B_HEADS) if reverse else (0, B_HEADS)
    heads = range(B_HEADS)
    cols = [slice(h * hd, (h + 1) * hd) for h in heads]
    ones_blk = jnp.ones((L, MLSTM_AUG), BF16)
    gr = gr_ref[0]

    i_rows = gr[gi:gi + B_HEADS, :]
    f_rows = _log_sigmoid(gr[gf:gf + B_HEADS, :])
    hi = f_rows.astype(BF16).astype(F32)
    r1 = f_rows - hi
    mid = r1.astype(BF16).astype(F32)
    lo = (r1 - mid).astype(BF16).astype(F32)
    terms = jnp.concatenate([hi, mid, lo, jnp.zeros_like(hi)], axis=0).astype(BF16)
    sums = _dot(terms, tri_t)
    cb_rows = sums[0:B_HEADS] + sums[B_HEADS:2 * B_HEADS] + sums[2 * B_HEADS:3 * B_HEADS]

    s_bf, e_inter, emt, ws, ec, m_new = [], [], [], [], [], []
    for h in heads:
        f_row, i_row, cb_row = f_rows[h:h + 1], i_rows[h:h + 1], cb_rows[h:h + 1]
        cb_col = jnp.sum(jnp.where(seen, f_row, 0.0), axis=1, keepdims=True)
        dmat = jnp.where(seen, cb_col - cb_row + i_row, NEG_BIG)
        m = m_scr[h][0:1, 0:1]
        inter = cb_col + m
        mt = jnp.maximum(inter, jnp.max(dmat, axis=1, keepdims=True))
        s_bf.append((_dot(q_ref[0, :, cols[h]], kt_ref[0, cols[h], :]) * jnp.exp(dmat - mt)).astype(BF16))
        e_inter.append(jnp.exp(inter - mt))
        emt.append(jnp.exp(-mt))
        cl = cb_row[:, last:last + 1]
        w_log = cl - cb_row + i_row
        m_new.append(jnp.maximum(cl + m, jnp.max(w_log, axis=1, keepdims=True)))
        ec.append(jnp.exp(cl + m - m_new[h]))
        ws.append(jnp.exp(w_log - m_new[h]))

    for h in heads:
        c_state = c_scr[h][...]
        v_aug = jnp.concatenate([v_ref[0, :, cols[h]].astype(BF16), ones_blk], axis=1)
        tot = _dot(s_bf[h], v_aug) + e_inter[h] * _dot(q_ref[0, :, cols[h]], c_state.astype(BF16))
        inv = 1.0 / jnp.maximum(jnp.abs(tot[:, hd:hd + MLSTM_AUG]), emt[h])
        hc = jnp.concatenate([tot[:, c0:c0 + MLSTM_AUG] * inv for c0 in range(0, hd, MLSTM_AUG)], axis=1)
        kw_t = (kt_ref[0, cols[h], :].astype(F32) * ws[h]).astype(BF16)
        c_scr[h][...] = ec[h] * c_state + _dot(kw_t, v_aug)
        m_scr[h][...] = jnp.broadcast_to(m_new[h], m_scr[h].shape)
        if readout:
            hs = hf_ref[0, :, cols[h]] + hc
            hn = hs * lax.rsqrt(jnp.mean(hs * hs, axis=-1, keepdims=True) + EPS) * g_ref[:, cols[h]]
            y = hn * _sigmoid(o_ref[0, :, cols[h]].astype(F32)) * _silu(z_ref[0, :, cols[h]].astype(F32))
            out_ref[0, :, cols[h]] = y.astype(out_ref.dtype)
        else:
            out_ref[0, :, cols[h]] = hc.astype(out_ref.dtype)


def _mlstm_dir(q, k_t, p3d, gates_row, t_lat, cols, reverse, hf=None, norm_g=None):
    b, tt, width = q.shape
    hd = width // B_HEADS
    L = B_CHUNK
    nch, nlat = tt // L, t_lat // L
    if reverse:
        chunk = lambda i: nch - 1 - i
    else:
        chunk = lambda i: lax.rem(i + nlat, nch)
    wb = lambda name: cols[name] // width
    tok = lambda cb: pl.BlockSpec((1, L, width), lambda bi, i, cb=cb: (bi, chunk(i), cb))
    in_specs = [tok(0), pl.BlockSpec((1, width, L), lambda bi, i: (bi, 0, chunk(i))), tok(wb("b_v")),
                pl.BlockSpec((1, gates_row.shape[1], L), lambda bi, i: (bi, 0, chunk(i)))]
    args = [q, k_t, p3d, gates_row]
    readout = hf is not None
    if readout:
        in_specs += [tok(0), tok(wb("b_o")), tok(wb("b_z")), pl.BlockSpec((1, width), lambda bi, i: (0, 0))]
        args += [hf, p3d, p3d, norm_g]
    kern = functools.partial(_mlstm_kernel, reverse=reverse, readout=readout, hd=hd)
    return pl.pallas_call(
        kern,
        grid=(b, nch),
        in_specs=in_specs,
        out_specs=tok(0),
        out_shape=jax.ShapeDtypeStruct((b, tt, width), BF16),
        scratch_shapes=([pltpu.VMEM((hd, hd + MLSTM_AUG), F32)] * B_HEADS + [pltpu.VMEM((8, 128), F32)] * B_HEADS),
        compiler_params=_params("parallel", "arbitrary"),
        name="mlstm_bwd_readout" if readout else "mlstm_fwd",
    )(*args)


def _hgrn_kernel(*refs, reverse, readout, dv):
    if readout:
        q_ref, z_ref, v_ref, lbf_ref, oml_ref, e_ref, of_ref, cg_ref, g_ref, out_ref, s_scr = refs
    else:
        q_ref, z_ref, v_ref, lbf_ref, oml_ref, e_ref, out_ref, s_scr = refs

    @pl.when(pl.program_id(1) == 0)
    def _():
        s_scr[...] = jnp.zeros_like(s_scr)

    L, dk, nb = C_CHUNK, C_KEY_DIM, C_CHUNK // C_SUB
    row = lax.broadcasted_iota(jnp.int32, (L, L), 0)
    col = lax.broadcasted_iota(jnp.int32, (L, L), 1)
    rb, cbk = row // C_SUB, col // C_SUB
    if reverse:
        seen, blk_before, last = col >= row, cbk > rb, 0
    else:
        seen, blk_before, last = col <= row, cbk < rb, L - 1
    sum_mat = jnp.concatenate([jnp.where(seen, 1.0, 0.0), jnp.where(blk_before, 1.0, 0.0)], axis=0).astype(BF16)
    same_blk = rb == cbk
    sub = lax.broadcasted_iota(jnp.int32, (L, dk), 0) % C_SUB

    def bcast_sub(x, j):
        x3 = x.reshape(nb, C_SUB, x.shape[-1])
        return jnp.broadcast_to(x3[:, j:j + 1, :], x3.shape).reshape(x.shape)

    heads = range(C_HEADS)
    kcs = [slice(h * dk, (h + 1) * dk) for h in heads]
    vcs = [slice(h * dv, (h + 1) * dv) for h in heads]

    q, k, v, c3 = [], [], [], []
    for h in heads:
        q.append(_silu(q_ref[0, :, kcs[h]].astype(F32)))
        z = z_ref[0, :, kcs[h]]
        v.append(v_ref[0, :, vcs[h]].astype(BF16))
        a = jnp.exp(-jnp.abs(z))
        r = 1.0 / (1.0 + a)
        pos = z >= 0.0
        oml = oml_ref[:, kcs[h]]
        f = lbf_ref[:, kcs[h]] + oml * jnp.where(pos, r, a * r)
        k.append(oml * jnp.where(pos, a * r, r))
        lf = jnp.log2(f)
        hi = lf.astype(BF16)
        r1 = lf - hi.astype(F32)
        mid = r1.astype(BF16)
        lo = (r1 - mid.astype(F32)).astype(BF16)
        c3.append(_dot(sum_mat, jnp.concatenate([hi, mid, lo], axis=1)))

    cb, a_off, a_diag = [], [], []
    for h in heads:
        c1 = c3[h][:, 0:dk] + c3[h][:, dk:2 * dk] + c3[h][:, 2 * dk:3 * dk]
        cbh, entry = c1[0:L], c1[L:2 * L]
        cb.append(cbh)
        qd = (q[h] * jnp.exp2(cbh - entry)).astype(BF16)
        parts = []
        for i in range(nb):
            lo_r, hi_r = ((i + 1) * C_SUB, L) if reverse else (0, i * C_SUB)
            if hi_r == lo_r:
                parts.append(jnp.zeros((C_SUB, L), F32))
                continue
            ent = entry[i * C_SUB:i * C_SUB + 1, :]
            kd = (k[h][lo_r:hi_r] * jnp.exp2(ent - cbh[lo_r:hi_r])).astype(BF16)
            pad = [jnp.zeros((n, dk), BF16) for n in (lo_r, L - hi_r)]
            kd = jnp.concatenate([p for p in (pad[0], kd, pad[1]) if p.shape[0]], axis=0)
            parts.append(_dot_nt(qd[i * C_SUB:(i + 1) * C_SUB], kd))
        a_off.append(jnp.concatenate(parts, axis=0))
        qk_parts = []
        for j in range(C_SUB):
            ok = (sub <= j) if reverse else (sub >= j)
            dec = jnp.exp2(jnp.where(ok, cbh - bcast_sub(cbh, j), NEG_BIG))
            qk_parts.append((q[h] * bcast_sub(k[h], j) * dec).astype(BF16))
        a_diag.append(_dot(jnp.concatenate(qk_parts, axis=1), e_ref[...]))

    for h in heads:
        scores = a_off[h] + jnp.where(same_blk, a_diag[h], 0.0)
        st = s_scr[h]
        o = _dot(scores.astype(BF16), v[h]) + _dot_nt((q[h] * jnp.exp2(cb[h])).astype(BF16), st.astype(BF16))
        cl = cb[h][last:last + 1, :]
        kdec = (k[h] * jnp.exp2(cl - cb[h])).astype(BF16)
        s_scr[h] = st * jnp.exp2(cl) + _dot_tn(v[h], kdec)
        if readout:
            os_ = of_ref[0, :, vcs[h]] + o
            on = os_ * lax.rsqrt(jnp.mean(os_ * os_, axis=-1, keepdims=True) + EPS) * g_ref[:, vcs[h]]
            out_ref[0, :, vcs[h]] = (on * _silu(cg_ref[0, :, vcs[h]].astype(F32))).astype(out_ref.dtype)
        else:
            out_ref[0, :, vcs[h]] = o.astype(out_ref.dtype)


def _hgrn_dir(pcq, pcf, lbf, oml, e_mat, cols, t_lat, reverse, of=None, norm_g=None):
    b, tt, _ = pcq.shape
    kw, L = C_HEADS * C_KEY_DIM, C_CHUNK
    vw = cols["c_g"] - cols["c_i"]
    dv = vw // C_HEADS
    nch, nlat = tt // L, t_lat // L
    if reverse:
        chunk = lambda i: nch - 1 - i
    else:
        chunk = lambda i: lax.rem(i + nlat, nch)
    tok = lambda c0, w: pl.BlockSpec((1, L, w), lambda bi, i: (bi, chunk(i), c0 // w))
    const2 = lambda x: pl.BlockSpec(x.shape, lambda bi, i: (0, 0))
    f_name = "c_f_bwd" if reverse else "c_f_fwd"
    in_specs = [tok(cols["c_q"], kw), tok(cols[f_name], kw), tok(cols["c_i"], vw),
                const2(lbf), const2(oml), const2(e_mat)]
    args = [pcq, pcf, pcq, lbf, oml, e_mat]
    readout = of is not None
    if readout:
        in_specs += [tok(0, vw), tok(cols["c_g"], vw), const2(norm_g)]
        args += [of, pcq, norm_g]
    kern = functools.partial(_hgrn_kernel, reverse=reverse, readout=readout, dv=dv)
    return pl.pallas_call(
        kern,
        grid=(b, nch),
        in_specs=in_specs,
        out_specs=tok(0, vw),
        out_shape=jax.ShapeDtypeStruct((b, tt, vw), BF16),
        scratch_shapes=[pltpu.VMEM((C_HEADS, dv, C_KEY_DIM), F32)],
        compiler_params=_params("parallel", "arbitrary"),
        name="hgrn_bwd_readout" if readout else "hgrn_fwd",
    )(*args)


def _out_kernel(ya_ref, yb_ref, yc_ref, w_ref, h_ref, mod_ref, *rest, final, wa, wb):
    d = h_ref.shape[-1]
    w = w_ref
    y = (_dot(ya_ref[0], w[0:wa, :]) + _dot(yb_ref[0], w[wa:wa + wb, :]) + _dot(yc_ref[0], w[wa + wb:, :]))
    h_new = h_ref[0] + mod_ref[0][:, 2 * d:3 * d] * y
    if final:
        g_ref, out_ref = rest
        out_ref[0] = h_new * lax.rsqrt(jnp.mean(h_new * h_new, axis=-1, keepdims=True) + EPS) * g_ref[...]
    else:
        g_ref, modn_ref, h_out_ref, n_ref = rest
        h_out_ref[0] = h_new
        n_ref[0] = _modulated_norm(h_new, g_ref[...], modn_ref[0], d).astype(n_ref.dtype)


def _out_proj(ya, yb, yc, w_out, h, mod, g_next, mod_next, t_lat, final, tm=256):
    b, tt, d = h.shape
    wa, wb, wc = ya.shape[-1], yb.shape[-1], yc.shape[-1]
    tok = lambda w: pl.BlockSpec((1, tm, w), lambda i, j: (i, j, 0))
    mod_spec = pl.BlockSpec((1, 1, 3 * d), _mod_index(t_lat // tm))
    in_specs = [tok(wa), tok(wb), tok(wc), pl.BlockSpec(w_out.shape, lambda i, j: (0, 0)), tok(d), mod_spec,
                pl.BlockSpec((1, d), lambda i, j: (0, 0))]
    args = [ya, yb, yc, w_out, h, mod, g_next]
    kern = functools.partial(_out_kernel, final=final, wa=wa, wb=wb)
    if final:
        return pl.pallas_call(
            kern, grid=(b, t_lat // tm), in_specs=in_specs, out_specs=tok(d),
            out_shape=jax.ShapeDtypeStruct((b, t_lat, d), F32),
            compiler_params=_params("parallel", "parallel"), name="out_proj_final",
        )(*args)
    return pl.pallas_call(
        kern, grid=(b, tt // tm), in_specs=in_specs + [mod_spec], out_specs=(tok(d), tok(d)),
        out_shape=(jax.ShapeDtypeStruct((b, tt, d), F32), jax.ShapeDtypeStruct((b, tt, d), BF16)),
        compiler_params=_params("parallel", "parallel"), name="out_proj",
    )(*args, mod_next)


def _packed_layout(d):
    a, bw, c, kq = d // 4, d // 2, d // 4, C_HEADS * C_KEY_DIM
    ref_order = (("a_u", a), ("a_v", a), ("a_z", a), ("b_q", bw), ("b_k", bw), ("b_v", bw), ("b_o", bw),
                 ("b_z", bw), ("gates", 4 * B_HEADS), ("c_q", kq), ("c_f_fwd", kq), ("c_f_bwd", kq),
                 ("c_i", c), ("c_g", c))
    src, start = {}, 0
    for name, w in ref_order:
        src[name] = (start, w)
        start += w
    groups = {"ab": ("b_q", "b_k", "b_v", "b_o", "b_z", "a_u", "a_v", "a_z"),
              "cq": ("c_q", "c_i", "c_g"), "cf": ("c_f_fwd", "c_f_bwd")}
    cols = {}
    for names in groups.values():
        pos = 0
        for name in names:
            cols[name] = pos
            pos += src[name][1]
    return src, groups, cols


def _pack_cols(w, bias, src, names):
    pick = lambda a: jnp.concatenate([a[..., src[k][0]:src[k][0] + src[k][1]] for k in names], axis=-1)
    return pick(w).astype(BF16), pick(bias)[None, :]


def _grid_transpose(x, rows, width):
    b, _, f = x.shape
    return x.reshape(b, rows, width, f).swapaxes(1, 2).reshape(b, rows * width, f)


def kernel(x, c, ctx, c_ctx, w_ada, b_ada, norm_g, w_in, b_in, w_spatial, b_spatial, conv_qk, mlstm_norm,
           hgrn_lb_logits, hgrn_norm, w_out, final_norm):
    b, t_lat, d = x.shape
    t_ctx = ctx.shape[1]
    tt = t_lat + t_ctx
    depth = w_ada.shape[0]
    src, groups, cols = _packed_layout(d)
    a_width, b_width = d // 4, d // 2
    rows = t_lat // GRID_W

    r_pad = -(-(b + 1) // 8) * 8
    cond = jnp.concatenate([c, c_ctx[None, :], jnp.zeros((r_pad - b - 1, d), F32)], axis=0)
    mod_all = _ada_mod(cond, w_ada, b_ada)
    mods = [jnp.stack([mod_all[l, :b], jnp.broadcast_to(mod_all[l, b], (b, 3 * d))], axis=1).reshape(2 * b, 1, 3 * d)
            for l in range(depth)]

    lbf, oml = _lower_bounds(hgrn_lb_logits.astype(F32))

    e_rows = lax.broadcasted_iota(jnp.int32, (C_SUB * C_KEY_DIM, C_CHUNK), 0) // C_KEY_DIM
    e_cols = lax.broadcasted_iota(jnp.int32, (C_SUB * C_KEY_DIM, C_CHUNK), 1) % C_SUB
    e_mat = (e_rows == e_cols).astype(BF16)

    h = jnp.concatenate([x, ctx], axis=1)
    n = _norm_mod(h, norm_g[0:1], mods[0], t_lat)
    out = None
    for l in range(depth):
        last = l == depth - 1
        w_ab, b_ab = _pack_cols(w_in[l], b_in[l], src, groups["ab"])
        w_cq, b_cq = _pack_cols(w_in[l], b_in[l], src, groups["cq"])
        w_cf, b_cf = _pack_cols(w_in[l], b_in[l], src, groups["cf"])
        g0, gw = src["gates"]
        w_gate_t = w_in[l][:, g0:g0 + gw].T.astype(BF16)
        b_gate = b_in[l][g0:g0 + gw][:, None]

        n_cm = jnp.concatenate([_grid_transpose(n[:, :t_lat], rows, GRID_W), n[:, t_lat:]], axis=1)
        n2d, n_cm2d = n.reshape(b * tt, d), n_cm.reshape(b * tt, d)
        p2d = _in_proj(n2d, w_ab, b_ab, 1024, 1664, "in_proj_ab", BF16)
        pcq = _in_proj(n_cm2d, w_cq, b_cq, 1024, 1536, "in_proj_cq", BF16).reshape(b, tt, -1)
        pcf = _in_proj(n_cm2d, w_cf, b_cf, 1024, 1024, "in_proj_cf").reshape(b, tt, -1)
        gates_row = _in_proj_gates_rows(n, w_gate_t, b_gate)
        p3d = p2d.reshape(b, tt, -1)

        ya = _chunk_mlp(p2d, w_spatial[l].astype(BF16), b_spatial[l].T, cols["a_u"], a_width).reshape(b, tt, a_width)

        q = _conv_silu(p3d, conv_qk[l], t_lat, cols["b_q"], 0, b_width, 1.0, transpose=False)
        k_t = _conv_silu(p3d, conv_qk[l], t_lat, cols["b_k"], b_width, b_width,
                         (b_width // B_HEADS) ** -0.5, transpose=True)
        hf = _mlstm_dir(q, k_t, p3d, gates_row, t_lat, cols, reverse=False)
        yb = _mlstm_dir(q, k_t, p3d, gates_row, t_lat, cols, reverse=True, hf=hf, norm_g=mlstm_norm[l:l + 1])

        lbf_l, oml_l = lbf[l:l + 1], oml[l:l + 1]
        of = _hgrn_dir(pcq, pcf, lbf_l, oml_l, e_mat, cols, t_lat, False)
        yc_cm = _hgrn_dir(pcq, pcf, lbf_l, oml_l, e_mat, cols, t_lat, True, of=of, norm_g=hgrn_norm[l:l + 1])
        yc = jnp.concatenate([_grid_transpose(yc_cm[:, :t_lat], GRID_W, rows), yc_cm[:, t_lat:]], axis=1)

        w_o = w_out[l].astype(BF16)
        if last:
            out = _out_proj(ya, yb, yc, w_o, h, mods[l], final_norm[None, :], None, t_lat, final=True)
        else:
            h, n = _out_proj(ya, yb, yc, w_o, h, mods[l], norm_g[l + 1:l + 2], mods[l + 1], t_lat, final=False)
    return out
```

```python
import functools

import jax
import jax.numpy as jnp
from jax import lax
from jax.experimental import pallas as pl
from jax.experimental.pallas import tpu as pltpu

EPS = 1e-6
NEG_BIG = -1e30
LB_FLOOR = 1e-30
GRID_W = 64
CONV_W = 3

A_GROUPS = 4
A_CHUNK = 128
B_HEADS = 4
B_CHUNK = 256
C_HEADS = 4
C_KEY_DIM = 128
C_CHUNK = 64
C_SUB = 8
GATE_PAD = 128
MLSTM_AUG = 128

V7X_VMEM_LIMIT = 56 * 1024 * 1024

F32 = jnp.float32
BF16 = jnp.bfloat16


def _params(*sem):
    return pltpu.CompilerParams(dimension_semantics=sem, vmem_limit_bytes=V7X_VMEM_LIMIT)


def _sigmoid(x):
    return 1.0 / (1.0 + jnp.exp(-x))


def _silu(x):
    return x * _sigmoid(x)


def _log_sigmoid(x):
    return jnp.minimum(x, 0.0) - jnp.log1p(jnp.exp(-jnp.abs(x)))


def _dot(a, b):
    return jnp.dot(a, b, preferred_element_type=F32)


def _dot_nt(a, b):
    return lax.dot_general(a, b, (((1,), (1,)), ((), ())), preferred_element_type=F32)


def _dot_tn(a, b):
    return lax.dot_general(a, b, (((0,), (0,)), ((), ())), preferred_element_type=F32)


def _lb_kernel(x_ref, lbf_ref, oml_ref):
    x = x_ref[...]
    depth = x.shape[0]
    e = jnp.exp(x - jnp.max(x, axis=0, keepdims=True))
    p = e / jnp.sum(e, axis=0, keepdims=True)
    rows = lax.broadcasted_iota(jnp.int32, x.shape, 0)
    lb = jnp.zeros_like(x)
    for j in range(1, depth):
        lb = lb + jnp.where(rows >= j, p[j:j + 1, :], 0.0)
    lbf_ref[...] = jnp.maximum(lb, LB_FLOOR)
    oml_ref[...] = 1.0 - lb


def _lower_bounds(logits):
    shp = jax.ShapeDtypeStruct(logits.shape, F32)
    return pl.pallas_call(_lb_kernel, out_shape=(shp, shp), name="hgrn_lower_bounds")(logits)


def _ada_kernel(c_ref, w_ref, b_ref, o_ref):
    s = _silu(c_ref[...]).astype(BF16)
    o_ref[0] = _dot(s, w_ref[0].astype(BF16)) + b_ref[0]


def _ada_mod(cond, w_ada, b_ada, tn=768):
    depth, d, n3 = w_ada.shape
    r = cond.shape[0]
    return pl.pallas_call(
        _ada_kernel,
        grid=(depth, n3 // tn),
        in_specs=[pl.BlockSpec((r, d), lambda l, j: (0, 0)),
                  pl.BlockSpec((1, d, tn), lambda l, j: (l, 0, j)),
                  pl.BlockSpec((1, 1, tn), lambda l, j: (l, 0, j))],
        out_specs=pl.BlockSpec((1, r, tn), lambda l, j: (l, 0, j)),
        out_shape=jax.ShapeDtypeStruct((depth, r, n3), F32),
        compiler_params=_params("parallel", "parallel"),
        name="adaln_mod",
    )(cond, w_ada, b_ada.reshape(depth, 1, n3))


def _modulated_norm(h, g, mod, d):
    y = h * lax.rsqrt(jnp.mean(h * h, axis=-1, keepdims=True) + EPS) * g
    return y * (1.0 + mod[:, d:2 * d]) + mod[:, 0:d]


def _norm_kernel(h_ref, g_ref, mod_ref, n_ref):
    d = h_ref.shape[-1]
    n_ref[0] = _modulated_norm(h_ref[0], g_ref[...], mod_ref[0], d).astype(n_ref.dtype)


def _mod_index(tiles_lat):
    return lambda b, j: (2 * b + jnp.where(j >= tiles_lat, 1, 0), 0, 0)


def _norm_mod(h, g, mod, t_lat, tm=256):
    b, tt, d = h.shape
    return pl.pallas_call(
        _norm_kernel,
        grid=(b, tt // tm),
        in_specs=[pl.BlockSpec((1, tm, d), lambda i, j: (i, j, 0)),
                  pl.BlockSpec((1, d), lambda i, j: (0, 0)),
                  pl.BlockSpec((1, 1, 3 * d), _mod_index(t_lat // tm))],
        out_specs=pl.BlockSpec((1, tm, d), lambda i, j: (i, j, 0)),
        out_shape=jax.ShapeDtypeStruct((b, tt, d), BF16),
        compiler_params=_params("parallel", "parallel"),
        name="norm_mod",
    )(h, g, mod)


def _matmul_bias_kernel(x_ref, w_ref, b_ref, o_ref):
    o_ref[...] = (_dot(x_ref[...], w_ref[...]) + b_ref[...]).astype(o_ref.dtype)


def _tile(m, pref, unit=128):
    t = min(pref, m) // unit * unit
    while m % t:
        t -= unit
    return t


def _in_proj(n2d, w, bias, tm, tn, name, out_dtype=F32):
    m, d = n2d.shape
    n = w.shape[1]
    tm = _tile(m, tm)
    return pl.pallas_call(
        _matmul_bias_kernel,
        grid=(n // tn, m // tm),
        in_specs=[pl.BlockSpec((tm, d), lambda j, i: (i, 0)),
                  pl.BlockSpec((d, tn), lambda j, i: (0, j)),
                  pl.BlockSpec((1, tn), lambda j, i: (0, j))],
        out_specs=pl.BlockSpec((tm, tn), lambda j, i: (i, j)),
        out_shape=jax.ShapeDtypeStruct((m, n), out_dtype),
        compiler_params=_params("parallel", "parallel"),
        name=name,
    )(n2d, w, bias)


def _gates_kernel(x_ref, w_ref, b_ref, o_ref):
    o_ref[0] = _dot_nt(w_ref[...], x_ref[0]) + b_ref[...]


def _in_proj_gates_rows(n, w_t, bias_col):
    b, tt, d = n.shape
    ng = w_t.shape[0]
    tm = _tile(tt, 2560)
    return pl.pallas_call(
        _gates_kernel,
        grid=(b, tt // tm),
        in_specs=[pl.BlockSpec((1, tm, d), lambda i, j: (i, j, 0)),
                  pl.BlockSpec((ng, d), lambda i, j: (0, 0)),
                  pl.BlockSpec((ng, 1), lambda i, j: (0, 0))],
        out_specs=pl.BlockSpec((1, ng, tm), lambda i, j: (i, 0, j)),
        out_shape=jax.ShapeDtypeStruct((b, ng, tt), F32),
        compiler_params=_params("parallel", "parallel"),
        name="in_proj_gates",
    )(n, w_t, bias_col)


def _chunk_mlp_kernel(u_ref, v_ref, z_ref, ws_ref, bs_ref, y_ref):
    tm, width = v_ref.shape
    gd = width // A_GROUPS
    for c in range(tm // A_CHUNK):
        rows = slice(c * A_CHUNK, (c + 1) * A_CHUNK)
        for g in range(A_GROUPS):
            cols = slice(g * gd, (g + 1) * gd)
            v = v_ref[rows, cols].astype(F32)
            mu = jnp.mean(v, axis=-1, keepdims=True)
            vc = v - mu
            var = jnp.mean(vc * vc, axis=-1, keepdims=True)
            vn = (vc * lax.rsqrt(var + EPS)).astype(BF16)
            mixed = _dot(ws_ref[g], vn) + bs_ref[:, g:g + 1]
            gate = _silu(z_ref[rows, cols].astype(F32))
            y_ref[rows, cols] = (u_ref[rows, cols].astype(F32) * mixed * gate).astype(y_ref.dtype)


def _chunk_mlp(p2d, ws, bs_t, col_u, width, tm=1024):
    m = p2d.shape[0]
    tm = _tile(m, tm)
    cb = col_u // width
    spec = lambda k: pl.BlockSpec((tm, width), lambda i, k=k: (i, cb + k))
    return pl.pallas_call(
        _chunk_mlp_kernel,
        grid=(m // tm,),
        in_specs=[spec(0), spec(1), spec(2),
                  pl.BlockSpec(ws.shape, lambda i: (0, 0, 0)),
                  pl.BlockSpec(bs_t.shape, lambda i: (0, 0))],
        out_specs=pl.BlockSpec((tm, width), lambda i: (i, 0)),
        out_shape=jax.ShapeDtypeStruct((m, width), BF16),
        compiler_params=_params("parallel"),
        name="chunk_mlp",
    )(p2d, p2d, p2d, ws, bs_t)


def _conv_kernel(x_ref, w_ref, o_ref, *, t_lat, scale, transpose):
    x = x_ref[0].astype(F32)
    tt = x.shape[0]
    rows = lax.broadcasted_iota(jnp.int32, x.shape, 0)
    prev = jnp.where((rows == 0) | (rows == t_lat), 0.0, pltpu.roll(x, 1, axis=0))
    nxt = jnp.where((rows == t_lat - 1) | (rows == tt - 1), 0.0, pltpu.roll(x, tt - 1, axis=0))
    w = w_ref[...]
    y = _silu(w[0:1] * prev + w[1:2] * x + w[2:3] * nxt) * scale
    o_ref[0] = (y.T if transpose else y).astype(o_ref.dtype)


def _conv_silu(p3d, conv_w, t_lat, col0, wcol0, width, scale, transpose, tc=256):
    b, tt, _ = p3d.shape
    kern = functools.partial(_conv_kernel, t_lat=t_lat, scale=scale, transpose=transpose)
    if transpose:
        out_spec = pl.BlockSpec((1, tc, tt), lambda i, j: (i, j, 0))
        out_shape = jax.ShapeDtypeStruct((b, width, tt), BF16)
    else:
        out_spec = pl.BlockSpec((1, tt, tc), lambda i, j: (i, 0, j))
        out_shape = jax.ShapeDtypeStruct((b, tt, width), BF16)
    return pl.pallas_call(
        kern,
        grid=(b, width // tc),
        in_specs=[pl.BlockSpec((1, tt, tc), lambda i, j: (i, 0, col0 // tc + j)),
                  pl.BlockSpec((CONV_W, tc), lambda i, j: (0, wcol0 // tc + j))],
        out_specs=out_spec,
        out_shape=out_shape,
        compiler_params=_params("parallel", "parallel"),
        name="conv_k_t" if transpose else "conv_q",
    )(p3d, conv_w)


def _mlstm_kernel(*refs, reverse, readout, hd, gb):
    units = [(g, h) for g in range(gb) for h in range(B_HEADS)]
    nu = len(units)
    refs, c_scr, m_scr = refs[:-2 * nu], refs[-2 * nu:-nu], refs[-nu:]
    if readout:
        q_ref, kt_ref, v_ref, gr_ref, hf_ref, o_ref, z_ref, g_ref, out_ref = refs
    else:
        q_ref, kt_ref, v_ref, gr_ref, out_ref = refs

    @pl.when(pl.program_id(1) == 0)
    def _():
        for scr in c_scr + m_scr:
            scr[...] = jnp.zeros_like(scr)

    L = B_CHUNK
    row = lax.broadcasted_iota(jnp.int32, (L, L), 0)
    col = lax.broadcasted_iota(jnp.int32, (L, L), 1)
    seen = (col >= row) if reverse else (col <= row)
    tri_t = jnp.where((row >= col) if reverse else (row <= col), 1.0, 0.0).astype(BF16)
    last = 0 if reverse else L - 1
    gi, gf = (2 * B_HEADS, 3 * B_HEADS) if reverse else (0, B_HEADS)
    cols = [slice(h * hd, (h + 1) * hd) for h in range(B_HEADS)]
    ones_blk = jnp.ones((L, MLSTM_AUG), BF16)
    lane_tile = lambda x, width: jnp.concatenate([x] * (width // MLSTM_AUG), axis=1)
    gates = lambda lo_row: jnp.concatenate([gr_ref[g, lo_row:lo_row + B_HEADS, :] for g in range(gb)], axis=0)

    i_rows = gates(gi)
    f_rows = _log_sigmoid(gates(gf))
    hi = f_rows.astype(BF16).astype(F32)
    r1 = f_rows - hi
    mid = r1.astype(BF16).astype(F32)
    lo = (r1 - mid).astype(BF16).astype(F32)
    terms = jnp.concatenate([hi, mid, lo, jnp.zeros_like(hi)], axis=0).astype(BF16)
    sums = _dot(terms, tri_t)
    cb_rows = sums[0:nu] + sums[nu:2 * nu] + sums[2 * nu:3 * nu]
    tri = jnp.where(seen, 1.0, 0.0).astype(BF16)
    rep = lambda x, u: jnp.broadcast_to(x[u:u + 1], (MLSTM_AUG, L))
    cb_cols = []
    for u in range(nu):
        c3 = _dot_nt(tri, jnp.concatenate([rep(hi, u), rep(mid, u), rep(lo, u)], axis=0).astype(BF16))
        cb_cols.append(c3[:, 0:MLSTM_AUG] + c3[:, MLSTM_AUG:2 * MLSTM_AUG] + c3[:, 2 * MLSTM_AUG:])

    s_bf, e_inter, emt, ws, ec, m_new = [], [], [], [], [], []
    for u, (g, h) in enumerate(units):
        i_row, cb_row = i_rows[u:u + 1], cb_rows[u:u + 1]
        dmat = jnp.where(seen, lane_tile(cb_cols[u], L) - cb_row + i_row, NEG_BIG)
        m = m_scr[u][0:1, 0:1]
        inter = cb_cols[u] + m
        mt = jnp.maximum(inter, jnp.max(dmat, axis=1, keepdims=True))
        decay = jnp.exp(dmat - lane_tile(mt, L))
        s_bf.append((_dot(q_ref[g, :, cols[h]], kt_ref[g, cols[h], :]) * decay).astype(BF16))
        e_inter.append(jnp.exp(inter - mt))
        emt.append(jnp.exp(-mt))
        cl = cb_row[:, last:last + 1]
        w_log = cl - cb_row + i_row
        m_new.append(jnp.maximum(cl + m, jnp.max(w_log, axis=1, keepdims=True)))
        ec.append(jnp.exp(cl + m - m_new[u]))
        ws.append(jnp.exp(w_log - m_new[u]))

    sv, qc, upd = [], [], []
    for u, (g, h) in enumerate(units):
        v_aug = jnp.concatenate([v_ref[g, :, cols[h]].astype(BF16), ones_blk], axis=1)
        sv.append(_dot(s_bf[u], v_aug))
        qc.append(_dot(q_ref[g, :, cols[h]], c_scr[u][...].astype(BF16)))
        kw_t = (kt_ref[g, cols[h], :].astype(F32) * ws[u]).astype(BF16)
        upd.append(_dot(kw_t, v_aug))
    for u, (g, h) in enumerate(units):
        tot = sv[u] + lane_tile(e_inter[u], hd + MLSTM_AUG) * qc[u]
        inv = 1.0 / jnp.maximum(jnp.abs(tot[:, hd:hd + MLSTM_AUG]), emt[u])
        hc = jnp.concatenate([tot[:, c0:c0 + MLSTM_AUG] * inv for c0 in range(0, hd, MLSTM_AUG)], axis=1)
        c_scr[u][...] = ec[u] * c_scr[u][...] + upd[u]
        m_scr[u][...] = jnp.broadcast_to(m_new[u], m_scr[u].shape)
        if readout:
            hs = hf_ref[g, :, cols[h]] + hc
            hn = hs * lax.rsqrt(jnp.mean(hs * hs, axis=-1, keepdims=True) + EPS) * g_ref[:, cols[h]]
            y = hn * _sigmoid(o_ref[g, :, cols[h]].astype(F32)) * _silu(z_ref[g, :, cols[h]].astype(F32))
            out_ref[g, :, cols[h]] = y.astype(out_ref.dtype)
        else:
            out_ref[g, :, cols[h]] = hc.astype(out_ref.dtype)


def _mlstm_dir(q, k_t, p3d, gates_row, t_lat, cols, reverse, hf=None, norm_g=None, gb=2):
    b, tt, width = q.shape
    gb = gb if b % gb == 0 else 1
    hd = width // B_HEADS
    L = B_CHUNK
    nch, nlat = tt // L, t_lat // L
    if reverse:
        chunk = lambda i: nch - 1 - i
    else:
        chunk = lambda i: lax.rem(i + nlat, nch)
    wb = lambda name: cols[name] // width
    tok = lambda cb: pl.BlockSpec((gb, L, width), lambda bi, i, cb=cb: (bi, chunk(i), cb))
    in_specs = [tok(0), pl.BlockSpec((gb, width, L), lambda bi, i: (bi, 0, chunk(i))), tok(wb("b_v")),
                pl.BlockSpec((gb, gates_row.shape[1], L), lambda bi, i: (bi, 0, chunk(i)))]
    args = [q, k_t, p3d, gates_row]
    readout = hf is not None
    if readout:
        in_specs += [tok(0), tok(wb("b_o")), tok(wb("b_z")), pl.BlockSpec((1, width), lambda bi, i: (0, 0))]
        args += [hf, p3d, p3d, norm_g]
    kern = functools.partial(_mlstm_kernel, reverse=reverse, readout=readout, hd=hd, gb=gb)
    nu = gb * B_HEADS
    return pl.pallas_call(
        kern,
        grid=(b // gb, nch),
        in_specs=in_specs,
        out_specs=tok(0),
        out_shape=jax.ShapeDtypeStruct((b, tt, width), BF16),
        scratch_shapes=[pltpu.VMEM((hd, hd + MLSTM_AUG), F32)] * nu + [pltpu.VMEM((8, 128), F32)] * nu,
        compiler_params=_params("parallel", "arbitrary"),
        name="mlstm_bwd_readout" if readout else "mlstm_fwd",
    )(*args)


def _hgrn_kernel(*refs, reverse, readout, dv, gb):
    if readout:
        q_ref, z_ref, v_ref, lbf_ref, oml_ref, e_ref, of_ref, cg_ref, g_ref, out_ref, s_scr, row_scr = refs
    else:
        q_ref, z_ref, v_ref, lbf_ref, oml_ref, e_ref, out_ref, s_scr, row_scr = refs

    @pl.when(pl.program_id(1) == 0)
    def _():
        s_scr[...] = jnp.zeros_like(s_scr)

    L, dk, nb = C_CHUNK, C_KEY_DIM, C_CHUNK // C_SUB
    row = lax.broadcasted_iota(jnp.int32, (L, L), 0)
    col = lax.broadcasted_iota(jnp.int32, (L, L), 1)
    rb, cbk = row // C_SUB, col // C_SUB
    if reverse:
        seen, blk_before, last = col >= row, cbk > rb, 0
    else:
        seen, blk_before, last = col <= row, cbk < rb, L - 1
    sum_mat = jnp.concatenate([jnp.where(seen, 1.0, 0.0), jnp.where(blk_before, 1.0, 0.0)], axis=0).astype(BF16)
    same_blk = rb == cbk
    sub = lax.broadcasted_iota(jnp.int32, (L, dk), 0) % C_SUB
    pair_ok = [(sub <= j) if reverse else (sub >= j) for j in range(C_SUB)]

    def bcast_sub(ref, j):
        return jnp.concatenate([jnp.broadcast_to(ref[i * C_SUB + j:i * C_SUB + j + 1, :], (C_SUB, dk))
                                for i in range(nb)], axis=0)

    units = [(g, h) for g in range(gb) for h in range(C_HEADS)]
    kcs = [slice(h * dk, (h + 1) * dk) for h in range(C_HEADS)]
    vcs = [slice(h * dv, (h + 1) * dv) for h in range(C_HEADS)]

    q, k, v, c3 = [], [], [], []
    for u, (g, h) in enumerate(units):
        q.append(_silu(q_ref[g, :, kcs[h]].astype(F32)))
        z = z_ref[g, :, kcs[h]]
        v.append(v_ref[g, :, vcs[h]].astype(BF16))
        a = jnp.exp(-jnp.abs(z))
        r = 1.0 / (1.0 + a)
        pos = z >= 0.0
        oml = oml_ref[:, kcs[h]]
        f = lbf_ref[:, kcs[h]] + oml * jnp.where(pos, r, a * r)
        k.append(oml * jnp.where(pos, a * r, r))
        lf = jnp.log2(f)
        hi = lf.astype(BF16)
        r1 = lf - hi.astype(F32)
        mid = r1.astype(BF16)
        lo = (r1 - mid.astype(F32)).astype(BF16)
        c3.append(_dot(sum_mat, jnp.concatenate([hi, mid, lo], axis=1)))

    cb, a_off, a_diag = [], [], []
    for u, (g, h) in enumerate(units):
        c1 = c3[u][:, 0:dk] + c3[u][:, dk:2 * dk] + c3[u][:, 2 * dk:3 * dk]
        cbh, entry = c1[0:L], c1[L:2 * L]
        cb.append(cbh)
        qd = (q[u] * jnp.exp2(cbh - entry)).astype(BF16)
        parts = []
        for i in range(nb):
            lo_r, hi_r = ((i + 1) * C_SUB, L) if reverse else (0, i * C_SUB)
            if hi_r == lo_r:
                parts.append(jnp.zeros((C_SUB, L), F32))
                continue
            ent = entry[i * C_SUB:i * C_SUB + 1, :]
            kd = (k[u][lo_r:hi_r] * jnp.exp2(ent - cbh[lo_r:hi_r])).astype(BF16)
            pad = [jnp.zeros((n, dk), BF16) for n in (lo_r, L - hi_r)]
            kd = jnp.concatenate([p for p in (pad[0], kd, pad[1]) if p.shape[0]], axis=0)
            parts.append(_dot_nt(qd[i * C_SUB:(i + 1) * C_SUB], kd))
        a_off.append(jnp.concatenate(parts, axis=0))
        qk_parts = []
        cb_rows, k_rows = row_scr.at[2 * u], row_scr.at[2 * u + 1]
        cb_rows[...] = cbh
        k_rows[...] = k[u]
        for j in range(C_SUB):
            dec = jnp.exp2(jnp.where(pair_ok[j], cbh - bcast_sub(cb_rows, j), NEG_BIG))
            qk_parts.append((q[u] * bcast_sub(k_rows, j) * dec).astype(BF16))
        a_diag.append(_dot(jnp.concatenate(qk_parts, axis=1), e_ref[...]))

    for u, (g, h) in enumerate(units):
        scores = a_off[u] + jnp.where(same_blk, a_diag[u], 0.0)
        st = s_scr[u]
        o = _dot(scores.astype(BF16), v[u]) + _dot_nt((q[u] * jnp.exp2(cb[u])).astype(BF16), st.astype(BF16))
        cl = cb[u][last:last + 1, :]
        kdec = (k[u] * jnp.exp2(cl - cb[u])).astype(BF16)
        s_scr[u] = st * jnp.exp2(cl) + _dot_tn(v[u], kdec)
        if readout:
            os_ = of_ref[g, :, vcs[h]] + o
            on = os_ * lax.rsqrt(jnp.mean(os_ * os_, axis=-1, keepdims=True) + EPS) * g_ref[:, vcs[h]]
            out_ref[g, :, vcs[h]] = (on * _silu(cg_ref[g, :, vcs[h]].astype(F32))).astype(out_ref.dtype)
        else:
            out_ref[g, :, vcs[h]] = o.astype(out_ref.dtype)


def _hgrn_dir(pcq, pcf, lbf, oml, e_mat, cols, t_lat, reverse, of=None, norm_g=None, gb=4):
    b, tt, _ = pcq.shape
    kw, L = C_HEADS * C_KEY_DIM, C_CHUNK
    vw = cols["c_g"] - cols["c_i"]
    dv = vw // C_HEADS
    nch, nlat = tt // L, t_lat // L
    if reverse:
        chunk = lambda i: nch - 1 - i
    else:
        chunk = lambda i: lax.rem(i + nlat, nch)
    gb = gb if b % gb == 0 else 1
    tok = lambda c0, w: pl.BlockSpec((gb, L, w), lambda bi, i: (bi, chunk(i), c0 // w))
    const2 = lambda x: pl.BlockSpec(x.shape, lambda bi, i: (0, 0))
    f_name = "c_f_bwd" if reverse else "c_f_fwd"
    in_specs = [tok(cols["c_q"], kw), tok(cols[f_name], kw), tok(cols["c_i"], vw),
                const2(lbf), const2(oml), const2(e_mat)]
    args = [pcq, pcf, pcq, lbf, oml, e_mat]
    readout = of is not None
    if readout:
        in_specs += [tok(0, vw), tok(cols["c_g"], vw), const2(norm_g)]
        args += [of, pcq, norm_g]
    kern = functools.partial(_hgrn_kernel, reverse=reverse, readout=readout, dv=dv, gb=gb)
    return pl.pallas_call(
        kern,
        grid=(b // gb, nch),
        in_specs=in_specs,
        out_specs=tok(0, vw),
        out_shape=jax.ShapeDtypeStruct((b, tt, vw), BF16),
        scratch_shapes=[pltpu.VMEM((gb * C_HEADS, dv, C_KEY_DIM), F32),
                        pltpu.VMEM((2 * gb * C_HEADS, L, C_KEY_DIM), F32)],
        compiler_params=_params("parallel", "arbitrary"),
        name="hgrn_bwd_readout" if readout else "hgrn_fwd",
    )(*args)


def _out_kernel(ya_ref, yb_ref, yc_ref, w_ref, h_ref, mod_ref, *rest, final, wa, wb):
    d = h_ref.shape[-1]
    w = w_ref
    y = (_dot(ya_ref[0], w[0:wa, :]) + _dot(yb_ref[0], w[wa:wa + wb, :]) + _dot(yc_ref[0], w[wa + wb:, :]))
    h_new = h_ref[0] + mod_ref[0][:, 2 * d:3 * d] * y
    if final:
        g_ref, out_ref = rest
        out_ref[0] = h_new * lax.rsqrt(jnp.mean(h_new * h_new, axis=-1, keepdims=True) + EPS) * g_ref[...]
    else:
        g_ref, modn_ref, h_out_ref, n_ref = rest
        h_out_ref[0] = h_new
        n_ref[0] = _modulated_norm(h_new, g_ref[...], modn_ref[0], d).astype(n_ref.dtype)


def _out_proj(ya, yb, yc, w_out, h, mod, g_next, mod_next, t_lat, final, tm=256):
    b, tt, d = h.shape
    wa, wb, wc = ya.shape[-1], yb.shape[-1], yc.shape[-1]
    tok = lambda w: pl.BlockSpec((1, tm, w), lambda i, j: (i, j, 0))
    mod_spec = pl.BlockSpec((1, 1, 3 * d), _mod_index(t_lat // tm))
    in_specs = [tok(wa), tok(wb), tok(wc), pl.BlockSpec(w_out.shape, lambda i, j: (0, 0)), tok(d), mod_spec,
                pl.BlockSpec((1, d), lambda i, j: (0, 0))]
    args = [ya, yb, yc, w_out, h, mod, g_next]
    kern = functools.partial(_out_kernel, final=final, wa=wa, wb=wb)
    if final:
        return pl.pallas_call(
            kern, grid=(b, t_lat // tm), in_specs=in_specs, out_specs=tok(d),
            out_shape=jax.ShapeDtypeStruct((b, t_lat, d), F32),
            compiler_params=_params("parallel", "parallel"), name="out_proj_final",
        )(*args)
    return pl.pallas_call(
        kern, grid=(b, tt // tm), in_specs=in_specs + [mod_spec], out_specs=(tok(d), tok(d)),
        out_shape=(jax.ShapeDtypeStruct((b, tt, d), F32), jax.ShapeDtypeStruct((b, tt, d), BF16)),
        compiler_params=_params("parallel", "parallel"), name="out_proj",
    )(*args, mod_next)


def _packed_layout(d):
    a, bw, c, kq = d // 4, d // 2, d // 4, C_HEADS * C_KEY_DIM
    ref_order = (("a_u", a), ("a_v", a), ("a_z", a), ("b_q", bw), ("b_k", bw), ("b_v", bw), ("b_o", bw),
                 ("b_z", bw), ("gates", 4 * B_HEADS), ("c_q", kq), ("c_f_fwd", kq), ("c_f_bwd", kq),
                 ("c_i", c), ("c_g", c))
    src, start = {}, 0
    for name, w in ref_order:
        src[name] = (start, w)
        start += w
    groups = {"ab": ("b_q", "b_k", "b_v", "b_o", "b_z", "a_u", "a_v", "a_z"),
              "cq": ("c_q", "c_i", "c_g"), "cf": ("c_f_fwd", "c_f_bwd")}
    cols = {}
    for names in groups.values():
        pos = 0
        for name in names:
            cols[name] = pos
            pos += src[name][1]
    return src, groups, cols


def _pack_cols(w, bias, src, names):
    pick = lambda a: jnp.concatenate([a[..., src[k][0]:src[k][0] + src[k][1]] for k in names], axis=-1)
    return pick(w).astype(BF16), pick(bias)[None, :]


def _grid_transpose(x, rows, width):
    b, _, f = x.shape
    return x.reshape(b, rows, width, f).swapaxes(1, 2).reshape(b, rows * width, f)


def kernel(x, c, ctx, c_ctx, w_ada, b_ada, norm_g, w_in, b_in, w_spatial, b_spatial, conv_qk, mlstm_norm,
           hgrn_lb_logits, hgrn_norm, w_out, final_norm):
    b, t_lat, d = x.shape
    t_ctx = ctx.shape[1]
    tt = t_lat + t_ctx
    depth = w_ada.shape[0]
    src, groups, cols = _packed_layout(d)
    a_width, b_width = d // 4, d // 2
    rows = t_lat // GRID_W

    r_pad = -(-(b + 1) // 8) * 8
    cond = jnp.concatenate([c, c_ctx[None, :], jnp.zeros((r_pad - b - 1, d), F32)], axis=0)
    mod_all = _ada_mod(cond, w_ada, b_ada)
    mods = [jnp.stack([mod_all[l, :b], jnp.broadcast_to(mod_all[l, b], (b, 3 * d))], axis=1).reshape(2 * b, 1, 3 * d)
            for l in range(depth)]

    lbf, oml = _lower_bounds(hgrn_lb_logits.astype(F32))

    e_rows = lax.broadcasted_iota(jnp.int32, (C_SUB * C_KEY_DIM, C_CHUNK), 0) // C_KEY_DIM
    e_cols = lax.broadcasted_iota(jnp.int32, (C_SUB * C_KEY_DIM, C_CHUNK), 1) % C_SUB
    e_mat = (e_rows == e_cols).astype(BF16)

    h = jnp.concatenate([x, ctx], axis=1)
    n = _norm_mod(h, norm_g[0:1], mods[0], t_lat)
    out = None
    for l in range(depth):
        last = l == depth - 1
        w_ab, b_ab = _pack_cols(w_in[l], b_in[l], src, groups["ab"])
        w_cq, b_cq = _pack_cols(w_in[l], b_in[l], src, groups["cq"])
        w_cf, b_cf = _pack_cols(w_in[l], b_in[l], src, groups["cf"])
        g0, gw = src["gates"]
        w_gate_t = w_in[l][:, g0:g0 + gw].T.astype(BF16)
        b_gate = b_in[l][g0:g0 + gw][:, None]

        n_cm = jnp.concatenate([_grid_transpose(n[:, :t_lat], rows, GRID_W), n[:, t_lat:]], axis=1)
        n2d, n_cm2d = n.reshape(b * tt, d), n_cm.reshape(b * tt, d)
        p2d = _in_proj(n2d, w_ab, b_ab, 1024, 1664, "in_proj_ab", BF16)
        pcq = _in_proj(n_cm2d, w_cq, b_cq, 1024, 1536, "in_proj_cq", BF16).reshape(b, tt, -1)
        pcf = _in_proj(n_cm2d, w_cf, b_cf, 1024, 1024, "in_proj_cf").reshape(b, tt, -1)
        gates_row = _in_proj_gates_rows(n, w_gate_t, b_gate)
        p3d = p2d.reshape(b, tt, -1)

        ya = _chunk_mlp(p2d, w_spatial[l].astype(BF16), b_spatial[l].T, cols["a_u"], a_width).reshape(b, tt, a_width)

        q = _conv_silu(p3d, conv_qk[l], t_lat, cols["b_q"], 0, b_width, 1.0, transpose=False)
        k_t = _conv_silu(p3d, conv_qk[l], t_lat, cols["b_k"], b_width, b_width,
                         (b_width // B_HEADS) ** -0.5, transpose=True)
        hf = _mlstm_dir(q, k_t, p3d, gates_row, t_lat, cols, reverse=False)
        yb = _mlstm_dir(q, k_t, p3d, gates_row, t_lat, cols, reverse=True, hf=hf, norm_g=mlstm_norm[l:l + 1])

        lbf_l, oml_l = lbf[l:l + 1], oml[l:l + 1]
        of = _hgrn_dir(pcq, pcf, lbf_l, oml_l, e_mat, cols, t_lat, False)
        yc_cm = _hgrn_dir(pcq, pcf, lbf_l, oml_l, e_mat, cols, t_lat, True, of=of, norm_g=hgrn_norm[l:l + 1])
        yc = jnp.concatenate([_grid_transpose(yc_cm[:, :t_lat], GRID_W, rows), yc_cm[:, t_lat:]], axis=1)

        w_o = w_out[l].astype(BF16)
        if last:
            out = _out_proj(ya, yb, yc, w_o, h, mods[l], final_norm[None, :], None, t_lat, final=True)
        else:
            h, n = _out_proj(ya, yb, yc, w_o, h, mods[l], norm_g[l + 1:l + 2], mods[l + 1], t_lat, final=False)
    return out
```

```python
import functools

import jax
import jax.numpy as jnp
from jax import lax
from jax.experimental import pallas as pl
from jax.experimental.pallas import tpu as pltpu

EPS = 1e-6
NEG_BIG = -1e30
LB_FLOOR = 1e-30
GRID_W = 64
CONV_W = 3

A_GROUPS = 4
A_CHUNK = 128
B_HEADS = 4
B_CHUNK = 256
C_HEADS = 4
C_KEY_DIM = 128
C_CHUNK = 64
C_SUB = 8
GATE_PAD = 128
MLSTM_AUG = 128

V7X_VMEM_LIMIT = 56 * 1024 * 1024

F32 = jnp.float32
BF16 = jnp.bfloat16


def _params(*sem):
    return pltpu.CompilerParams(dimension_semantics=sem, vmem_limit_bytes=V7X_VMEM_LIMIT)


def _sigmoid(x):
    return 1.0 / (1.0 + jnp.exp(-x))


def _silu(x):
    return x * _sigmoid(x)


def _log_sigmoid(x):
    return jnp.minimum(x, 0.0) - jnp.log1p(jnp.exp(-jnp.abs(x)))


def _dot(a, b):
    return jnp.dot(a, b, preferred_element_type=F32)


def _dot_nt(a, b):
    return lax.dot_general(a, b, (((1,), (1,)), ((), ())), preferred_element_type=F32)


def _dot_tn(a, b):
    return lax.dot_general(a, b, (((0,), (0,)), ((), ())), preferred_element_type=F32)


def _lb_kernel(x_ref, lbf_ref, oml_ref):
    x = x_ref[...]
    depth = x.shape[0]
    e = jnp.exp(x - jnp.max(x, axis=0, keepdims=True))
    p = e / jnp.sum(e, axis=0, keepdims=True)
    rows = lax.broadcasted_iota(jnp.int32, x.shape, 0)
    lb = jnp.zeros_like(x)
    for j in range(1, depth):
        lb = lb + jnp.where(rows >= j, p[j:j + 1, :], 0.0)
    lbf_ref[...] = jnp.maximum(lb, LB_FLOOR)
    oml_ref[...] = 1.0 - lb


def _lower_bounds(logits):
    shp = jax.ShapeDtypeStruct(logits.shape, F32)
    return pl.pallas_call(_lb_kernel, out_shape=(shp, shp), name="hgrn_lower_bounds")(logits)


def _ada_kernel(c_ref, w_ref, b_ref, o_ref):
    s = _silu(c_ref[...]).astype(BF16)
    o_ref[0] = _dot(s, w_ref[0].astype(BF16)) + b_ref[0]


def _ada_mod(cond, w_ada, b_ada, tn=768):
    depth, d, n3 = w_ada.shape
    r = cond.shape[0]
    return pl.pallas_call(
        _ada_kernel,
        grid=(depth, n3 // tn),
        in_specs=[pl.BlockSpec((r, d), lambda l, j: (0, 0)),
                  pl.BlockSpec((1, d, tn), lambda l, j: (l, 0, j)),
                  pl.BlockSpec((1, 1, tn), lambda l, j: (l, 0, j))],
        out_specs=pl.BlockSpec((1, r, tn), lambda l, j: (l, 0, j)),
        out_shape=jax.ShapeDtypeStruct((depth, r, n3), F32),
        compiler_params=_params("parallel", "parallel"),
        name="adaln_mod",
    )(cond, w_ada, b_ada.reshape(depth, 1, n3))


def _modulated_norm(h, g, mod, d):
    y = h * lax.rsqrt(jnp.mean(h * h, axis=-1, keepdims=True) + EPS) * g
    return y * (1.0 + mod[:, d:2 * d]) + mod[:, 0:d]


def _norm_kernel(x_ref, ctx_ref, g_ref, mod_ref, h_ref, n_ref, *, tiles_lat):
    d = x_ref.shape[-1]

    def emit(src_ref):
        h = src_ref[0]
        h_ref[0] = h
        n_ref[0] = _modulated_norm(h, g_ref[...], mod_ref[0], d).astype(n_ref.dtype)

    is_lat = pl.program_id(1) < tiles_lat
    pl.when(is_lat)(lambda: emit(x_ref))
    pl.when(jnp.logical_not(is_lat))(lambda: emit(ctx_ref))


def _mod_index(tiles_lat):
    return lambda b, j: (2 * b + jnp.where(j >= tiles_lat, 1, 0), 0, 0)


def _join_norm_mod(x, ctx, g, mod, tm=256):
    b, t_lat, d = x.shape
    tt = t_lat + ctx.shape[1]
    tl = t_lat // tm
    tok = pl.BlockSpec((1, tm, d), lambda i, j: (i, j, 0))
    return pl.pallas_call(
        functools.partial(_norm_kernel, tiles_lat=tl),
        grid=(b, tt // tm),
        in_specs=[pl.BlockSpec((1, tm, d), lambda i, j: (i, jnp.minimum(j, tl - 1), 0)),
                  pl.BlockSpec((1, tm, d), lambda i, j: (i, jnp.maximum(j - tl, 0), 0)),
                  pl.BlockSpec((1, d), lambda i, j: (0, 0)),
                  pl.BlockSpec((1, 1, 3 * d), _mod_index(tl))],
        out_specs=(tok, tok),
        out_shape=(jax.ShapeDtypeStruct((b, tt, d), F32), jax.ShapeDtypeStruct((b, tt, d), BF16)),
        compiler_params=_params("parallel", "arbitrary"),
        name="join_norm_mod",
    )(x, ctx, g, mod)


def _matmul_bias_kernel(x_ref, w_ref, b_ref, o_ref):
    o_ref[...] = (_dot(x_ref[...], w_ref[...]) + b_ref[...]).astype(o_ref.dtype)


def _tile(m, pref, unit=128):
    t = min(pref, m) // unit * unit
    while m % t:
        t -= unit
    return t


def _in_proj(n2d, w, bias, tm, tn, name, out_dtype=F32):
    m, d = n2d.shape
    n = w.shape[1]
    tm = _tile(m, tm)
    return pl.pallas_call(
        _matmul_bias_kernel,
        grid=(n // tn, m // tm),
        in_specs=[pl.BlockSpec((tm, d), lambda j, i: (i, 0)),
                  pl.BlockSpec((d, tn), lambda j, i: (0, j)),
                  pl.BlockSpec((1, tn), lambda j, i: (0, j))],
        out_specs=pl.BlockSpec((tm, tn), lambda j, i: (i, j)),
        out_shape=jax.ShapeDtypeStruct((m, n), out_dtype),
        compiler_params=_params("parallel", "parallel"),
        name=name,
    )(n2d, w, bias)


def _gates_kernel(x_ref, w_ref, b_ref, o_ref):
    o_ref[0] = _dot_nt(w_ref[...], x_ref[0]) + b_ref[...]


def _in_proj_gates_rows(n, w_t, bias_col):
    b, tt, d = n.shape
    ng = w_t.shape[0]
    tm = _tile(tt, 2560)
    return pl.pallas_call(
        _gates_kernel,
        grid=(b, tt // tm),
        in_specs=[pl.BlockSpec((1, tm, d), lambda i, j: (i, j, 0)),
                  pl.BlockSpec((ng, d), lambda i, j: (0, 0)),
                  pl.BlockSpec((ng, 1), lambda i, j: (0, 0))],
        out_specs=pl.BlockSpec((1, ng, tm), lambda i, j: (i, 0, j)),
        out_shape=jax.ShapeDtypeStruct((b, ng, tt), F32),
        compiler_params=_params("parallel", "parallel"),
        name="in_proj_gates",
    )(n, w_t, bias_col)


def _chunk_mlp_kernel(u_ref, v_ref, z_ref, ws_ref, bs_ref, y_ref):
    tm, width = v_ref.shape
    gd = width // A_GROUPS
    for c in range(tm // A_CHUNK):
        rows = slice(c * A_CHUNK, (c + 1) * A_CHUNK)
        for g in range(A_GROUPS):
            cols = slice(g * gd, (g + 1) * gd)
            v = v_ref[rows, cols].astype(F32)
            mu = jnp.mean(v, axis=-1, keepdims=True)
            vc = v - mu
            var = jnp.mean(vc * vc, axis=-1, keepdims=True)
            vn = (vc * lax.rsqrt(var + EPS)).astype(BF16)
            mixed = _dot(ws_ref[g], vn) + bs_ref[:, g:g + 1]
            gate = _silu(z_ref[rows, cols].astype(F32))
            y_ref[rows, cols] = (u_ref[rows, cols].astype(F32) * mixed * gate).astype(y_ref.dtype)


def _chunk_mlp(p2d, ws, bs_t, col_u, width, tm=1024):
    m = p2d.shape[0]
    tm = _tile(m, tm)
    cb = col_u // width
    spec = lambda k: pl.BlockSpec((tm, width), lambda i, k=k: (i, cb + k))
    return pl.pallas_call(
        _chunk_mlp_kernel,
        grid=(m // tm,),
        in_specs=[spec(0), spec(1), spec(2),
                  pl.BlockSpec(ws.shape, lambda i: (0, 0, 0)),
                  pl.BlockSpec(bs_t.shape, lambda i: (0, 0))],
        out_specs=pl.BlockSpec((tm, width), lambda i: (i, 0)),
        out_shape=jax.ShapeDtypeStruct((m, width), BF16),
        compiler_params=_params("parallel"),
        name="chunk_mlp",
    )(p2d, p2d, p2d, ws, bs_t)


def _conv_kernel(x_ref, w_ref, o_ref, *, t_lat, scale, transpose):
    x = x_ref[0].astype(F32)
    tt = x.shape[0]
    rows = lax.broadcasted_iota(jnp.int32, x.shape, 0)
    prev = jnp.where((rows == 0) | (rows == t_lat), 0.0, pltpu.roll(x, 1, axis=0))
    nxt = jnp.where((rows == t_lat - 1) | (rows == tt - 1), 0.0, pltpu.roll(x, tt - 1, axis=0))
    w = w_ref[...]
    y = _silu(w[0:1] * prev + w[1:2] * x + w[2:3] * nxt) * scale
    o_ref[0] = (y.T if transpose else y).astype(o_ref.dtype)


def _conv_silu(p3d, conv_w, t_lat, col0, wcol0, width, scale, transpose, tc=256):
    b, tt, _ = p3d.shape
    kern = functools.partial(_conv_kernel, t_lat=t_lat, scale=scale, transpose=transpose)
    if transpose:
        out_spec = pl.BlockSpec((1, tc, tt), lambda i, j: (i, j, 0))
        out_shape = jax.ShapeDtypeStruct((b, width, tt), BF16)
    else:
        out_spec = pl.BlockSpec((1, tt, tc), lambda i, j: (i, 0, j))
        out_shape = jax.ShapeDtypeStruct((b, tt, width), BF16)
    return pl.pallas_call(
        kern,
        grid=(b, width // tc),
        in_specs=[pl.BlockSpec((1, tt, tc), lambda i, j: (i, 0, col0 // tc + j)),
                  pl.BlockSpec((CONV_W, tc), lambda i, j: (0, wcol0 // tc + j))],
        out_specs=out_spec,
        out_shape=out_shape,
        compiler_params=_params("parallel", "parallel"),
        name="conv_k_t" if transpose else "conv_q",
    )(p3d, conv_w)


def _mlstm_kernel(*refs, reverse, readout, hd, gb):
    units = [(g, h) for g in range(gb) for h in range(B_HEADS)]
    nu = len(units)
    refs, c_scr, m_scr = refs[:-2 * nu], refs[-2 * nu:-nu], refs[-nu:]
    if readout:
        q_ref, kt_ref, v_ref, gr_ref, hf_ref, o_ref, z_ref, g_ref, out_ref = refs
    else:
        q_ref, kt_ref, v_ref, gr_ref, out_ref = refs

    @pl.when(pl.program_id(1) == 0)
    def _():
        for scr in c_scr + m_scr:
            scr[...] = jnp.zeros_like(scr)

    L = B_CHUNK
    row = lax.broadcasted_iota(jnp.int32, (L, L), 0)
    col = lax.broadcasted_iota(jnp.int32, (L, L), 1)
    seen = (col >= row) if reverse else (col <= row)
    tri_t = jnp.where((row >= col) if reverse else (row <= col), 1.0, 0.0).astype(BF16)
    last = 0 if reverse else L - 1
    gi, gf = (2 * B_HEADS, 3 * B_HEADS) if reverse else (0, B_HEADS)
    cols = [slice(h * hd, (h + 1) * hd) for h in range(B_HEADS)]
    ones_blk = jnp.ones((L, MLSTM_AUG), BF16)
    lane_tile = lambda x, width: jnp.concatenate([x] * (width // MLSTM_AUG), axis=1)
    gates = lambda lo_row: jnp.concatenate([gr_ref[g, lo_row:lo_row + B_HEADS, :] for g in range(gb)], axis=0)

    i_rows = gates(gi)
    f_rows = _log_sigmoid(gates(gf))
    hi = f_rows.astype(BF16).astype(F32)
    r1 = f_rows - hi
    mid = r1.astype(BF16).astype(F32)
    lo = (r1 - mid).astype(BF16).astype(F32)
    terms = jnp.concatenate([hi, mid, lo, jnp.zeros_like(hi)], axis=0).astype(BF16)
    sums = _dot(terms, tri_t)
    cb_rows = sums[0:nu] + sums[nu:2 * nu] + sums[2 * nu:3 * nu]
    tri = jnp.where(seen, 1.0, 0.0).astype(BF16)
    rep = lambda x, u: jnp.broadcast_to(x[u:u + 1], (MLSTM_AUG, L))
    cb_cols = []
    for u in range(nu):
        c3 = _dot_nt(tri, jnp.concatenate([rep(hi, u), rep(mid, u), rep(lo, u)], axis=0).astype(BF16))
        cb_cols.append(c3[:, 0:MLSTM_AUG] + c3[:, MLSTM_AUG:2 * MLSTM_AUG] + c3[:, 2 * MLSTM_AUG:])

    s_bf, e_inter, emt, ws, ec, m_new = [], [], [], [], [], []
    for u, (g, h) in enumerate(units):
        i_row, cb_row = i_rows[u:u + 1], cb_rows[u:u + 1]
        dmat = jnp.where(seen, lane_tile(cb_cols[u], L) - cb_row + i_row, NEG_BIG)
        m = m_scr[u][0:1, 0:1]
        inter = cb_cols[u] + m
        mt = jnp.maximum(inter, jnp.max(dmat, axis=1, keepdims=True))
        decay = jnp.exp(dmat - lane_tile(mt, L))
        s_bf.append((_dot(q_ref[g, :, cols[h]], kt_ref[g, cols[h], :]) * decay).astype(BF16))
        e_inter.append(jnp.exp(inter - mt))
        emt.append(jnp.exp(-mt))
        cl = cb_row[:, last:last + 1]
        w_log = cl - cb_row + i_row
        m_new.append(jnp.maximum(cl + m, jnp.max(w_log, axis=1, keepdims=True)))
        ec.append(jnp.exp(cl + m - m_new[u]))
        ws.append(jnp.exp(w_log - m_new[u]))

    sv, qc, upd = [], [], []
    for u, (g, h) in enumerate(units):
        v_aug = jnp.concatenate([v_ref[g, :, cols[h]].astype(BF16), ones_blk], axis=1)
        sv.append(_dot(s_bf[u], v_aug))
        qc.append(_dot(q_ref[g, :, cols[h]], c_scr[u][...].astype(BF16)))
        kw_t = (kt_ref[g, cols[h], :].astype(F32) * ws[u]).astype(BF16)
        upd.append(_dot(kw_t, v_aug))
    for u, (g, h) in enumerate(units):
        tot = sv[u] + lane_tile(e_inter[u], hd + MLSTM_AUG) * qc[u]
        inv = 1.0 / jnp.maximum(jnp.abs(tot[:, hd:hd + MLSTM_AUG]), emt[u])
        hc = jnp.concatenate([tot[:, c0:c0 + MLSTM_AUG] * inv for c0 in range(0, hd, MLSTM_AUG)], axis=1)
        c_scr[u][...] = ec[u] * c_scr[u][...] + upd[u]
        m_scr[u][...] = jnp.broadcast_to(m_new[u], m_scr[u].shape)
        if readout:
            hs = hf_ref[g, :, cols[h]] + hc
            hn = hs * lax.rsqrt(jnp.mean(hs * hs, axis=-1, keepdims=True) + EPS) * g_ref[:, cols[h]]
            y = hn * _sigmoid(o_ref[g, :, cols[h]].astype(F32)) * _silu(z_ref[g, :, cols[h]].astype(F32))
            out_ref[g, :, cols[h]] = y.astype(out_ref.dtype)
        else:
            out_ref[g, :, cols[h]] = hc.astype(out_ref.dtype)


def _mlstm_dir(q, k_t, p3d, gates_row, t_lat, cols, reverse, hf=None, norm_g=None, gb=2):
    b, tt, width = q.shape
    gb = max(g for g in range(1, gb + 1) if b % g == 0)
    hd = width // B_HEADS
    L = B_CHUNK
    nch, nlat = tt // L, t_lat // L
    if reverse:
        chunk = lambda i: nch - 1 - i
    else:
        chunk = lambda i: lax.rem(i + nlat, nch)
    wb = lambda name: cols[name] // width
    tok = lambda cb: pl.BlockSpec((gb, L, width), lambda bi, i, cb=cb: (bi, chunk(i), cb))
    in_specs = [tok(0), pl.BlockSpec((gb, width, L), lambda bi, i: (bi, 0, chunk(i))), tok(wb("b_v")),
                pl.BlockSpec((gb, gates_row.shape[1], L), lambda bi, i: (bi, 0, chunk(i)))]
    args = [q, k_t, p3d, gates_row]
    readout = hf is not None
    if readout:
        in_specs += [tok(0), tok(wb("b_o")), tok(wb("b_z")), pl.BlockSpec((1, width), lambda bi, i: (0, 0))]
        args += [hf, p3d, p3d, norm_g]
    kern = functools.partial(_mlstm_kernel, reverse=reverse, readout=readout, hd=hd, gb=gb)
    nu = gb * B_HEADS
    return pl.pallas_call(
        kern,
        grid=(b // gb, nch),
        in_specs=in_specs,
        out_specs=tok(0),
        out_shape=jax.ShapeDtypeStruct((b, tt, width), BF16),
        scratch_shapes=[pltpu.VMEM((hd, hd + MLSTM_AUG), F32)] * nu + [pltpu.VMEM((8, 128), F32)] * nu,
        compiler_params=_params("parallel", "arbitrary"),
        name="mlstm_bwd_readout" if readout else "mlstm_fwd",
    )(*args)


def _hgrn_kernel(*refs, reverse, readout, dv, gb):
    if readout:
        q_ref, z_ref, v_ref, lbf_ref, oml_ref, e_ref, of_ref, cg_ref, g_ref, out_ref, s_scr, row_scr = refs
    else:
        q_ref, z_ref, v_ref, lbf_ref, oml_ref, e_ref, out_ref, s_scr, row_scr = refs

    @pl.when(pl.program_id(1) == 0)
    def _():
        s_scr[...] = jnp.zeros_like(s_scr)

    L, dk, nb = C_CHUNK, C_KEY_DIM, C_CHUNK // C_SUB
    row = lax.broadcasted_iota(jnp.int32, (L, L), 0)
    col = lax.broadcasted_iota(jnp.int32, (L, L), 1)
    rb, cbk = row // C_SUB, col // C_SUB
    if reverse:
        seen, blk_before, last = col >= row, cbk > rb, 0
    else:
        seen, blk_before, last = col <= row, cbk < rb, L - 1
    sum_mat = jnp.concatenate([jnp.where(seen, 1.0, 0.0), jnp.where(blk_before, 1.0, 0.0)], axis=0).astype(BF16)
    same_blk = rb == cbk
    sub = lax.broadcasted_iota(jnp.int32, (L, dk), 0) % C_SUB
    pair_ok = [(sub <= j) if reverse else (sub >= j) for j in range(C_SUB)]

    def bcast_sub(ref, j):
        return jnp.concatenate([jnp.broadcast_to(ref[i * C_SUB + j:i * C_SUB + j + 1, :], (C_SUB, dk))
                                for i in range(nb)], axis=0)

    units = [(g, h) for g in range(gb) for h in range(C_HEADS)]
    kcs = [slice(h * dk, (h + 1) * dk) for h in range(C_HEADS)]
    vcs = [slice(h * dv, (h + 1) * dv) for h in range(C_HEADS)]

    q, k, v, c3 = [], [], [], []
    for u, (g, h) in enumerate(units):
        q.append(_silu(q_ref[g, :, kcs[h]].astype(F32)))
        z = z_ref[g, :, kcs[h]]
        v.append(v_ref[g, :, vcs[h]].astype(BF16))
        a = jnp.exp(-jnp.abs(z))
        r = 1.0 / (1.0 + a)
        pos = z >= 0.0
        oml = oml_ref[:, kcs[h]]
        f = lbf_ref[:, kcs[h]] + oml * jnp.where(pos, r, a * r)
        k.append(oml * jnp.where(pos, a * r, r))
        lf = jnp.log2(f)
        hi = lf.astype(BF16)
        r1 = lf - hi.astype(F32)
        mid = r1.astype(BF16)
        lo = (r1 - mid.astype(F32)).astype(BF16)
        c3.append(_dot(sum_mat, jnp.concatenate([hi, mid, lo], axis=1)))

    cb, a_off, a_diag = [], [], []
    for u, (g, h) in enumerate(units):
        c1 = c3[u][:, 0:dk] + c3[u][:, dk:2 * dk] + c3[u][:, 2 * dk:3 * dk]
        cbh, entry = c1[0:L], c1[L:2 * L]
        cb.append(cbh)
        qd = (q[u] * jnp.exp2(cbh - entry)).astype(BF16)
        parts = []
        for i in range(nb):
            lo_r, hi_r = ((i + 1) * C_SUB, L) if reverse else (0, i * C_SUB)
            if hi_r == lo_r:
                parts.append(jnp.zeros((C_SUB, L), F32))
                continue
            ent = entry[i * C_SUB:i * C_SUB + 1, :]
            kd = (k[u][lo_r:hi_r] * jnp.exp2(ent - cbh[lo_r:hi_r])).astype(BF16)
            pad = [jnp.zeros((n, dk), BF16) for n in (lo_r, L - hi_r)]
            kd = jnp.concatenate([p for p in (pad[0], kd, pad[1]) if p.shape[0]], axis=0)
            parts.append(_dot_nt(qd[i * C_SUB:(i + 1) * C_SUB], kd))
        a_off.append(jnp.concatenate(parts, axis=0))
        qk_parts = []
        cb_rows, k_rows = row_scr.at[2 * u], row_scr.at[2 * u + 1]
        cb_rows[...] = cbh
        k_rows[...] = k[u]
        for j in range(C_SUB):
            dec = jnp.exp2(jnp.where(pair_ok[j], cbh - bcast_sub(cb_rows, j), NEG_BIG))
            qk_parts.append((q[u] * bcast_sub(k_rows, j) * dec).astype(BF16))
        a_diag.append(_dot(jnp.concatenate(qk_parts, axis=1), e_ref[...]))

    for u, (g, h) in enumerate(units):
        scores = a_off[u] + jnp.where(same_blk, a_diag[u], 0.0)
        st = s_scr[u]
        o = _dot(scores.astype(BF16), v[u]) + _dot_nt((q[u] * jnp.exp2(cb[u])).astype(BF16), st.astype(BF16))
        cl = cb[u][last:last + 1, :]
        kdec = (k[u] * jnp.exp2(cl - cb[u])).astype(BF16)
        s_scr[u] = st * jnp.exp2(cl) + _dot_tn(v[u], kdec)
        if readout:
            os_ = of_ref[g, :, vcs[h]] + o
            on = os_ * lax.rsqrt(jnp.mean(os_ * os_, axis=-1, keepdims=True) + EPS) * g_ref[:, vcs[h]]
            out_ref[g, :, vcs[h]] = (on * _silu(cg_ref[g, :, vcs[h]].astype(F32))).astype(out_ref.dtype)
        else:
            out_ref[g, :, vcs[h]] = o.astype(out_ref.dtype)


def _hgrn_dir(pcq, pcf, lbf, oml, e_mat, cols, t_lat, reverse, of=None, norm_g=None, gb=8):
    b, tt, _ = pcq.shape
    kw, L = C_HEADS * C_KEY_DIM, C_CHUNK
    vw = cols["c_g"] - cols["c_i"]
    dv = vw // C_HEADS
    nch, nlat = tt // L, t_lat // L
    if reverse:
        chunk = lambda i: nch - 1 - i
    else:
        chunk = lambda i: lax.rem(i + nlat, nch)
    gb = max(g for g in range(1, gb + 1) if b % g == 0)
    tok = lambda c0, w: pl.BlockSpec((gb, L, w), lambda bi, i: (bi, chunk(i), c0 // w))
    const2 = lambda x: pl.BlockSpec(x.shape, lambda bi, i: (0, 0))
    f_name = "c_f_bwd" if reverse else "c_f_fwd"
    in_specs = [tok(cols["c_q"], kw), tok(cols[f_name], kw), tok(cols["c_i"], vw),
                const2(lbf), const2(oml), const2(e_mat)]
    args = [pcq, pcf, pcq, lbf, oml, e_mat]
    readout = of is not None
    if readout:
        in_specs += [tok(0, vw), tok(cols["c_g"], vw), const2(norm_g)]
        args += [of, pcq, norm_g]
    kern = functools.partial(_hgrn_kernel, reverse=reverse, readout=readout, dv=dv, gb=gb)
    return pl.pallas_call(
        kern,
        grid=(b // gb, nch),
        in_specs=in_specs,
        out_specs=tok(0, vw),
        out_shape=jax.ShapeDtypeStruct((b, tt, vw), BF16),
        scratch_shapes=[pltpu.VMEM((gb * C_HEADS, dv, C_KEY_DIM), F32),
                        pltpu.VMEM((2 * gb * C_HEADS, L, C_KEY_DIM), F32)],
        compiler_params=_params("parallel", "arbitrary"),
        name="hgrn_bwd_readout" if readout else "hgrn_fwd",
    )(*args)


def _out_kernel(ya_ref, yb_ref, yc_ref, w_ref, h_ref, mod_ref, *rest, final, wa, wb):
    d = h_ref.shape[-1]
    w = w_ref
    y = (_dot(ya_ref[0], w[0:wa, :]) + _dot(yb_ref[0], w[wa:wa + wb, :]) + _dot(yc_ref[0], w[wa + wb:, :]))
    h_new = h_ref[0] + mod_ref[0][:, 2 * d:3 * d] * y
    if final:
        g_ref, out_ref = rest
        out_ref[0] = h_new * lax.rsqrt(jnp.mean(h_new * h_new, axis=-1, keepdims=True) + EPS) * g_ref[...]
    else:
        g_ref, modn_ref, h_out_ref, n_ref = rest
        h_out_ref[0] = h_new
        n_ref[0] = _modulated_norm(h_new, g_ref[...], modn_ref[0], d).astype(n_ref.dtype)


def _out_proj(ya, yb, yc, w_out, h, mod, g_next, mod_next, t_lat, final, tm=256):
    b, tt, d = h.shape
    wa, wb, wc = ya.shape[-1], yb.shape[-1], yc.shape[-1]
    tok = lambda w: pl.BlockSpec((1, tm, w), lambda i, j: (i, j, 0))
    mod_spec = pl.BlockSpec((1, 1, 3 * d), _mod_index(t_lat // tm))
    in_specs = [tok(wa), tok(wb), tok(wc), pl.BlockSpec(w_out.shape, lambda i, j: (0, 0)), tok(d), mod_spec,
                pl.BlockSpec((1, d), lambda i, j: (0, 0))]
    args = [ya, yb, yc, w_out, h, mod, g_next]
    kern = functools.partial(_out_kernel, final=final, wa=wa, wb=wb)
    if final:
        return pl.pallas_call(
            kern, grid=(b, t_lat // tm), in_specs=in_specs, out_specs=tok(d),
            out_shape=jax.ShapeDtypeStruct((b, t_lat, d), F32),
            compiler_params=_params("parallel", "parallel"), name="out_proj_final",
        )(*args)
    return pl.pallas_call(
        kern, grid=(b, tt // tm), in_specs=in_specs + [mod_spec], out_specs=(tok(d), tok(d)),
        out_shape=(jax.ShapeDtypeStruct((b, tt, d), F32), jax.ShapeDtypeStruct((b, tt, d), BF16)),
        compiler_params=_params("parallel", "parallel"), name="out_proj",
    )(*args, mod_next)


def _packed_layout(d):
    a, bw, c, kq = d // 4, d // 2, d // 4, C_HEADS * C_KEY_DIM
    ref_order = (("a_u", a), ("a_v", a), ("a_z", a), ("b_q", bw), ("b_k", bw), ("b_v", bw), ("b_o", bw),
                 ("b_z", bw), ("gates", 4 * B_HEADS), ("c_q", kq), ("c_f_fwd", kq), ("c_f_bwd", kq),
                 ("c_i", c), ("c_g", c))
    src, start = {}, 0
    for name, w in ref_order:
        src[name] = (start, w)
        start += w
    groups = {"ab": ("b_q", "b_k", "b_v", "b_o", "b_z", "a_u", "a_v", "a_z"),
              "cq": ("c_q", "c_i", "c_g"), "cf": ("c_f_fwd", "c_f_bwd")}
    cols = {}
    for names in groups.values():
        pos = 0
        for name in names:
            cols[name] = pos
            pos += src[name][1]
    return src, groups, cols


def _pack_cols(w, bias, src, names):
    spans = []
    for k in names:
        lo, width = src[k]
        if spans and spans[-1][1] == lo:
            spans[-1][1] = lo + width
        else:
            spans.append([lo, lo + width])
    pick = lambda a: jnp.concatenate([a[..., lo:hi] for lo, hi in spans], axis=-1)
    return pick(w).astype(BF16), pick(bias)[None, :]


def _grid_transpose(x, t_lat, rows, width):
    b, tt, f = x.shape
    lat = x[:, :t_lat].reshape(b, rows, width, f).swapaxes(1, 2)
    ctx = x[:, t_lat:].reshape(b, (tt - t_lat) // rows, rows, f)
    return jnp.concatenate([lat, ctx], axis=1).reshape(b, tt, f)


def kernel(x, c, ctx, c_ctx, w_ada, b_ada, norm_g, w_in, b_in, w_spatial, b_spatial, conv_qk, mlstm_norm,
           hgrn_lb_logits, hgrn_norm, w_out, final_norm):
    b, t_lat, d = x.shape
    t_ctx = ctx.shape[1]
    tt = t_lat + t_ctx
    depth = w_ada.shape[0]
    src, groups, cols = _packed_layout(d)
    a_width, b_width = d // 4, d // 2
    rows = t_lat // GRID_W

    r_pad = -(-(b + 1) // 8) * 8
    cond = jnp.concatenate([c, c_ctx[None, :], jnp.zeros((r_pad - b - 1, d), F32)], axis=0)
    mod_all = _ada_mod(cond, w_ada, b_ada)
    mods = [jnp.stack([mod_all[l, :b], jnp.broadcast_to(mod_all[l, b], (b, 3 * d))], axis=1).reshape(2 * b, 1, 3 * d)
            for l in range(depth)]

    lbf, oml = _lower_bounds(hgrn_lb_logits.astype(F32))

    e_rows = lax.broadcasted_iota(jnp.int32, (C_SUB * C_KEY_DIM, C_CHUNK), 0) // C_KEY_DIM
    e_cols = lax.broadcasted_iota(jnp.int32, (C_SUB * C_KEY_DIM, C_CHUNK), 1) % C_SUB
    e_mat = (e_rows == e_cols).astype(BF16)

    h, n = _join_norm_mod(x, ctx, norm_g[0:1], mods[0])
    out = None
    for l in range(depth):
        last = l == depth - 1
        w_ab, b_ab = _pack_cols(w_in[l], b_in[l], src, groups["ab"])
        w_cq, b_cq = _pack_cols(w_in[l], b_in[l], src, groups["cq"])
        w_cf, b_cf = _pack_cols(w_in[l], b_in[l], src, groups["cf"])
        g0, gw = src["gates"]
        w_gate_t = w_in[l][:, g0:g0 + gw].T.astype(BF16)
        b_gate = b_in[l][g0:g0 + gw][:, None]

        n_cm = _grid_transpose(n, t_lat, rows, GRID_W)
        n2d, n_cm2d = n.reshape(b * tt, d), n_cm.reshape(b * tt, d)
        p2d = _in_proj(n2d, w_ab, b_ab, 1024, 1664, "in_proj_ab", BF16)
        pcq = _in_proj(n_cm2d, w_cq, b_cq, 1024, 1536, "in_proj_cq", BF16).reshape(b, tt, -1)
        pcf = _in_proj(n_cm2d, w_cf, b_cf, 1024, 1024, "in_proj_cf").reshape(b, tt, -1)
        gates_row = _in_proj_gates_rows(n, w_gate_t, b_gate)
        p3d = p2d.reshape(b, tt, -1)

        ya = _chunk_mlp(p2d, w_spatial[l].astype(BF16), b_spatial[l].T, cols["a_u"], a_width).reshape(b, tt, a_width)

        q = _conv_silu(p3d, conv_qk[l], t_lat, cols["b_q"], 0, b_width, 1.0, transpose=False)
        k_t = _conv_silu(p3d, conv_qk[l], t_lat, cols["b_k"], b_width, b_width,
                         (b_width // B_HEADS) ** -0.5, transpose=True)
        hf = _mlstm_dir(q, k_t, p3d, gates_row, t_lat, cols, reverse=False)
        yb = _mlstm_dir(q, k_t, p3d, gates_row, t_lat, cols, reverse=True, hf=hf, norm_g=mlstm_norm[l:l + 1])

        lbf_l, oml_l = lbf[l:l + 1], oml[l:l + 1]
        of = _hgrn_dir(pcq, pcf, lbf_l, oml_l, e_mat, cols, t_lat, False)
        yc_cm = _hgrn_dir(pcq, pcf, lbf_l, oml_l, e_mat, cols, t_lat, True, of=of, norm_g=hgrn_norm[l:l + 1])
        yc = _grid_transpose(yc_cm, t_lat, GRID_W, rows)

        w_o = w_out[l].astype(BF16)
        if last:
            out = _out_proj(ya, yb, yc, w_o, h, mods[l], final_norm[None, :], None, t_lat, final=True)
        else:
            h, n = _out_proj(ya, yb, yc, w_o, h, mods[l], norm_g[l + 1:l + 2], mods[l + 1], t_lat, final=False)
    return out
```

```python
import functools

import jax
import jax.numpy as jnp
from jax import lax
from jax.experimental import pallas as pl
from jax.experimental.pallas import tpu as pltpu

EPS = 1e-6
NEG_BIG = -1e30
LB_FLOOR = 1e-30
GRID_W = 64
CONV_W = 3

A_GROUPS = 4
A_CHUNK = 128
B_HEADS = 4
B_CHUNK = 256
C_HEADS = 4
C_KEY_DIM = 128
C_CHUNK = 64
C_SUB = 8
GATE_PAD = 128
MLSTM_AUG = 128

V7X_VMEM_LIMIT = 56 * 1024 * 1024

F32 = jnp.float32
BF16 = jnp.bfloat16


def _params(*sem):
    return pltpu.CompilerParams(dimension_semantics=sem, vmem_limit_bytes=V7X_VMEM_LIMIT)


def _sigmoid(x):
    return 1.0 / (1.0 + jnp.exp(-x))


def _silu(x):
    return x * _sigmoid(x)


def _log_sigmoid(x):
    return jnp.minimum(x, 0.0) - jnp.log1p(jnp.exp(-jnp.abs(x)))


def _dot(a, b):
    return jnp.dot(a, b, preferred_element_type=F32)


def _dot_nt(a, b):
    return lax.dot_general(a, b, (((1,), (1,)), ((), ())), preferred_element_type=F32)


def _dot_tn(a, b):
    return lax.dot_general(a, b, (((0,), (0,)), ((), ())), preferred_element_type=F32)


def _lb_kernel(x_ref, lbf_ref, oml_ref):
    x = x_ref[...]
    depth = x.shape[0]
    e = jnp.exp(x - jnp.max(x, axis=0, keepdims=True))
    p = e / jnp.sum(e, axis=0, keepdims=True)
    rows = lax.broadcasted_iota(jnp.int32, x.shape, 0)
    lb = jnp.zeros_like(x)
    for j in range(1, depth):
        lb = lb + jnp.where(rows >= j, p[j:j + 1, :], 0.0)
    lbf_ref[...] = jnp.maximum(lb, LB_FLOOR)
    oml_ref[...] = 1.0 - lb


def _lower_bounds(logits):
    shp = jax.ShapeDtypeStruct(logits.shape, F32)
    return pl.pallas_call(_lb_kernel, out_shape=(shp, shp), name="hgrn_lower_bounds")(logits)


def _ada_kernel(c_ref, w_ref, b_ref, o_ref):
    s = _silu(c_ref[...]).astype(BF16)
    o_ref[0] = _dot(s, w_ref[0].astype(BF16)) + b_ref[0]


def _ada_mod(cond, w_ada, b_ada, tn=768):
    depth, d, n3 = w_ada.shape
    r = cond.shape[0]
    return pl.pallas_call(
        _ada_kernel,
        grid=(depth, n3 // tn),
        in_specs=[pl.BlockSpec((r, d), lambda l, j: (0, 0)),
                  pl.BlockSpec((1, d, tn), lambda l, j: (l, 0, j)),
                  pl.BlockSpec((1, 1, tn), lambda l, j: (l, 0, j))],
        out_specs=pl.BlockSpec((1, r, tn), lambda l, j: (l, 0, j)),
        out_shape=jax.ShapeDtypeStruct((depth, r, n3), F32),
        compiler_params=_params("parallel", "parallel"),
        name="adaln_mod",
    )(cond, w_ada, b_ada.reshape(depth, 1, n3))


def _modulated_norm(h, g, mod, d):
    y = h * lax.rsqrt(jnp.mean(h * h, axis=-1, keepdims=True) + EPS) * g
    return y * (1.0 + mod[:, d:2 * d]) + mod[:, 0:d]


def _norm_kernel(x_ref, ctx_ref, g_ref, mod_ref, h_ref, n_ref, *, tiles_lat):
    d = x_ref.shape[-1]

    def emit(src_ref):
        h = src_ref[0]
        h_ref[0] = h
        n_ref[0] = _modulated_norm(h, g_ref[...], mod_ref[0], d).astype(n_ref.dtype)

    is_lat = pl.program_id(1) < tiles_lat
    pl.when(is_lat)(lambda: emit(x_ref))
    pl.when(jnp.logical_not(is_lat))(lambda: emit(ctx_ref))


def _mod_index(tiles_lat):
    return lambda b, j: (2 * b + jnp.where(j >= tiles_lat, 1, 0), 0, 0)


def _join_norm_mod(x, ctx, g, mod, tm=256):
    b, t_lat, d = x.shape
    tt = t_lat + ctx.shape[1]
    tl = t_lat // tm
    tok = pl.BlockSpec((1, tm, d), lambda i, j: (i, j, 0))
    return pl.pallas_call(
        functools.partial(_norm_kernel, tiles_lat=tl),
        grid=(b, tt // tm),
        in_specs=[pl.BlockSpec((1, tm, d), lambda i, j: (i, jnp.minimum(j, tl - 1), 0)),
                  pl.BlockSpec((1, tm, d), lambda i, j: (i, jnp.maximum(j - tl, 0), 0)),
                  pl.BlockSpec((1, d), lambda i, j: (0, 0)),
                  pl.BlockSpec((1, 1, 3 * d), _mod_index(tl))],
        out_specs=(tok, tok),
        out_shape=(jax.ShapeDtypeStruct((b, tt, d), F32), jax.ShapeDtypeStruct((b, tt, d), BF16)),
        compiler_params=_params("parallel", "arbitrary"),
        name="join_norm_mod",
    )(x, ctx, g, mod)


def _matmul_bias_kernel(x_ref, w_ref, b_ref, o_ref):
    o_ref[...] = (_dot(x_ref[...], w_ref[...]) + b_ref[...]).astype(o_ref.dtype)


def _tile(m, pref, unit=128):
    t = min(pref, m) // unit * unit
    while m % t:
        t -= unit
    return t


def _in_proj(n2d, w, bias, tm, tn, name, out_dtype=F32):
    m, d = n2d.shape
    n = w.shape[1]
    tm = _tile(m, tm)
    return pl.pallas_call(
        _matmul_bias_kernel,
        grid=(n // tn, m // tm),
        in_specs=[pl.BlockSpec((tm, d), lambda j, i: (i, 0)),
                  pl.BlockSpec((d, tn), lambda j, i: (0, j)),
                  pl.BlockSpec((1, tn), lambda j, i: (0, j))],
        out_specs=pl.BlockSpec((tm, tn), lambda j, i: (i, j)),
        out_shape=jax.ShapeDtypeStruct((m, n), out_dtype),
        compiler_params=_params("parallel", "parallel"),
        name=name,
    )(n2d, w, bias)


def _gates_kernel(x_ref, w_ref, b_ref, o_ref):
    o_ref[0] = _dot_nt(w_ref[...], x_ref[0]) + b_ref[...]


def _in_proj_gates_rows(n, w_t, bias_col):
    b, tt, d = n.shape
    ng = w_t.shape[0]
    tm = _tile(tt, 2560)
    return pl.pallas_call(
        _gates_kernel,
        grid=(b, tt // tm),
        in_specs=[pl.BlockSpec((1, tm, d), lambda i, j: (i, j, 0)),
                  pl.BlockSpec((ng, d), lambda i, j: (0, 0)),
                  pl.BlockSpec((ng, 1), lambda i, j: (0, 0))],
        out_specs=pl.BlockSpec((1, ng, tm), lambda i, j: (i, 0, j)),
        out_shape=jax.ShapeDtypeStruct((b, ng, tt), F32),
        compiler_params=_params("parallel", "parallel"),
        name="in_proj_gates",
    )(n, w_t, bias_col)


def _chunk_mlp_kernel(u_ref, v_ref, z_ref, ws_ref, bs_ref, y_ref):
    tm, width = v_ref.shape
    gd = width // A_GROUPS
    for c in range(tm // A_CHUNK):
        rows = slice(c * A_CHUNK, (c + 1) * A_CHUNK)
        for g in range(A_GROUPS):
            cols = slice(g * gd, (g + 1) * gd)
            v = v_ref[rows, cols].astype(F32)
            mu = jnp.mean(v, axis=-1, keepdims=True)
            vc = v - mu
            var = jnp.mean(vc * vc, axis=-1, keepdims=True)
            vn = (vc * lax.rsqrt(var + EPS)).astype(BF16)
            mixed = _dot(ws_ref[g], vn) + bs_ref[:, g:g + 1]
            gate = _silu(z_ref[rows, cols].astype(F32))
            y_ref[rows, cols] = (u_ref[rows, cols].astype(F32) * mixed * gate).astype(y_ref.dtype)


def _chunk_mlp(p2d, ws, bs_t, col_u, width, tm=1024):
    m = p2d.shape[0]
    tm = _tile(m, tm)
    cb = col_u // width
    spec = lambda k: pl.BlockSpec((tm, width), lambda i, k=k: (i, cb + k))
    return pl.pallas_call(
        _chunk_mlp_kernel,
        grid=(m // tm,),
        in_specs=[spec(0), spec(1), spec(2),
                  pl.BlockSpec(ws.shape, lambda i: (0, 0, 0)),
                  pl.BlockSpec(bs_t.shape, lambda i: (0, 0))],
        out_specs=pl.BlockSpec((tm, width), lambda i: (i, 0)),
        out_shape=jax.ShapeDtypeStruct((m, width), BF16),
        compiler_params=_params("parallel"),
        name="chunk_mlp",
    )(p2d, p2d, p2d, ws, bs_t)


def _conv_kernel(x_ref, w_ref, o_ref, *, t_lat, scale, transpose):
    x = x_ref[0].astype(F32)
    tt = x.shape[0]
    rows = lax.broadcasted_iota(jnp.int32, x.shape, 0)
    prev = jnp.where((rows == 0) | (rows == t_lat), 0.0, pltpu.roll(x, 1, axis=0))
    nxt = jnp.where((rows == t_lat - 1) | (rows == tt - 1), 0.0, pltpu.roll(x, tt - 1, axis=0))
    w = w_ref[...]
    y = _silu(w[0:1] * prev + w[1:2] * x + w[2:3] * nxt) * scale
    o_ref[0] = (y.T if transpose else y).astype(o_ref.dtype)


def _conv_silu(p3d, conv_w, t_lat, col0, wcol0, width, scale, transpose, tc=256):
    b, tt, _ = p3d.shape
    kern = functools.partial(_conv_kernel, t_lat=t_lat, scale=scale, transpose=transpose)
    if transpose:
        out_spec = pl.BlockSpec((1, tc, tt), lambda i, j: (i, j, 0))
        out_shape = jax.ShapeDtypeStruct((b, width, tt), BF16)
    else:
        out_spec = pl.BlockSpec((1, tt, tc), lambda i, j: (i, 0, j))
        out_shape = jax.ShapeDtypeStruct((b, tt, width), BF16)
    return pl.pallas_call(
        kern,
        grid=(b, width // tc),
        in_specs=[pl.BlockSpec((1, tt, tc), lambda i, j: (i, 0, col0 // tc + j)),
                  pl.BlockSpec((CONV_W, tc), lambda i, j: (0, wcol0 // tc + j))],
        out_specs=out_spec,
        out_shape=out_shape,
        compiler_params=_params("parallel", "parallel"),
        name="conv_k_t" if transpose else "conv_q",
    )(p3d, conv_w)


def _mlstm_kernel(*refs, reverse, readout, hd, gb):
    units = [(g, h) for g in range(gb) for h in range(B_HEADS)]
    nu = len(units)
    refs, c_scr, m_scr = refs[:-2 * nu], refs[-2 * nu:-nu], refs[-nu:]
    if readout:
        q_ref, kt_ref, v_ref, gr_ref, hf_ref, o_ref, z_ref, g_ref, out_ref = refs
    else:
        q_ref, kt_ref, v_ref, gr_ref, out_ref = refs

    @pl.when(pl.program_id(1) == 0)
    def _():
        for scr in c_scr + m_scr:
            scr[...] = jnp.zeros_like(scr)

    L = B_CHUNK
    row = lax.broadcasted_iota(jnp.int32, (L, L), 0)
    col = lax.broadcasted_iota(jnp.int32, (L, L), 1)
    seen = (col >= row) if reverse else (col <= row)
    tri_t = jnp.where((row >= col) if reverse else (row <= col), 1.0, 0.0).astype(BF16)
    last = 0 if reverse else L - 1
    gi, gf = (2 * B_HEADS, 3 * B_HEADS) if reverse else (0, B_HEADS)
    cols = [slice(h * hd, (h + 1) * hd) for h in range(B_HEADS)]
    ones_blk = jnp.ones((L, MLSTM_AUG), BF16)
    lane_tile = lambda x, width: jnp.concatenate([x] * (width // MLSTM_AUG), axis=1)
    gates = lambda lo_row: jnp.concatenate([gr_ref[g, lo_row:lo_row + B_HEADS, :] for g in range(gb)], axis=0)

    i_rows = gates(gi)
    f_rows = _log_sigmoid(gates(gf))
    hi = f_rows.astype(BF16).astype(F32)
    r1 = f_rows - hi
    mid = r1.astype(BF16).astype(F32)
    lo = (r1 - mid).astype(BF16).astype(F32)
    terms = jnp.concatenate([hi, mid, lo, jnp.zeros_like(hi)], axis=0).astype(BF16)
    sums = _dot(terms, tri_t)
    cb_rows = sums[0:nu] + sums[nu:2 * nu] + sums[2 * nu:3 * nu]
    tri = jnp.where(seen, 1.0, 0.0).astype(BF16)
    rep = lambda x, u: jnp.broadcast_to(x[u:u + 1], (MLSTM_AUG, L))
    cb_cols = []
    for u in range(nu):
        c3 = _dot_nt(tri, jnp.concatenate([rep(hi, u), rep(mid, u), rep(lo, u)], axis=0).astype(BF16))
        cb_cols.append(c3[:, 0:MLSTM_AUG] + c3[:, MLSTM_AUG:2 * MLSTM_AUG] + c3[:, 2 * MLSTM_AUG:])

    s_bf, e_inter, emt, ws, ec, m_new = [], [], [], [], [], []
    for u, (g, h) in enumerate(units):
        i_row, cb_row = i_rows[u:u + 1], cb_rows[u:u + 1]
        dmat = jnp.where(seen, lane_tile(cb_cols[u], L) - cb_row + i_row, NEG_BIG)
        m = m_scr[u][0:1, 0:1]
        inter = cb_cols[u] + m
        mt = jnp.maximum(inter, jnp.max(dmat, axis=1, keepdims=True))
        decay = jnp.exp(dmat - lane_tile(mt, L))
        s_bf.append((_dot(q_ref[g, :, cols[h]], kt_ref[g, cols[h], :]) * decay).astype(BF16))
        e_inter.append(jnp.exp(inter - mt))
        emt.append(jnp.exp(-mt))
        cl = cb_row[:, last:last + 1]
        w_log = cl - cb_row + i_row
        m_new.append(jnp.maximum(cl + m, jnp.max(w_log, axis=1, keepdims=True)))
        ec.append(jnp.exp(cl + m - m_new[u]))
        ws.append(jnp.exp(w_log - m_new[u]))

    sv, qc, upd = [], [], []
    for u, (g, h) in enumerate(units):
        v_aug = jnp.concatenate([v_ref[g, :, cols[h]].astype(BF16), ones_blk], axis=1)
        sv.append(_dot(s_bf[u], v_aug))
        qc.append(_dot(q_ref[g, :, cols[h]], c_scr[u][...].astype(BF16)))
        kw_t = (kt_ref[g, cols[h], :].astype(F32) * ws[u]).astype(BF16)
        upd.append(_dot(kw_t, v_aug))
    for u, (g, h) in enumerate(units):
        tot = sv[u] + lane_tile(e_inter[u], hd + MLSTM_AUG) * qc[u]
        inv = 1.0 / jnp.maximum(jnp.abs(tot[:, hd:hd + MLSTM_AUG]), emt[u])
        hc = jnp.concatenate([tot[:, c0:c0 + MLSTM_AUG] * inv for c0 in range(0, hd, MLSTM_AUG)], axis=1)
        c_scr[u][...] = ec[u] * c_scr[u][...] + upd[u]
        m_scr[u][...] = jnp.broadcast_to(m_new[u], m_scr[u].shape)
        if readout:
            hs = hf_ref[g, :, cols[h]] + hc
            hn = hs * lax.rsqrt(jnp.mean(hs * hs, axis=-1, keepdims=True) + EPS) * g_ref[:, cols[h]]
            y = hn * _sigmoid(o_ref[g, :, cols[h]].astype(F32)) * _silu(z_ref[g, :, cols[h]].astype(F32))
            out_ref[g, :, cols[h]] = y.astype(out_ref.dtype)
        else:
            out_ref[g, :, cols[h]] = hc.astype(out_ref.dtype)


def _mlstm_dir(q, k_t, p3d, gates_row, t_lat, cols, reverse, hf=None, norm_g=None, gb=2):
    b, tt, width = q.shape
    gb = max(g for g in range(1, gb + 1) if b % g == 0)
    hd = width // B_HEADS
    L = B_CHUNK
    nch, nlat = tt // L, t_lat // L
    if reverse:
        chunk = lambda i: nch - 1 - i
    else:
        chunk = lambda i: lax.rem(i + nlat, nch)
    wb = lambda name: cols[name] // width
    tok = lambda cb: pl.BlockSpec((gb, L, width), lambda bi, i, cb=cb: (bi, chunk(i), cb))
    in_specs = [tok(0), pl.BlockSpec((gb, width, L), lambda bi, i: (bi, 0, chunk(i))), tok(wb("b_v")),
                pl.BlockSpec((gb, gates_row.shape[1], L), lambda bi, i: (bi, 0, chunk(i)))]
    args = [q, k_t, p3d, gates_row]
    readout = hf is not None
    if readout:
        in_specs += [tok(0), tok(wb("b_o")), tok(wb("b_z")), pl.BlockSpec((1, width), lambda bi, i: (0, 0))]
        args += [hf, p3d, p3d, norm_g]
    kern = functools.partial(_mlstm_kernel, reverse=reverse, readout=readout, hd=hd, gb=gb)
    nu = gb * B_HEADS
    return pl.pallas_call(
        kern,
        grid=(b // gb, nch),
        in_specs=in_specs,
        out_specs=tok(0),
        out_shape=jax.ShapeDtypeStruct((b, tt, width), BF16),
        scratch_shapes=[pltpu.VMEM((hd, hd + MLSTM_AUG), F32)] * nu + [pltpu.VMEM((8, 128), F32)] * nu,
        compiler_params=_params("parallel", "arbitrary"),
        name="mlstm_bwd_readout" if readout else "mlstm_fwd",
    )(*args)


def _hgrn_kernel(*refs, reverse, readout, dv, gb):
    if readout:
        q_ref, z_ref, v_ref, lbf_ref, oml_ref, e_ref, of_ref, cg_ref, g_ref, out_ref, s_scr, row_scr = refs
    else:
        q_ref, z_ref, v_ref, lbf_ref, oml_ref, e_ref, out_ref, s_scr, row_scr = refs

    @pl.when(pl.program_id(1) == 0)
    def _():
        s_scr[...] = jnp.zeros_like(s_scr)

    L, dk, nb = C_CHUNK, C_KEY_DIM, C_CHUNK // C_SUB
    row = lax.broadcasted_iota(jnp.int32, (L, L), 0)
    col = lax.broadcasted_iota(jnp.int32, (L, L), 1)
    rb, cbk = row // C_SUB, col // C_SUB
    if reverse:
        seen, blk_before, last = col >= row, cbk > rb, 0
    else:
        seen, blk_before, last = col <= row, cbk < rb, L - 1
    sum_mat = jnp.concatenate([jnp.where(seen, 1.0, 0.0), jnp.where(blk_before, 1.0, 0.0)], axis=0).astype(BF16)
    same_blk = rb == cbk
    sub = lax.broadcasted_iota(jnp.int32, (L, dk), 0) % C_SUB
    pair_ok = [(sub <= j) if reverse else (sub >= j) for j in range(C_SUB)]

    def bcast_sub(ref, j):
        return jnp.concatenate([jnp.broadcast_to(ref[i * C_SUB + j:i * C_SUB + j + 1, :], (C_SUB, dk))
                                for i in range(nb)], axis=0)

    units = [(g, h) for g in range(gb) for h in range(C_HEADS)]
    kcs = [slice(h * dk, (h + 1) * dk) for h in range(C_HEADS)]
    vcs = [slice(h * dv, (h + 1) * dv) for h in range(C_HEADS)]

    q, k, v, c3 = [], [], [], []
    for u, (g, h) in enumerate(units):
        q.append(_silu(q_ref[g, :, kcs[h]].astype(F32)))
        z = z_ref[g, :, kcs[h]]
        v.append(v_ref[g, :, vcs[h]].astype(BF16))
        a = jnp.exp(-jnp.abs(z))
        r = 1.0 / (1.0 + a)
        pos = z >= 0.0
        oml = oml_ref[:, kcs[h]]
        f = lbf_ref[:, kcs[h]] + oml * jnp.where(pos, r, a * r)
        k.append(oml * jnp.where(pos, a * r, r))
        lf = jnp.log2(f)
        hi = lf.astype(BF16)
        r1 = lf - hi.astype(F32)
        mid = r1.astype(BF16)
        lo = (r1 - mid.astype(F32)).astype(BF16)
        c3.append(_dot(sum_mat, jnp.concatenate([hi, mid, lo], axis=1)))

    cb, a_off, a_diag = [], [], []
    for u, (g, h) in enumerate(units):
        c1 = c3[u][:, 0:dk] + c3[u][:, dk:2 * dk] + c3[u][:, 2 * dk:3 * dk]
        cbh, entry = c1[0:L], c1[L:2 * L]
        cb.append(cbh)
        qd = (q[u] * jnp.exp2(cbh - entry)).astype(BF16)
        parts = []
        for i in range(nb):
            lo_r, hi_r = ((i + 1) * C_SUB, L) if reverse else (0, i * C_SUB)
            if hi_r == lo_r:
                parts.append(jnp.zeros((C_SUB, L), F32))
                continue
            ent = entry[i * C_SUB:i * C_SUB + 1, :]
            kd = (k[u][lo_r:hi_r] * jnp.exp2(ent - cbh[lo_r:hi_r])).astype(BF16)
            pad = [jnp.zeros((n, dk), BF16) for n in (lo_r, L - hi_r)]
            kd = jnp.concatenate([p for p in (pad[0], kd, pad[1]) if p.shape[0]], axis=0)
            parts.append(_dot_nt(qd[i * C_SUB:(i + 1) * C_SUB], kd))
        a_off.append(jnp.concatenate(parts, axis=0))
        qk_parts = []
        cb_rows, k_rows = row_scr.at[2 * u], row_scr.at[2 * u + 1]
        cb_rows[...] = cbh
        k_rows[...] = k[u]
        for j in range(C_SUB):
            dec = jnp.exp2(jnp.where(pair_ok[j], cbh - bcast_sub(cb_rows, j), NEG_BIG))
            qk_parts.append((q[u] * bcast_sub(k_rows, j) * dec).astype(BF16))
        a_diag.append(_dot(jnp.concatenate(qk_parts, axis=1), e_ref[...]))

    for u, (g, h) in enumerate(units):
        scores = a_off[u] + jnp.where(same_blk, a_diag[u], 0.0)
        st = s_scr[u]
        o = _dot(scores.astype(BF16), v[u]) + _dot_nt((q[u] * jnp.exp2(cb[u])).astype(BF16), st.astype(BF16))
        cl = cb[u][last:last + 1, :]
        kdec = (k[u] * jnp.exp2(cl - cb[u])).astype(BF16)
        s_scr[u] = st * jnp.exp2(cl) + _dot_tn(v[u], kdec)
        if readout:
            os_ = of_ref[g, :, vcs[h]] + o
            on = os_ * lax.rsqrt(jnp.mean(os_ * os_, axis=-1, keepdims=True) + EPS) * g_ref[:, vcs[h]]
            out_ref[g, :, vcs[h]] = (on * _silu(cg_ref[g, :, vcs[h]].astype(F32))).astype(out_ref.dtype)
        else:
            out_ref[g, :, vcs[h]] = o.astype(out_ref.dtype)


def _hgrn_dir(pcq, pcf, lbf, oml, e_mat, cols, t_lat, reverse, of=None, norm_g=None, gb=8):
    b, tt, _ = pcq.shape
    kw, L = C_HEADS * C_KEY_DIM, C_CHUNK
    vw = cols["c_g"] - cols["c_i"]
    dv = vw // C_HEADS
    nch, nlat = tt // L, t_lat // L
    if reverse:
        chunk = lambda i: nch - 1 - i
    else:
        chunk = lambda i: lax.rem(i + nlat, nch)
    gb = max(g for g in range(1, gb + 1) if b % g == 0)
    tok = lambda c0, w: pl.BlockSpec((gb, L, w), lambda bi, i: (bi, chunk(i), c0 // w))
    const2 = lambda x: pl.BlockSpec(x.shape, lambda bi, i: (0, 0))
    f_name = "c_f_bwd" if reverse else "c_f_fwd"
    in_specs = [tok(cols["c_q"], kw), tok(cols[f_name], kw), tok(cols["c_i"], vw),
                const2(lbf), const2(oml), const2(e_mat)]
    args = [pcq, pcf, pcq, lbf, oml, e_mat]
    readout = of is not None
    if readout:
        in_specs += [tok(0, vw), tok(cols["c_g"], vw), const2(norm_g)]
        args += [of, pcq, norm_g]
    kern = functools.partial(_hgrn_kernel, reverse=reverse, readout=readout, dv=dv, gb=gb)
    return pl.pallas_call(
        kern,
        grid=(b // gb, nch),
        in_specs=in_specs,
        out_specs=tok(0, vw),
        out_shape=jax.ShapeDtypeStruct((b, tt, vw), BF16),
        scratch_shapes=[pltpu.VMEM((gb * C_HEADS, dv, C_KEY_DIM), F32),
                        pltpu.VMEM((2 * gb * C_HEADS, L, C_KEY_DIM), F32)],
        compiler_params=_params("parallel", "arbitrary"),
        name="hgrn_bwd_readout" if readout else "hgrn_fwd",
    )(*args)


def _out_kernel(ya_ref, yb_ref, yc_ref, w_ref, h_ref, mod_ref, *rest, final, wa, wb):
    d = h_ref.shape[-1]
    w = w_ref
    y = (_dot(ya_ref[0], w[0:wa, :]) + _dot(yb_ref[0], w[wa:wa + wb, :]) + _dot(yc_ref[0], w[wa + wb:, :]))
    h_new = h_ref[0] + mod_ref[0][:, 2 * d:3 * d] * y
    if final:
        g_ref, out_ref = rest
        out_ref[0] = h_new * lax.rsqrt(jnp.mean(h_new * h_new, axis=-1, keepdims=True) + EPS) * g_ref[...]
    else:
        g_ref, modn_ref, h_out_ref, n_ref = rest
        h_out_ref[0] = h_new
        n_ref[0] = _modulated_norm(h_new, g_ref[...], modn_ref[0], d).astype(n_ref.dtype)


def _out_proj(ya, yb, yc, w_out, h, mod, g_next, mod_next, t_lat, final, tm=256):
    b, tt, d = h.shape
    wa, wb, wc = ya.shape[-1], yb.shape[-1], yc.shape[-1]
    if final:
        tm = _tile(t_lat, 512)
    tok = lambda w: pl.BlockSpec((1, tm, w), lambda i, j: (i, j, 0))
    mod_spec = pl.BlockSpec((1, 1, 3 * d), _mod_index(t_lat // tm))
    in_specs = [tok(wa), tok(wb), tok(wc), pl.BlockSpec(w_out.shape, lambda i, j: (0, 0)), tok(d), mod_spec,
                pl.BlockSpec((1, d), lambda i, j: (0, 0))]
    args = [ya, yb, yc, w_out, h, mod, g_next]
    kern = functools.partial(_out_kernel, final=final, wa=wa, wb=wb)
    if final:
        return pl.pallas_call(
            kern, grid=(b, t_lat // tm), in_specs=in_specs, out_specs=tok(d),
            out_shape=jax.ShapeDtypeStruct((b, t_lat, d), F32),
            compiler_params=_params("parallel", "parallel"), name="out_proj_final",
        )(*args)
    return pl.pallas_call(
        kern, grid=(b, tt // tm), in_specs=in_specs + [mod_spec], out_specs=(tok(d), tok(d)),
        out_shape=(jax.ShapeDtypeStruct((b, tt, d), F32), jax.ShapeDtypeStruct((b, tt, d), BF16)),
        compiler_params=_params("parallel", "parallel"), name="out_proj",
    )(*args, mod_next)


def _packed_layout(d):
    a, bw, c, kq = d // 4, d // 2, d // 4, C_HEADS * C_KEY_DIM
    ref_order = (("a_u", a), ("a_v", a), ("a_z", a), ("b_q", bw), ("b_k", bw), ("b_v", bw), ("b_o", bw),
                 ("b_z", bw), ("gates", 4 * B_HEADS), ("c_q", kq), ("c_f_fwd", kq), ("c_f_bwd", kq),
                 ("c_i", c), ("c_g", c))
    src, start = {}, 0
    for name, w in ref_order:
        src[name] = (start, w)
        start += w
    groups = {"ab": ("b_q", "b_k", "b_v", "b_o", "b_z", "a_u", "a_v", "a_z"),
              "cq": ("c_q", "c_i", "c_g"), "cf": ("c_f_fwd", "c_f_bwd")}
    cols = {}
    for names in groups.values():
        pos = 0
        for name in names:
            cols[name] = pos
            pos += src[name][1]
    return src, groups, cols


def _pack_cols(w, bias, src, names):
    spans = []
    for k in names:
        lo, width = src[k]
        if spans and spans[-1][1] == lo:
            spans[-1][1] = lo + width
        else:
            spans.append([lo, lo + width])
    pick = lambda a: jnp.concatenate([a[..., lo:hi] for lo, hi in spans], axis=-1)
    return pick(w).astype(BF16), pick(bias)[None, :]


def _grid_transpose(x, t_lat, rows, width):
    b, _, f = x.shape
    lat = x[:, :t_lat].reshape(b, rows, width, f).swapaxes(1, 2).reshape(b, t_lat, f)
    return jnp.concatenate([lat, x[:, t_lat:]], axis=1)


def kernel(x, c, ctx, c_ctx, w_ada, b_ada, norm_g, w_in, b_in, w_spatial, b_spatial, conv_qk, mlstm_norm,
           hgrn_lb_logits, hgrn_norm, w_out, final_norm):
    b, t_lat, d = x.shape
    t_ctx = ctx.shape[1]
    tt = t_lat + t_ctx
    depth = w_ada.shape[0]
    src, groups, cols = _packed_layout(d)
    a_width, b_width = d // 4, d // 2
    rows = t_lat // GRID_W

    r_pad = -(-(b + 1) // 8) * 8
    cond = jnp.concatenate([c, c_ctx[None, :], jnp.zeros((r_pad - b - 1, d), F32)], axis=0)
    mod_all = _ada_mod(cond, w_ada, b_ada)
    mods = [jnp.stack([mod_all[l, :b], jnp.broadcast_to(mod_all[l, b], (b, 3 * d))], axis=1).reshape(2 * b, 1, 3 * d)
            for l in range(depth)]

    lbf, oml = _lower_bounds(hgrn_lb_logits.astype(F32))

    e_rows = lax.broadcasted_iota(jnp.int32, (C_SUB * C_KEY_DIM, C_CHUNK), 0) // C_KEY_DIM
    e_cols = lax.broadcasted_iota(jnp.int32, (C_SUB * C_KEY_DIM, C_CHUNK), 1) % C_SUB
    e_mat = (e_rows == e_cols).astype(BF16)

    h, n = _join_norm_mod(x, ctx, norm_g[0:1], mods[0])
    out = None
    for l in range(depth):
        last = l == depth - 1
        w_ab, b_ab = _pack_cols(w_in[l], b_in[l], src, groups["ab"])
        w_cq, b_cq = _pack_cols(w_in[l], b_in[l], src, groups["cq"])
        w_cf, b_cf = _pack_cols(w_in[l], b_in[l], src, groups["cf"])
        g0, gw = src["gates"]
        w_gate_t = w_in[l][:, g0:g0 + gw].T.astype(BF16)
        b_gate = b_in[l][g0:g0 + gw][:, None]

        n_cm = _grid_transpose(n, t_lat, rows, GRID_W)
        n2d, n_cm2d = n.reshape(b * tt, d), n_cm.reshape(b * tt, d)
        p2d = _in_proj(n2d, w_ab, b_ab, 2048, 1664, "in_proj_ab", BF16)
        pcq = _in_proj(n_cm2d, w_cq, b_cq, 1024, 1536, "in_proj_cq", BF16).reshape(b, tt, -1)
        pcf = _in_proj(n_cm2d, w_cf, b_cf, 1024, 1024, "in_proj_cf").reshape(b, tt, -1)
        gates_row = _in_proj_gates_rows(n, w_gate_t, b_gate)
        p3d = p2d.reshape(b, tt, -1)

        ya = _chunk_mlp(p2d, w_spatial[l].astype(BF16), b_spatial[l].T, cols["a_u"], a_width).reshape(b, tt, a_width)

        q = _conv_silu(p3d, conv_qk[l], t_lat, cols["b_q"], 0, b_width, 1.0, transpose=False)
        k_t = _conv_silu(p3d, conv_qk[l], t_lat, cols["b_k"], b_width, b_width,
                         (b_width // B_HEADS) ** -0.5, transpose=True)
        hf = _mlstm_dir(q, k_t, p3d, gates_row, t_lat, cols, reverse=False)
        yb = _mlstm_dir(q, k_t, p3d, gates_row, t_lat, cols, reverse=True, hf=hf, norm_g=mlstm_norm[l:l + 1])

        lbf_l, oml_l = lbf[l:l + 1], oml[l:l + 1]
        of = _hgrn_dir(pcq, pcf, lbf_l, oml_l, e_mat, cols, t_lat, False)
        yc_cm = _hgrn_dir(pcq, pcf, lbf_l, oml_l, e_mat, cols, t_lat, True, of=of, norm_g=hgrn_norm[l:l + 1])
        yc = _grid_transpose(yc_cm, t_lat, GRID_W, rows)

        w_o = w_out[l].astype(BF16)
        if last:
            out = _out_proj(ya, yb, yc, w_o, h, mods[l], final_norm[None, :], None, t_lat, final=True)
        else:
            h, n = _out_proj(ya, yb, yc, w_o, h, mods[l], norm_g[l + 1:l + 2], mods[l + 1], t_lat, final=False)
    return out
```

```python
import functools

import jax
import jax.numpy as jnp
from jax import lax
from jax.experimental import pallas as pl
from jax.experimental.pallas import tpu as pltpu

EPS = 1e-6
NEG_BIG = -1e30
LB_FLOOR = 1e-30
GRID_W = 64
CONV_W = 3

A_GROUPS = 4
A_CHUNK = 128
B_HEADS = 4
B_CHUNK = 256
C_HEADS = 4
C_KEY_DIM = 128
C_CHUNK = 64
C_SUB = 8
GATE_PAD = 128
MLSTM_AUG = 128

V7X_VMEM_LIMIT = 56 * 1024 * 1024

F32 = jnp.float32
BF16 = jnp.bfloat16


def _params(*sem):
    return pltpu.CompilerParams(dimension_semantics=sem, vmem_limit_bytes=V7X_VMEM_LIMIT)


def _sigmoid(x):
    return 1.0 / (1.0 + jnp.exp(-x))


def _silu(x):
    return x * _sigmoid(x)


def _log_sigmoid(x):
    return jnp.minimum(x, 0.0) - jnp.log1p(jnp.exp(-jnp.abs(x)))


def _dot(a, b):
    return jnp.dot(a, b, preferred_element_type=F32)


def _dot_nt(a, b):
    return lax.dot_general(a, b, (((1,), (1,)), ((), ())), preferred_element_type=F32)


def _dot_tn(a, b):
    return lax.dot_general(a, b, (((0,), (0,)), ((), ())), preferred_element_type=F32)


def _lb_kernel(x_ref, lbf_ref, oml_ref):
    x = x_ref[...]
    depth = x.shape[0]
    e = jnp.exp(x - jnp.max(x, axis=0, keepdims=True))
    p = e / jnp.sum(e, axis=0, keepdims=True)
    rows = lax.broadcasted_iota(jnp.int32, x.shape, 0)
    lb = jnp.zeros_like(x)
    for j in range(1, depth):
        lb = lb + jnp.where(rows >= j, p[j:j + 1, :], 0.0)
    lbf_ref[...] = jnp.maximum(lb, LB_FLOOR)
    oml_ref[...] = 1.0 - lb


def _lower_bounds(logits):
    shp = jax.ShapeDtypeStruct(logits.shape, F32)
    return pl.pallas_call(_lb_kernel, out_shape=(shp, shp), name="hgrn_lower_bounds")(logits)


def _ada_kernel(c_ref, w_ref, b_ref, o_ref):
    s = _silu(c_ref[...]).astype(BF16)
    o_ref[0] = _dot(s, w_ref[0].astype(BF16)) + b_ref[0]


def _ada_mod(cond, w_ada, b_ada, tn=768):
    depth, d, n3 = w_ada.shape
    r = cond.shape[0]
    return pl.pallas_call(
        _ada_kernel,
        grid=(depth, n3 // tn),
        in_specs=[pl.BlockSpec((r, d), lambda l, j: (0, 0)),
                  pl.BlockSpec((1, d, tn), lambda l, j: (l, 0, j)),
                  pl.BlockSpec((1, 1, tn), lambda l, j: (l, 0, j))],
        out_specs=pl.BlockSpec((1, r, tn), lambda l, j: (l, 0, j)),
        out_shape=jax.ShapeDtypeStruct((depth, r, n3), F32),
        compiler_params=_params("parallel", "parallel"),
        name="adaln_mod",
    )(cond, w_ada, b_ada.reshape(depth, 1, n3))


def _modulated_norm(h, g, mod, d):
    y = h * lax.rsqrt(jnp.mean(h * h, axis=-1, keepdims=True) + EPS) * g
    return y * (1.0 + mod[:, d:2 * d]) + mod[:, 0:d]


def _norm_kernel(x_ref, ctx_ref, g_ref, mod_ref, h_ref, n_ref, *, tiles_lat):
    d = x_ref.shape[-1]

    def emit(src_ref):
        h = src_ref[0]
        h_ref[0] = h
        n_ref[0] = _modulated_norm(h, g_ref[...], mod_ref[0], d).astype(n_ref.dtype)

    is_lat = pl.program_id(1) < tiles_lat
    pl.when(is_lat)(lambda: emit(x_ref))
    pl.when(jnp.logical_not(is_lat))(lambda: emit(ctx_ref))


def _mod_index(tiles_lat):
    return lambda b, j: (2 * b + jnp.where(j >= tiles_lat, 1, 0), 0, 0)


def _join_norm_mod(x, ctx, g, mod, tm=256):
    b, t_lat, d = x.shape
    tt = t_lat + ctx.shape[1]
    tl = t_lat // tm
    tok = pl.BlockSpec((1, tm, d), lambda i, j: (i, j, 0))
    return pl.pallas_call(
        functools.partial(_norm_kernel, tiles_lat=tl),
        grid=(b, tt // tm),
        in_specs=[pl.BlockSpec((1, tm, d), lambda i, j: (i, jnp.minimum(j, tl - 1), 0)),
                  pl.BlockSpec((1, tm, d), lambda i, j: (i, jnp.maximum(j - tl, 0), 0)),
                  pl.BlockSpec((1, d), lambda i, j: (0, 0)),
                  pl.BlockSpec((1, 1, 3 * d), _mod_index(tl))],
        out_specs=(tok, tok),
        out_shape=(jax.ShapeDtypeStruct((b, tt, d), F32), jax.ShapeDtypeStruct((b, tt, d), BF16)),
        compiler_params=_params("parallel", "arbitrary"),
        name="join_norm_mod",
    )(x, ctx, g, mod)


def _matmul_bias_kernel(x_ref, w_ref, b_ref, o_ref):
    o_ref[...] = (_dot(x_ref[...], w_ref[...]) + b_ref[...]).astype(o_ref.dtype)


def _tile(m, pref, unit=128):
    t = min(pref, m) // unit * unit
    while m % t:
        t -= unit
    return t


def _in_proj(n2d, w, bias, tm, tn, name, out_dtype=F32):
    m, d = n2d.shape
    n = w.shape[1]
    tm = _tile(m, tm)
    return pl.pallas_call(
        _matmul_bias_kernel,
        grid=(n // tn, m // tm),
        in_specs=[pl.BlockSpec((tm, d), lambda j, i: (i, 0)),
                  pl.BlockSpec((d, tn), lambda j, i: (0, j)),
                  pl.BlockSpec((1, tn), lambda j, i: (0, j))],
        out_specs=pl.BlockSpec((tm, tn), lambda j, i: (i, j)),
        out_shape=jax.ShapeDtypeStruct((m, n), out_dtype),
        compiler_params=_params("parallel", "parallel"),
        name=name,
    )(n2d, w, bias)


def _gates_kernel(x_ref, w_ref, b_ref, o_ref):
    o_ref[0] = _dot_nt(w_ref[...], x_ref[0]) + b_ref[...]


def _in_proj_gates_rows(n, w_t, bias_col):
    b, tt, d = n.shape
    ng = w_t.shape[0]
    tm = _tile(tt, 2560)
    return pl.pallas_call(
        _gates_kernel,
        grid=(b, tt // tm),
        in_specs=[pl.BlockSpec((1, tm, d), lambda i, j: (i, j, 0)),
                  pl.BlockSpec((ng, d), lambda i, j: (0, 0)),
                  pl.BlockSpec((ng, 1), lambda i, j: (0, 0))],
        out_specs=pl.BlockSpec((1, ng, tm), lambda i, j: (i, 0, j)),
        out_shape=jax.ShapeDtypeStruct((b, ng, tt), F32),
        compiler_params=_params("parallel", "parallel"),
        name="in_proj_gates",
    )(n, w_t, bias_col)


def _chunk_mlp_kernel(u_ref, v_ref, z_ref, ws_ref, bs_ref, y_ref):
    tm, width = v_ref.shape
    gd = width // A_GROUPS
    for c in range(tm // A_CHUNK):
        rows = slice(c * A_CHUNK, (c + 1) * A_CHUNK)
        for g in range(A_GROUPS):
            cols = slice(g * gd, (g + 1) * gd)
            v = v_ref[rows, cols].astype(F32)
            mu = jnp.mean(v, axis=-1, keepdims=True)
            vc = v - mu
            var = jnp.mean(vc * vc, axis=-1, keepdims=True)
            vn = (vc * lax.rsqrt(var + EPS)).astype(BF16)
            mixed = _dot(ws_ref[g], vn) + bs_ref[:, g:g + 1]
            gate = _silu(z_ref[rows, cols].astype(F32))
            y_ref[rows, cols] = (u_ref[rows, cols].astype(F32) * mixed * gate).astype(y_ref.dtype)


def _chunk_mlp(p2d, ws, bs_t, col_u, width, tm=1024):
    m = p2d.shape[0]
    tm = _tile(m, tm)
    cb = col_u // width
    spec = lambda k: pl.BlockSpec((tm, width), lambda i, k=k: (i, cb + k))
    return pl.pallas_call(
        _chunk_mlp_kernel,
        grid=(m // tm,),
        in_specs=[spec(0), spec(1), spec(2),
                  pl.BlockSpec(ws.shape, lambda i: (0, 0, 0)),
                  pl.BlockSpec(bs_t.shape, lambda i: (0, 0))],
        out_specs=pl.BlockSpec((tm, width), lambda i: (i, 0)),
        out_shape=jax.ShapeDtypeStruct((m, width), BF16),
        compiler_params=_params("parallel"),
        name="chunk_mlp",
    )(p2d, p2d, p2d, ws, bs_t)


def _conv_kernel(x_ref, w_ref, o_ref, *, t_lat, scale, transpose):
    x = x_ref[0].astype(F32)
    tt = x.shape[0]
    rows = lax.broadcasted_iota(jnp.int32, x.shape, 0)
    prev = jnp.where((rows == 0) | (rows == t_lat), 0.0, pltpu.roll(x, 1, axis=0))
    nxt = jnp.where((rows == t_lat - 1) | (rows == tt - 1), 0.0, pltpu.roll(x, tt - 1, axis=0))
    w = w_ref[...]
    y = _silu(w[0:1] * prev + w[1:2] * x + w[2:3] * nxt) * scale
    o_ref[0] = (y.T if transpose else y).astype(o_ref.dtype)


def _conv_silu(p3d, conv_w, t_lat, col0, wcol0, width, scale, transpose, tc=256):
    b, tt, _ = p3d.shape
    kern = functools.partial(_conv_kernel, t_lat=t_lat, scale=scale, transpose=transpose)
    if transpose:
        out_spec = pl.BlockSpec((1, tc, tt), lambda i, j: (i, j, 0))
        out_shape = jax.ShapeDtypeStruct((b, width, tt), BF16)
    else:
        out_spec = pl.BlockSpec((1, tt, tc), lambda i, j: (i, 0, j))
        out_shape = jax.ShapeDtypeStruct((b, tt, width), BF16)
    return pl.pallas_call(
        kern,
        grid=(b, width // tc),
        in_specs=[pl.BlockSpec((1, tt, tc), lambda i, j: (i, 0, col0 // tc + j)),
                  pl.BlockSpec((CONV_W, tc), lambda i, j: (0, wcol0 // tc + j))],
        out_specs=out_spec,
        out_shape=out_shape,
        compiler_params=_params("parallel", "parallel"),
        name="conv_k_t" if transpose else "conv_q",
    )(p3d, conv_w)


def _mlstm_kernel(*refs, reverse, readout, hd, gb):
    units = [(g, h) for g in range(gb) for h in range(B_HEADS)]
    nu = len(units)
    refs, c_scr, m_scr = refs[:-2 * nu], refs[-2 * nu:-nu], refs[-nu:]
    if readout:
        q_ref, kt_ref, v_ref, gr_ref, hf_ref, o_ref, z_ref, g_ref, out_ref = refs
    else:
        q_ref, kt_ref, v_ref, gr_ref, out_ref = refs

    @pl.when(pl.program_id(1) == 0)
    def _():
        for scr in c_scr + m_scr:
            scr[...] = jnp.zeros_like(scr)

    L = B_CHUNK
    row = lax.broadcasted_iota(jnp.int32, (L, L), 0)
    col = lax.broadcasted_iota(jnp.int32, (L, L), 1)
    seen = (col >= row) if reverse else (col <= row)
    tri_t = jnp.where((row >= col) if reverse else (row <= col), 1.0, 0.0).astype(BF16)
    last = 0 if reverse else L - 1
    gi, gf = (2 * B_HEADS, 3 * B_HEADS) if reverse else (0, B_HEADS)
    cols = [slice(h * hd, (h + 1) * hd) for h in range(B_HEADS)]
    ones_blk = jnp.ones((L, MLSTM_AUG), BF16)
    lane_tile = lambda x, width: jnp.concatenate([x] * (width // MLSTM_AUG), axis=1)
    gates = lambda lo_row: jnp.concatenate([gr_ref[g, lo_row:lo_row + B_HEADS, :] for g in range(gb)], axis=0)

    i_rows = gates(gi)
    f_rows = _log_sigmoid(gates(gf))
    hi = f_rows.astype(BF16).astype(F32)
    r1 = f_rows - hi
    mid = r1.astype(BF16).astype(F32)
    lo = (r1 - mid).astype(BF16).astype(F32)
    terms = jnp.concatenate([hi, mid, lo, jnp.zeros_like(hi)], axis=0).astype(BF16)
    sums = _dot(terms, tri_t)
    cb_rows = sums[0:nu] + sums[nu:2 * nu] + sums[2 * nu:3 * nu]
    tri = jnp.where(seen, 1.0, 0.0).astype(BF16)
    rep = lambda x, u: jnp.broadcast_to(x[u:u + 1], (MLSTM_AUG, L))
    cb_cols = []
    for u in range(nu):
        c3 = _dot_nt(tri, jnp.concatenate([rep(hi, u), rep(mid, u), rep(lo, u)], axis=0).astype(BF16))
        cb_cols.append(c3[:, 0:MLSTM_AUG] + c3[:, MLSTM_AUG:2 * MLSTM_AUG] + c3[:, 2 * MLSTM_AUG:])

    s_bf, e_inter, emt, ws, ec, m_new = [], [], [], [], [], []
    for u, (g, h) in enumerate(units):
        i_row, cb_row = i_rows[u:u + 1], cb_rows[u:u + 1]
        dmat = jnp.where(seen, lane_tile(cb_cols[u], L) - cb_row + i_row, NEG_BIG)
        m = m_scr[u][0:1, 0:1]
        inter = cb_cols[u] + m
        mt = jnp.maximum(inter, jnp.max(dmat, axis=1, keepdims=True))
        decay = jnp.exp(dmat - lane_tile(mt, L))
        s_bf.append((_dot(q_ref[g, :, cols[h]], kt_ref[g, cols[h], :]) * decay).astype(BF16))
        e_inter.append(jnp.exp(inter - mt))
        emt.append(jnp.exp(-mt))
        cl = cb_row[:, last:last + 1]
        w_log = cl - cb_row + i_row
        m_new.append(jnp.maximum(cl + m, jnp.max(w_log, axis=1, keepdims=True)))
        ec.append(jnp.exp(cl + m - m_new[u]))
        ws.append(jnp.exp(w_log - m_new[u]))

    sv, qc, upd = [], [], []
    for u, (g, h) in enumerate(units):
        v_aug = jnp.concatenate([v_ref[g, :, cols[h]].astype(BF16), ones_blk], axis=1)
        sv.append(_dot(s_bf[u], v_aug))
        qc.append(_dot(q_ref[g, :, cols[h]], c_scr[u][...].astype(BF16)))
        kw_t = (kt_ref[g, cols[h], :].astype(F32) * ws[u]).astype(BF16)
        upd.append(_dot(kw_t, v_aug))
    for u, (g, h) in enumerate(units):
        tot = sv[u] + lane_tile(e_inter[u], hd + MLSTM_AUG) * qc[u]
        inv = 1.0 / jnp.maximum(jnp.abs(tot[:, hd:hd + MLSTM_AUG]), emt[u])
        hc = jnp.concatenate([tot[:, c0:c0 + MLSTM_AUG] * inv for c0 in range(0, hd, MLSTM_AUG)], axis=1)
        c_scr[u][...] = ec[u] * c_scr[u][...] + upd[u]
        m_scr[u][...] = jnp.broadcast_to(m_new[u], m_scr[u].shape)
        if readout:
            hs = hf_ref[g, :, cols[h]] + hc
            hn = hs * lax.rsqrt(jnp.mean(hs * hs, axis=-1, keepdims=True) + EPS) * g_ref[:, cols[h]]
            y = hn * _sigmoid(o_ref[g, :, cols[h]].astype(F32)) * _silu(z_ref[g, :, cols[h]].astype(F32))
            out_ref[g, :, cols[h]] = y.astype(out_ref.dtype)
        else:
            out_ref[g, :, cols[h]] = hc.astype(out_ref.dtype)


def _mlstm_dir(q, k_t, p3d, gates_row, t_lat, cols, reverse, hf=None, norm_g=None, gb=2):
    b, tt, width = q.shape
    gb = max(g for g in range(1, gb + 1) if b % g == 0)
    hd = width // B_HEADS
    L = B_CHUNK
    nch, nlat = tt // L, t_lat // L
    if reverse:
        chunk = lambda i: nch - 1 - i
    else:
        chunk = lambda i: lax.rem(i + nlat, nch)
    wb = lambda name: cols[name] // width
    tok = lambda cb: pl.BlockSpec((gb, L, width), lambda bi, i, cb=cb: (bi, chunk(i), cb))
    in_specs = [tok(0), pl.BlockSpec((gb, width, L), lambda bi, i: (bi, 0, chunk(i))), tok(wb("b_v")),
                pl.BlockSpec((gb, gates_row.shape[1], L), lambda bi, i: (bi, 0, chunk(i)))]
    args = [q, k_t, p3d, gates_row]
    readout = hf is not None
    if readout:
        in_specs += [tok(0), tok(wb("b_o")), tok(wb("b_z")), pl.BlockSpec((1, width), lambda bi, i: (0, 0))]
        args += [hf, p3d, p3d, norm_g]
    kern = functools.partial(_mlstm_kernel, reverse=reverse, readout=readout, hd=hd, gb=gb)
    nu = gb * B_HEADS
    return pl.pallas_call(
        kern,
        grid=(b // gb, nch),
        in_specs=in_specs,
        out_specs=tok(0),
        out_shape=jax.ShapeDtypeStruct((b, tt, width), BF16),
        scratch_shapes=[pltpu.VMEM((hd, hd + MLSTM_AUG), F32)] * nu + [pltpu.VMEM((8, 128), F32)] * nu,
        compiler_params=_params("parallel", "arbitrary"),
        name="mlstm_bwd_readout" if readout else "mlstm_fwd",
    )(*args)


def _hgrn_kernel(*refs, reverse, readout, dv, gb):
    if readout:
        q_ref, z_ref, v_ref, lbf_ref, oml_ref, e_ref, of_ref, cg_ref, g_ref, out_ref, s_scr, row_scr = refs
    else:
        q_ref, z_ref, v_ref, lbf_ref, oml_ref, e_ref, out_ref, s_scr, row_scr = refs

    @pl.when(pl.program_id(1) == 0)
    def _():
        s_scr[...] = jnp.zeros_like(s_scr)

    L, dk, nb = C_CHUNK, C_KEY_DIM, C_CHUNK // C_SUB
    row = lax.broadcasted_iota(jnp.int32, (L, L), 0)
    col = lax.broadcasted_iota(jnp.int32, (L, L), 1)
    rb, cbk = row // C_SUB, col // C_SUB
    if reverse:
        seen, blk_before, last = col >= row, cbk > rb, 0
    else:
        seen, blk_before, last = col <= row, cbk < rb, L - 1
    sum_mat = jnp.concatenate([jnp.where(seen, 1.0, 0.0), jnp.where(blk_before, 1.0, 0.0)], axis=0).astype(BF16)
    same_blk = rb == cbk
    sub = lax.broadcasted_iota(jnp.int32, (L, dk), 0) % C_SUB
    pair_ok = [(sub <= j) if reverse else (sub >= j) for j in range(C_SUB)]

    def bcast_sub(ref, j):
        return jnp.concatenate([jnp.broadcast_to(ref[i * C_SUB + j:i * C_SUB + j + 1, :], (C_SUB, dk))
                                for i in range(nb)], axis=0)

    units = [(g, h) for g in range(gb) for h in range(C_HEADS)]
    kcs = [slice(h * dk, (h + 1) * dk) for h in range(C_HEADS)]
    vcs = [slice(h * dv, (h + 1) * dv) for h in range(C_HEADS)]

    q, k, v, c3 = [], [], [], []
    for u, (g, h) in enumerate(units):
        q.append(_silu(q_ref[g, :, kcs[h]].astype(F32)))
        z = z_ref[g, :, kcs[h]]
        v.append(v_ref[g, :, vcs[h]].astype(BF16))
        a = jnp.exp(-jnp.abs(z))
        r = 1.0 / (1.0 + a)
        pos = z >= 0.0
        oml = oml_ref[:, kcs[h]]
        f = lbf_ref[:, kcs[h]] + oml * jnp.where(pos, r, a * r)
        k.append(oml * jnp.where(pos, a * r, r))
        lf = jnp.log2(f)
        hi = lf.astype(BF16)
        r1 = lf - hi.astype(F32)
        mid = r1.astype(BF16)
        lo = (r1 - mid.astype(F32)).astype(BF16)
        c3.append(_dot(sum_mat, jnp.concatenate([hi, mid, lo], axis=1)))

    cb, a_off, a_diag = [], [], []
    for u, (g, h) in enumerate(units):
        c1 = c3[u][:, 0:dk] + c3[u][:, dk:2 * dk] + c3[u][:, 2 * dk:3 * dk]
        cbh, entry = c1[0:L], c1[L:2 * L]
        cb.append(cbh)
        qd = (q[u] * jnp.exp2(cbh - entry)).astype(BF16)
        parts = []
        for i in range(nb):
            lo_r, hi_r = ((i + 1) * C_SUB, L) if reverse else (0, i * C_SUB)
            if hi_r == lo_r:
                parts.append(jnp.zeros((C_SUB, L), F32))
                continue
            ent = entry[i * C_SUB:i * C_SUB + 1, :]
            kd = (k[u][lo_r:hi_r] * jnp.exp2(ent - cbh[lo_r:hi_r])).astype(BF16)
            pad = [jnp.zeros((n, dk), BF16) for n in (lo_r, L - hi_r)]
            kd = jnp.concatenate([p for p in (pad[0], kd, pad[1]) if p.shape[0]], axis=0)
            parts.append(_dot_nt(qd[i * C_SUB:(i + 1) * C_SUB], kd))
        a_off.append(jnp.concatenate(parts, axis=0))
        qk_parts = []
        cb_rows, k_rows = row_scr.at[2 * u], row_scr.at[2 * u + 1]
        cb_rows[...] = cbh
        k_rows[...] = k[u]
        for j in range(C_SUB):
            dec = jnp.exp2(jnp.where(pair_ok[j], cbh - bcast_sub(cb_rows, j), NEG_BIG))
            qk_parts.append((q[u] * bcast_sub(k_rows, j) * dec).astype(BF16))
        a_diag.append(_dot(jnp.concatenate(qk_parts, axis=1), e_ref[...]))

    for u, (g, h) in enumerate(units):
        scores = a_off[u] + jnp.where(same_blk, a_diag[u], 0.0)
        st = s_scr[u]
        o = _dot(scores.astype(BF16), v[u]) + _dot_nt((q[u] * jnp.exp2(cb[u])).astype(BF16), st.astype(BF16))
        cl = cb[u][last:last + 1, :]
        kdec = (k[u] * jnp.exp2(cl - cb[u])).astype(BF16)
        s_scr[u] = st * jnp.exp2(cl) + _dot_tn(v[u], kdec)
        if readout:
            os_ = of_ref[g, :, vcs[h]] + o
            on = os_ * lax.rsqrt(jnp.mean(os_ * os_, axis=-1, keepdims=True) + EPS) * g_ref[:, vcs[h]]
            out_ref[g, :, vcs[h]] = (on * _silu(cg_ref[g, :, vcs[h]].astype(F32))).astype(out_ref.dtype)
        else:
            out_ref[g, :, vcs[h]] = o.astype(out_ref.dtype)


def _hgrn_dir(pcq, pcf, lbf, oml, e_mat, cols, t_lat, reverse, of=None, norm_g=None, gb=8):
    b, tt, _ = pcq.shape
    kw, L = C_HEADS * C_KEY_DIM, C_CHUNK
    vw = cols["c_g"] - cols["c_i"]
    dv = vw // C_HEADS
    nch, nlat = tt // L, t_lat // L
    if reverse:
        chunk = lambda i: nch - 1 - i
    else:
        chunk = lambda i: lax.rem(i + nlat, nch)
    gb = max(g for g in range(1, gb + 1) if b % g == 0)
    tok = lambda c0, w: pl.BlockSpec((gb, L, w), lambda bi, i: (bi, chunk(i), c0 // w))
    const2 = lambda x: pl.BlockSpec(x.shape, lambda bi, i: (0, 0))
    f_name = "c_f_bwd" if reverse else "c_f_fwd"
    in_specs = [tok(cols["c_q"], kw), tok(cols[f_name], kw), tok(cols["c_i"], vw),
                const2(lbf), const2(oml), const2(e_mat)]
    args = [pcq, pcf, pcq, lbf, oml, e_mat]
    readout = of is not None
    if readout:
        in_specs += [tok(0, vw), tok(cols["c_g"], vw), const2(norm_g)]
        args += [of, pcq, norm_g]
    kern = functools.partial(_hgrn_kernel, reverse=reverse, readout=readout, dv=dv, gb=gb)
    return pl.pallas_call(
        kern,
        grid=(b // gb, nch),
        in_specs=in_specs,
        out_specs=tok(0, vw),
        out_shape=jax.ShapeDtypeStruct((b, tt, vw), BF16),
        scratch_shapes=[pltpu.VMEM((gb * C_HEADS, dv, C_KEY_DIM), F32),
                        pltpu.VMEM((2 * gb * C_HEADS, L, C_KEY_DIM), F32)],
        compiler_params=_params("parallel", "arbitrary"),
        name="hgrn_bwd_readout" if readout else "hgrn_fwd",
    )(*args)


def _out_kernel(ya_ref, yb_ref, yc_ref, w_ref, h_ref, mod_ref, *rest, final, wa, wb):
    d = h_ref.shape[-1]
    w = w_ref
    y = (_dot(ya_ref[0], w[0:wa, :]) + _dot(yb_ref[0], w[wa:wa + wb, :]) + _dot(yc_ref[0], w[wa + wb:, :]))
    h_new = h_ref[0] + mod_ref[0][:, 2 * d:3 * d] * y
    if final:
        g_ref, out_ref = rest
        out_ref[0] = h_new * lax.rsqrt(jnp.mean(h_new * h_new, axis=-1, keepdims=True) + EPS) * g_ref[...]
    else:
        g_ref, modn_ref, h_out_ref, n_ref = rest
        h_out_ref[0] = h_new
        n_ref[0] = _modulated_norm(h_new, g_ref[...], modn_ref[0], d).astype(n_ref.dtype)


def _out_proj(ya, yb, yc, w_out, h, mod, g_next, mod_next, t_lat, final, tm=256):
    b, tt, d = h.shape
    wa, wb, wc = ya.shape[-1], yb.shape[-1], yc.shape[-1]
    if final:
        tm = _tile(t_lat, 512)
    tok = lambda w: pl.BlockSpec((1, tm, w), lambda i, j: (i, j, 0))
    mod_spec = pl.BlockSpec((1, 1, 3 * d), _mod_index(t_lat // tm))
    in_specs = [tok(wa), tok(wb), tok(wc), pl.BlockSpec(w_out.shape, lambda i, j: (0, 0)), tok(d), mod_spec,
                pl.BlockSpec((1, d), lambda i, j: (0, 0))]
    args = [ya, yb, yc, w_out, h, mod, g_next]
    kern = functools.partial(_out_kernel, final=final, wa=wa, wb=wb)
    if final:
        return pl.pallas_call(
            kern, grid=(b, t_lat // tm), in_specs=in_specs, out_specs=tok(d),
            out_shape=jax.ShapeDtypeStruct((b, t_lat, d), F32),
            compiler_params=_params("parallel", "parallel"), name="out_proj_final",
        )(*args)
    return pl.pallas_call(
        kern, grid=(b, tt // tm), in_specs=in_specs + [mod_spec], out_specs=(tok(d), tok(d)),
        out_shape=(jax.ShapeDtypeStruct((b, tt, d), F32), jax.ShapeDtypeStruct((b, tt, d), BF16)),
        compiler_params=_params("parallel", "parallel"), name="out_proj",
    )(*args, mod_next)


def _packed_layout(d):
    a, bw, c, kq = d // 4, d // 2, d // 4, C_HEADS * C_KEY_DIM
    ref_order = (("a_u", a), ("a_v", a), ("a_z", a), ("b_q", bw), ("b_k", bw), ("b_v", bw), ("b_o", bw),
                 ("b_z", bw), ("gates", 4 * B_HEADS), ("c_q", kq), ("c_f_fwd", kq), ("c_f_bwd", kq),
                 ("c_i", c), ("c_g", c))
    src, start = {}, 0
    for name, w in ref_order:
        src[name] = (start, w)
        start += w
    groups = {"ab": ("b_q", "b_k", "b_v", "b_o", "b_z", "a_u", "a_v", "a_z"),
              "cq": ("c_q", "c_i", "c_g"), "cf": ("c_f_fwd", "c_f_bwd")}
    cols = {}
    for names in groups.values():
        pos = 0
        for name in names:
            cols[name] = pos
            pos += src[name][1]
    return src, groups, cols


def _pack_cols(w, bias, src, names):
    spans = []
    for k in names:
        lo, width = src[k]
        if spans and spans[-1][1] == lo:
            spans[-1][1] = lo + width
        else:
            spans.append([lo, lo + width])
    pick = lambda a: jnp.concatenate([a[..., lo:hi] for lo, hi in spans], axis=-1)
    return pick(w).astype(BF16), pick(bias)[None, :]


BF16_ROWS = 16
LANES = 128


def _grid_transpose_kernel(x_ref, o_ref, scr, *, width):
    lanes = scr.shape[-1]
    pitch = scr.shape[1] // BF16_ROWS
    for c in range(scr.shape[0]):
        for r in range(BF16_ROWS):
            scr[c, r * pitch:r * pitch + width, :] = x_ref[0, r * width:(r + 1) * width,
                                                           c * lanes:(c + 1) * lanes].astype(F32)
    for w in range(width):
        rows = [scr[c, pl.ds(w, BF16_ROWS, stride=pitch), :] for c in range(scr.shape[0])]
        o_ref[0, w] = jnp.concatenate(rows, axis=1).astype(o_ref.dtype)


def _copy_kernel(x_ref, dst_ref, o_ref):
    del dst_ref
    o_ref[...] = x_ref[...]


def _grid_transpose(x, t_lat, rows, width):
    b, tt, f = x.shape
    t_ctx = tt - t_lat
    assert rows % BF16_ROWS == 0 and tt % rows == 0 and t_lat % t_ctx == 0
    tile = BF16_ROWS * width
    lat = pl.pallas_call(
        functools.partial(_grid_transpose_kernel, width=width),
        grid=(b, rows // BF16_ROWS),
        in_specs=[pl.BlockSpec((1, tile, f), lambda i, j: (i, j, 0))],
        out_specs=pl.BlockSpec((1, width, BF16_ROWS, f), lambda i, j: (i, 0, j, 0)),
        out_shape=jax.ShapeDtypeStruct((b, tt // rows, rows, f), x.dtype),
        scratch_shapes=[pltpu.VMEM((f // LANES, BF16_ROWS * (width + 8), LANES), F32)],
        compiler_params=_params("parallel", "parallel"),
        name="grid_transpose",
    )(x).reshape(b, tt, f)
    ctx_spec = pl.BlockSpec((1, t_ctx, f), lambda i: (i, t_lat // t_ctx, 0))
    return pl.pallas_call(
        _copy_kernel,
        grid=(b,),
        in_specs=[ctx_spec, pl.BlockSpec(memory_space=pl.ANY)],
        out_specs=ctx_spec,
        out_shape=jax.ShapeDtypeStruct((b, tt, f), x.dtype),
        input_output_aliases={1: 0},
        compiler_params=_params("parallel"),
        name="context_rows_copy",
    )(x, lat)


def kernel(x, c, ctx, c_ctx, w_ada, b_ada, norm_g, w_in, b_in, w_spatial, b_spatial, conv_qk, mlstm_norm,
           hgrn_lb_logits, hgrn_norm, w_out, final_norm):
    b, t_lat, d = x.shape
    t_ctx = ctx.shape[1]
    tt = t_lat + t_ctx
    depth = w_ada.shape[0]
    src, groups, cols = _packed_layout(d)
    a_width, b_width = d // 4, d // 2
    rows = t_lat // GRID_W

    r_pad = -(-(b + 1) // 8) * 8
    cond = jnp.concatenate([c, c_ctx[None, :], jnp.zeros((r_pad - b - 1, d), F32)], axis=0)
    mod_all = _ada_mod(cond, w_ada, b_ada)
    mods = [jnp.stack([mod_all[l, :b], jnp.broadcast_to(mod_all[l, b], (b, 3 * d))], axis=1).reshape(2 * b, 1, 3 * d)
            for l in range(depth)]

    lbf, oml = _lower_bounds(hgrn_lb_logits.astype(F32))

    e_rows = lax.broadcasted_iota(jnp.int32, (C_SUB * C_KEY_DIM, C_CHUNK), 0) // C_KEY_DIM
    e_cols = lax.broadcasted_iota(jnp.int32, (C_SUB * C_KEY_DIM, C_CHUNK), 1) % C_SUB
    e_mat = (e_rows == e_cols).astype(BF16)

    h, n = _join_norm_mod(x, ctx, norm_g[0:1], mods[0])
    out = None
    for l in range(depth):
        last = l == depth - 1
        w_ab, b_ab = _pack_cols(w_in[l], b_in[l], src, groups["ab"])
        w_cq, b_cq = _pack_cols(w_in[l], b_in[l], src, groups["cq"])
        w_cf, b_cf = _pack_cols(w_in[l], b_in[l], src, groups["cf"])
        g0, gw = src["gates"]
        w_gate_t = w_in[l][:, g0:g0 + gw].T.astype(BF16)
        b_gate = b_in[l][g0:g0 + gw][:, None]

        n_cm = _grid_transpose(n, t_lat, rows, GRID_W)
        n2d, n_cm2d = n.reshape(b * tt, d), n_cm.reshape(b * tt, d)
        p2d = _in_proj(n2d, w_ab, b_ab, 2048, 1664, "in_proj_ab", BF16)
        pcq = _in_proj(n_cm2d, w_cq, b_cq, 1024, 1536, "in_proj_cq", BF16).reshape(b, tt, -1)
        pcf = _in_proj(n_cm2d, w_cf, b_cf, 1024, 1024, "in_proj_cf").reshape(b, tt, -1)
        gates_row = _in_proj_gates_rows(n, w_gate_t, b_gate)
        p3d = p2d.reshape(b, tt, -1)

        ya = _chunk_mlp(p2d, w_spatial[l].astype(BF16), b_spatial[l].T, cols["a_u"], a_width).reshape(b, tt, a_width)

        q = _conv_silu(p3d, conv_qk[l], t_lat, cols["b_q"], 0, b_width, 1.0, transpose=False)
        k_t = _conv_silu(p3d, conv_qk[l], t_lat, cols["b_k"], b_width, b_width,
                         (b_width // B_HEADS) ** -0.5, transpose=True)
        hf = _mlstm_dir(q, k_t, p3d, gates_row, t_lat, cols, reverse=False)
        yb = _mlstm_dir(q, k_t, p3d, gates_row, t_lat, cols, reverse=True, hf=hf, norm_g=mlstm_norm[l:l + 1])

        lbf_l, oml_l = lbf[l:l + 1], oml[l:l + 1]
        of = _hgrn_dir(pcq, pcf, lbf_l, oml_l, e_mat, cols, t_lat, False)
        yc_cm = _hgrn_dir(pcq, pcf, lbf_l, oml_l, e_mat, cols, t_lat, True, of=of, norm_g=hgrn_norm[l:l + 1])
        yc = _grid_transpose(yc_cm, t_lat, GRID_W, rows)

        w_o = w_out[l].astype(BF16)
        if last:
            out = _out_proj(ya, yb, yc, w_o, h, mods[l], final_norm[None, :], None, t_lat, final=True)
        else:
            h, n = _out_proj(ya, yb, yc, w_o, h, mods[l], norm_g[l + 1:l + 2], mods[l + 1], t_lat, final=False)
    return out
```

```python
import functools

import jax
import jax.numpy as jnp
from jax import lax
from jax.experimental import pallas as pl
from jax.experimental.pallas import tpu as pltpu

EPS = 1e-6
NEG_BIG = -1e30
LB_FLOOR = 1e-30
GRID_W = 64
CONV_W = 3

A_GROUPS = 4
A_CHUNK = 128
B_HEADS = 4
B_CHUNK = 256
C_HEADS = 4
C_KEY_DIM = 128
C_CHUNK = 64
C_SUB = 8
GATE_PAD = 128
HGRN_SAFE_LOG2 = 100.0
MLSTM_AUG = 128

V7X_VMEM_LIMIT = 56 * 1024 * 1024

F32 = jnp.float32
BF16 = jnp.bfloat16


def _params(*sem):
    return pltpu.CompilerParams(dimension_semantics=sem, vmem_limit_bytes=V7X_VMEM_LIMIT)


def _sigmoid(x):
    return 1.0 / (1.0 + jnp.exp(-x))


def _silu(x):
    return x * _sigmoid(x)


def _log_sigmoid(x):
    return jnp.minimum(x, 0.0) - jnp.log1p(jnp.exp(-jnp.abs(x)))


def _dot(a, b):
    return jnp.dot(a, b, preferred_element_type=F32)


def _dot_nt(a, b):
    return lax.dot_general(a, b, (((1,), (1,)), ((), ())), preferred_element_type=F32)


def _dot_tn(a, b):
    return lax.dot_general(a, b, (((0,), (0,)), ((), ())), preferred_element_type=F32)


def _lb_kernel(x_ref, lbf_ref, oml_ref):
    x = x_ref[...]
    depth = x.shape[0]
    e = jnp.exp(x - jnp.max(x, axis=0, keepdims=True))
    p = e / jnp.sum(e, axis=0, keepdims=True)
    rows = lax.broadcasted_iota(jnp.int32, x.shape, 0)
    lb = jnp.zeros_like(x)
    for j in range(1, depth):
        lb = lb + jnp.where(rows >= j, p[j:j + 1, :], 0.0)
    lbf_ref[...] = jnp.maximum(lb, LB_FLOOR)
    oml_ref[...] = 1.0 - lb


def _lower_bounds(logits):
    shp = jax.ShapeDtypeStruct(logits.shape, F32)
    return pl.pallas_call(_lb_kernel, out_shape=(shp, shp), name="hgrn_lower_bounds")(logits)


def _ada_kernel(c_ref, w_ref, b_ref, o_ref):
    s = _silu(c_ref[...]).astype(BF16)
    o_ref[0] = _dot(s, w_ref[0].astype(BF16)) + b_ref[0]


def _ada_mod(cond, w_ada, b_ada, tn=768):
    depth, d, n3 = w_ada.shape
    r = cond.shape[0]
    return pl.pallas_call(
        _ada_kernel,
        grid=(depth, n3 // tn),
        in_specs=[pl.BlockSpec((r, d), lambda l, j: (0, 0)),
                  pl.BlockSpec((1, d, tn), lambda l, j: (l, 0, j)),
                  pl.BlockSpec((1, 1, tn), lambda l, j: (l, 0, j))],
        out_specs=pl.BlockSpec((1, r, tn), lambda l, j: (l, 0, j)),
        out_shape=jax.ShapeDtypeStruct((depth, r, n3), F32),
        compiler_params=_params("parallel", "parallel"),
        name="adaln_mod",
    )(cond, w_ada, b_ada.reshape(depth, 1, n3))


def _modulated_norm(h, g, mod, d):
    y = h * lax.rsqrt(jnp.mean(h * h, axis=-1, keepdims=True) + EPS) * g
    return y * (1.0 + mod[:, d:2 * d]) + mod[:, 0:d]


def _norm_kernel(x_ref, ctx_ref, g_ref, mod_ref, h_ref, n_ref, *, tiles_lat):
    d = x_ref.shape[-1]

    def emit(src_ref):
        h = src_ref[0]
        h_ref[0] = h
        n_ref[0] = _modulated_norm(h, g_ref[...], mod_ref[0], d).astype(n_ref.dtype)

    is_lat = pl.program_id(1) < tiles_lat
    pl.when(is_lat)(lambda: emit(x_ref))
    pl.when(jnp.logical_not(is_lat))(lambda: emit(ctx_ref))


def _mod_index(tiles_lat):
    return lambda b, j: (2 * b + jnp.where(j >= tiles_lat, 1, 0), 0, 0)


def _join_norm_mod(x, ctx, g, mod, tm=256):
    b, t_lat, d = x.shape
    tt = t_lat + ctx.shape[1]
    tl = t_lat // tm
    tok = pl.BlockSpec((1, tm, d), lambda i, j: (i, j, 0))
    return pl.pallas_call(
        functools.partial(_norm_kernel, tiles_lat=tl),
        grid=(b, tt // tm),
        in_specs=[pl.BlockSpec((1, tm, d), lambda i, j: (i, jnp.minimum(j, tl - 1), 0)),
                  pl.BlockSpec((1, tm, d), lambda i, j: (i, jnp.maximum(j - tl, 0), 0)),
                  pl.BlockSpec((1, d), lambda i, j: (0, 0)),
                  pl.BlockSpec((1, 1, 3 * d), _mod_index(tl))],
        out_specs=(tok, tok),
        out_shape=(jax.ShapeDtypeStruct((b, tt, d), F32), jax.ShapeDtypeStruct((b, tt, d), BF16)),
        compiler_params=_params("parallel", "arbitrary"),
        name="join_norm_mod",
    )(x, ctx, g, mod)


def _matmul_bias_kernel(x_ref, w_ref, b_ref, o_ref):
    o_ref[...] = (_dot(x_ref[...], w_ref[...]) + b_ref[...]).astype(o_ref.dtype)


def _tile(m, pref, unit=128):
    t = min(pref, m) // unit * unit
    while m % t:
        t -= unit
    return t


def _in_proj(n2d, w, bias, tm, tn, name, out_dtype=F32):
    m, d = n2d.shape
    n = w.shape[1]
    tm = _tile(m, tm)
    return pl.pallas_call(
        _matmul_bias_kernel,
        grid=(n // tn, m // tm),
        in_specs=[pl.BlockSpec((tm, d), lambda j, i: (i, 0)),
                  pl.BlockSpec((d, tn), lambda j, i: (0, j)),
                  pl.BlockSpec((1, tn), lambda j, i: (0, j))],
        out_specs=pl.BlockSpec((tm, tn), lambda j, i: (i, j)),
        out_shape=jax.ShapeDtypeStruct((m, n), out_dtype),
        compiler_params=_params("parallel", "parallel"),
        name=name,
    )(n2d, w, bias)


def _gates_kernel(x_ref, w_ref, b_ref, o_ref):
    o_ref[0] = _dot_nt(w_ref[...], x_ref[0]) + b_ref[...]


def _in_proj_gates_rows(n, w_t, bias_col):
    b, tt, d = n.shape
    ng = w_t.shape[0]
    tm = _tile(tt, 2560)
    return pl.pallas_call(
        _gates_kernel,
        grid=(b, tt // tm),
        in_specs=[pl.BlockSpec((1, tm, d), lambda i, j: (i, j, 0)),
                  pl.BlockSpec((ng, d), lambda i, j: (0, 0)),
                  pl.BlockSpec((ng, 1), lambda i, j: (0, 0))],
        out_specs=pl.BlockSpec((1, ng, tm), lambda i, j: (i, 0, j)),
        out_shape=jax.ShapeDtypeStruct((b, ng, tt), F32),
        compiler_params=_params("parallel", "parallel"),
        name="in_proj_gates",
    )(n, w_t, bias_col)


def _chunk_mlp_kernel(u_ref, v_ref, z_ref, ws_ref, bs_ref, y_ref):
    tm, width = v_ref.shape
    gd = width // A_GROUPS
    for c in range(tm // A_CHUNK):
        rows = slice(c * A_CHUNK, (c + 1) * A_CHUNK)
        for g in range(A_GROUPS):
            cols = slice(g * gd, (g + 1) * gd)
            v = v_ref[rows, cols].astype(F32)
            mu = jnp.mean(v, axis=-1, keepdims=True)
            vc = v - mu
            var = jnp.mean(vc * vc, axis=-1, keepdims=True)
            vn = (vc * lax.rsqrt(var + EPS)).astype(BF16)
            mixed = _dot(ws_ref[g], vn) + bs_ref[:, g:g + 1]
            gate = _silu(z_ref[rows, cols].astype(F32))
            y_ref[rows, cols] = (u_ref[rows, cols].astype(F32) * mixed * gate).astype(y_ref.dtype)


def _chunk_mlp(p2d, ws, bs_t, col_u, width, tm=1024):
    m = p2d.shape[0]
    tm = _tile(m, tm)
    cb = col_u // width
    spec = lambda k: pl.BlockSpec((tm, width), lambda i, k=k: (i, cb + k))
    return pl.pallas_call(
        _chunk_mlp_kernel,
        grid=(m // tm,),
        in_specs=[spec(0), spec(1), spec(2),
                  pl.BlockSpec(ws.shape, lambda i: (0, 0, 0)),
                  pl.BlockSpec(bs_t.shape, lambda i: (0, 0))],
        out_specs=pl.BlockSpec((tm, width), lambda i: (i, 0)),
        out_shape=jax.ShapeDtypeStruct((m, width), BF16),
        compiler_params=_params("parallel"),
        name="chunk_mlp",
    )(p2d, p2d, p2d, ws, bs_t)


def _conv_kernel(x_ref, w_ref, o_ref, *, t_lat, scale, transpose):
    x = x_ref[0].astype(F32)
    tt = x.shape[0]
    rows = lax.broadcasted_iota(jnp.int32, x.shape, 0)
    prev = jnp.where((rows == 0) | (rows == t_lat), 0.0, pltpu.roll(x, 1, axis=0))
    nxt = jnp.where((rows == t_lat - 1) | (rows == tt - 1), 0.0, pltpu.roll(x, tt - 1, axis=0))
    w = w_ref[...]
    y = _silu(w[0:1] * prev + w[1:2] * x + w[2:3] * nxt) * scale
    o_ref[0] = (y.T if transpose else y).astype(o_ref.dtype)


def _conv_silu(p3d, conv_w, t_lat, col0, wcol0, width, scale, transpose, tc=256):
    b, tt, _ = p3d.shape
    kern = functools.partial(_conv_kernel, t_lat=t_lat, scale=scale, transpose=transpose)
    if transpose:
        out_spec = pl.BlockSpec((1, tc, tt), lambda i, j: (i, j, 0))
        out_shape = jax.ShapeDtypeStruct((b, width, tt), BF16)
    else:
        out_spec = pl.BlockSpec((1, tt, tc), lambda i, j: (i, 0, j))
        out_shape = jax.ShapeDtypeStruct((b, tt, width), BF16)
    return pl.pallas_call(
        kern,
        grid=(b, width // tc),
        in_specs=[pl.BlockSpec((1, tt, tc), lambda i, j: (i, 0, col0 // tc + j)),
                  pl.BlockSpec((CONV_W, tc), lambda i, j: (0, wcol0 // tc + j))],
        out_specs=out_spec,
        out_shape=out_shape,
        compiler_params=_params("parallel", "parallel"),
        name="conv_k_t" if transpose else "conv_q",
    )(p3d, conv_w)


def _mlstm_kernel(*refs, reverse, readout, hd, gb):
    units = [(g, h) for g in range(gb) for h in range(B_HEADS)]
    nu = len(units)
    refs, c_scr, m_scr = refs[:-2 * nu], refs[-2 * nu:-nu], refs[-nu:]
    if readout:
        q_ref, kt_ref, v_ref, gr_ref, hf_ref, o_ref, z_ref, g_ref, out_ref = refs
    else:
        q_ref, kt_ref, v_ref, gr_ref, out_ref = refs

    @pl.when(pl.program_id(1) == 0)
    def _():
        for scr in c_scr + m_scr:
            scr[...] = jnp.zeros_like(scr)

    L = B_CHUNK
    row = lax.broadcasted_iota(jnp.int32, (L, L), 0)
    col = lax.broadcasted_iota(jnp.int32, (L, L), 1)
    seen = (col >= row) if reverse else (col <= row)
    tri_t = jnp.where((row >= col) if reverse else (row <= col), 1.0, 0.0).astype(BF16)
    last = 0 if reverse else L - 1
    gi, gf = (2 * B_HEADS, 3 * B_HEADS) if reverse else (0, B_HEADS)
    cols = [slice(h * hd, (h + 1) * hd) for h in range(B_HEADS)]
    ones_blk = jnp.ones((L, MLSTM_AUG), BF16)
    lane_tile = lambda x, width: jnp.concatenate([x] * (width // MLSTM_AUG), axis=1)
    gates = lambda lo_row: jnp.concatenate([gr_ref[g, lo_row:lo_row + B_HEADS, :] for g in range(gb)], axis=0)

    i_rows = gates(gi)
    f_rows = _log_sigmoid(gates(gf))
    hi = f_rows.astype(BF16).astype(F32)
    r1 = f_rows - hi
    mid = r1.astype(BF16).astype(F32)
    lo = (r1 - mid).astype(BF16).astype(F32)
    terms = jnp.concatenate([hi, mid, lo, jnp.zeros_like(hi)], axis=0).astype(BF16)
    sums = _dot(terms, tri_t)
    cb_rows = sums[0:nu] + sums[nu:2 * nu] + sums[2 * nu:3 * nu]
    tri = jnp.where(seen, 1.0, 0.0).astype(BF16)
    rep = lambda x, u: jnp.broadcast_to(x[u:u + 1], (MLSTM_AUG, L))
    cb_cols = []
    for u in range(nu):
        c3 = _dot_nt(tri, jnp.concatenate([rep(hi, u), rep(mid, u), rep(lo, u)], axis=0).astype(BF16))
        cb_cols.append(c3[:, 0:MLSTM_AUG] + c3[:, MLSTM_AUG:2 * MLSTM_AUG] + c3[:, 2 * MLSTM_AUG:])

    s_bf, e_inter, emt, ws, ec, m_new = [], [], [], [], [], []
    for u, (g, h) in enumerate(units):
        i_row, cb_row = i_rows[u:u + 1], cb_rows[u:u + 1]
        dmat = jnp.where(seen, lane_tile(cb_cols[u], L) - cb_row + i_row, NEG_BIG)
        m = m_scr[u][0:1, 0:1]
        inter = cb_cols[u] + m
        mt = jnp.maximum(inter, jnp.max(dmat, axis=1, keepdims=True))
        decay = jnp.exp(dmat - lane_tile(mt, L))
        s_bf.append((_dot(q_ref[g, :, cols[h]], kt_ref[g, cols[h], :]) * decay).astype(BF16))
        e_inter.append(jnp.exp(inter - mt))
        emt.append(jnp.exp(-mt))
        cl = cb_row[:, last:last + 1]
        w_log = cl - cb_row + i_row
        m_new.append(jnp.maximum(cl + m, jnp.max(w_log, axis=1, keepdims=True)))
        ec.append(jnp.exp(cl + m - m_new[u]))
        ws.append(jnp.exp(w_log - m_new[u]))

    sv, qc, upd = [], [], []
    for u, (g, h) in enumerate(units):
        v_aug = jnp.concatenate([v_ref[g, :, cols[h]].astype(BF16), ones_blk], axis=1)
        sv.append(_dot(s_bf[u], v_aug))
        qc.append(_dot(q_ref[g, :, cols[h]], c_scr[u][...].astype(BF16)))
        kw_t = (kt_ref[g, cols[h], :].astype(F32) * ws[u]).astype(BF16)
        upd.append(_dot(kw_t, v_aug))
    for u, (g, h) in enumerate(units):
        tot = sv[u] + lane_tile(e_inter[u], hd + MLSTM_AUG) * qc[u]
        inv = 1.0 / jnp.maximum(jnp.abs(tot[:, hd:hd + MLSTM_AUG]), emt[u])
        hc = jnp.concatenate([tot[:, c0:c0 + MLSTM_AUG] * inv for c0 in range(0, hd, MLSTM_AUG)], axis=1)
        c_scr[u][...] = ec[u] * c_scr[u][...] + upd[u]
        m_scr[u][...] = jnp.broadcast_to(m_new[u], m_scr[u].shape)
        if readout:
            hs = hf_ref[g, :, cols[h]] + hc
            hn = hs * lax.rsqrt(jnp.mean(hs * hs, axis=-1, keepdims=True) + EPS) * g_ref[:, cols[h]]
            y = hn * _sigmoid(o_ref[g, :, cols[h]].astype(F32)) * _silu(z_ref[g, :, cols[h]].astype(F32))
            out_ref[g, :, cols[h]] = y.astype(out_ref.dtype)
        else:
            out_ref[g, :, cols[h]] = hc.astype(out_ref.dtype)


def _mlstm_dir(q, k_t, p3d, gates_row, t_lat, cols, reverse, hf=None, norm_g=None, gb=2):
    b, tt, width = q.shape
    gb = max(g for g in range(1, gb + 1) if b % g == 0)
    hd = width // B_HEADS
    L = B_CHUNK
    nch, nlat = tt // L, t_lat // L
    if reverse:
        chunk = lambda i: nch - 1 - i
    else:
        chunk = lambda i: lax.rem(i + nlat, nch)
    wb = lambda name: cols[name] // width
    tok = lambda cb: pl.BlockSpec((gb, L, width), lambda bi, i, cb=cb: (bi, chunk(i), cb))
    in_specs = [tok(0), pl.BlockSpec((gb, width, L), lambda bi, i: (bi, 0, chunk(i))), tok(wb("b_v")),
                pl.BlockSpec((gb, gates_row.shape[1], L), lambda bi, i: (bi, 0, chunk(i)))]
    args = [q, k_t, p3d, gates_row]
    readout = hf is not None
    if readout:
        in_specs += [tok(0), tok(wb("b_o")), tok(wb("b_z")), pl.BlockSpec((1, width), lambda bi, i: (0, 0))]
        args += [hf, p3d, p3d, norm_g]
    kern = functools.partial(_mlstm_kernel, reverse=reverse, readout=readout, hd=hd, gb=gb)
    nu = gb * B_HEADS
    return pl.pallas_call(
        kern,
        grid=(b // gb, nch),
        in_specs=in_specs,
        out_specs=tok(0),
        out_shape=jax.ShapeDtypeStruct((b, tt, width), BF16),
        scratch_shapes=[pltpu.VMEM((hd, hd + MLSTM_AUG), F32)] * nu + [pltpu.VMEM((8, 128), F32)] * nu,
        compiler_params=_params("parallel", "arbitrary"),
        name="mlstm_bwd_readout" if readout else "mlstm_fwd",
    )(*args)


def _hgrn_kernel(*refs, reverse, readout, dv, gb):
    if readout:
        q_ref, z_ref, v_ref, lbf_ref, oml_ref, e_ref, of_ref, cg_ref, g_ref, out_ref, s_scr, row_scr, diag_scr = refs
    else:
        q_ref, z_ref, v_ref, lbf_ref, oml_ref, e_ref, out_ref, s_scr, row_scr, diag_scr = refs

    @pl.when(pl.program_id(1) == 0)
    def _():
        s_scr[...] = jnp.zeros_like(s_scr)

    L, dk, nb = C_CHUNK, C_KEY_DIM, C_CHUNK // C_SUB
    row = lax.broadcasted_iota(jnp.int32, (L, L), 0)
    col = lax.broadcasted_iota(jnp.int32, (L, L), 1)
    rb, cbk = row // C_SUB, col // C_SUB
    if reverse:
        seen, blk_before, last = col >= row, cbk > rb, 0
    else:
        seen, blk_before, last = col <= row, cbk < rb, L - 1
    sum_mat = jnp.concatenate([jnp.where(seen, 1.0, 0.0), jnp.where(blk_before, 1.0, 0.0)], axis=0).astype(BF16)
    same_blk = rb == cbk
    sub = lax.broadcasted_iota(jnp.int32, (L, dk), 0) % C_SUB
    pair_ok = [(sub <= j) if reverse else (sub >= j) for j in range(C_SUB)]

    def bcast_sub(ref, j):
        return jnp.concatenate([jnp.broadcast_to(ref[i * C_SUB + j:i * C_SUB + j + 1, :], (C_SUB, dk))
                                for i in range(nb)], axis=0)

    units = [(g, h) for g in range(gb) for h in range(C_HEADS)]
    kcs = [slice(h * dk, (h + 1) * dk) for h in range(C_HEADS)]
    vcs = [slice(h * dv, (h + 1) * dv) for h in range(C_HEADS)]

    q, k, v, c3 = [], [], [], []
    for u, (g, h) in enumerate(units):
        q.append(_silu(q_ref[g, :, kcs[h]].astype(F32)))
        z = z_ref[g, :, kcs[h]]
        v.append(v_ref[g, :, vcs[h]].astype(BF16))
        a = jnp.exp(-jnp.abs(z))
        r = 1.0 / (1.0 + a)
        pos = z >= 0.0
        oml = oml_ref[:, kcs[h]]
        f = lbf_ref[:, kcs[h]] + oml * jnp.where(pos, r, a * r)
        k.append(oml * jnp.where(pos, a * r, r))
        lf = jnp.log2(f)
        hi = lf.astype(BF16)
        r1 = lf - hi.astype(F32)
        mid = r1.astype(BF16)
        lo = (r1 - mid.astype(F32)).astype(BF16)
        c3.append(_dot(sum_mat, jnp.concatenate([hi, mid, lo], axis=1)))

    cb, entries, qds, a_off = [], [], [], []
    span = jnp.zeros((L, dk), F32)
    for u, (g, h) in enumerate(units):
        c1 = c3[u][:, 0:dk] + c3[u][:, dk:2 * dk] + c3[u][:, 2 * dk:3 * dk]
        cbh, entry = c1[0:L], c1[L:2 * L]
        cb.append(cbh)
        entries.append(entry)
        span = jnp.maximum(span, entry - cbh)
        qd = (q[u] * jnp.exp2(cbh - entry)).astype(BF16)
        qds.append(qd)
        parts = []
        for i in range(nb):
            lo_r, hi_r = ((i + 1) * C_SUB, L) if reverse else (0, i * C_SUB)
            if hi_r == lo_r:
                parts.append(jnp.zeros((C_SUB, L), F32))
                continue
            ent = entry[i * C_SUB:i * C_SUB + 1, :]
            kd = (k[u][lo_r:hi_r] * jnp.exp2(ent - cbh[lo_r:hi_r])).astype(BF16)
            pad = [jnp.zeros((n, dk), BF16) for n in (lo_r, L - hi_r)]
            kd = jnp.concatenate([p for p in (pad[0], kd, pad[1]) if p.shape[0]], axis=0)
            parts.append(_dot_nt(qd[i * C_SUB:(i + 1) * C_SUB], kd))
        a_off.append(jnp.concatenate(parts, axis=0))

    factorised_ok = jnp.max(span) <= HGRN_SAFE_LOG2

    @pl.when(factorised_ok)
    def _():
        for u in range(len(units)):
            kd = (k[u] * jnp.exp2(entries[u] - cb[u])).astype(BF16)
            diag_scr[u] = _dot_nt(qds[u], kd)

    @pl.when(jnp.logical_not(factorised_ok))
    def _():
        for u in range(len(units)):
            qk_parts = []
            cb_rows, k_rows = row_scr.at[2 * u], row_scr.at[2 * u + 1]
            cb_rows[...] = cb[u]
            k_rows[...] = k[u]
            for j in range(C_SUB):
                dec = jnp.exp2(jnp.where(pair_ok[j], cb[u] - bcast_sub(cb_rows, j), NEG_BIG))
                qk_parts.append((q[u] * bcast_sub(k_rows, j) * dec).astype(BF16))
            diag_scr[u] = _dot(jnp.concatenate(qk_parts, axis=1), e_ref[...])

    diag_mask = same_blk & seen
    for u, (g, h) in enumerate(units):
        scores = a_off[u] + jnp.where(diag_mask, diag_scr[u], 0.0)
        st = s_scr[u]
        o = _dot(scores.astype(BF16), v[u]) + _dot_nt((q[u] * jnp.exp2(cb[u])).astype(BF16), st.astype(BF16))
        cl = cb[u][last:last + 1, :]
        kdec = (k[u] * jnp.exp2(cl - cb[u])).astype(BF16)
        s_scr[u] = st * jnp.exp2(cl) + _dot_tn(v[u], kdec)
        if readout:
            os_ = of_ref[g, :, vcs[h]] + o
            on = os_ * lax.rsqrt(jnp.mean(os_ * os_, axis=-1, keepdims=True) + EPS) * g_ref[:, vcs[h]]
            out_ref[g, :, vcs[h]] = (on * _silu(cg_ref[g, :, vcs[h]].astype(F32))).astype(out_ref.dtype)
        else:
            out_ref[g, :, vcs[h]] = o.astype(out_ref.dtype)


def _hgrn_dir(pcq, pcf, lbf, oml, e_mat, cols, t_lat, reverse, of=None, norm_g=None, gb=8):
    b, tt, _ = pcq.shape
    kw, L = C_HEADS * C_KEY_DIM, C_CHUNK
    vw = cols["c_g"] - cols["c_i"]
    dv = vw // C_HEADS
    nch, nlat = tt // L, t_lat // L
    if reverse:
        chunk = lambda i: nch - 1 - i
    else:
        chunk = lambda i: lax.rem(i + nlat, nch)
    gb = max(g for g in range(1, gb + 1) if b % g == 0)
    tok = lambda c0, w: pl.BlockSpec((gb, L, w), lambda bi, i: (bi, chunk(i), c0 // w))
    const2 = lambda x: pl.BlockSpec(x.shape, lambda bi, i: (0, 0))
    f_name = "c_f_bwd" if reverse else "c_f_fwd"
    in_specs = [tok(cols["c_q"], kw), tok(cols[f_name], kw), tok(cols["c_i"], vw),
                const2(lbf), const2(oml), const2(e_mat)]
    args = [pcq, pcf, pcq, lbf, oml, e_mat]
    readout = of is not None
    if readout:
        in_specs += [tok(0, vw), tok(cols["c_g"], vw), const2(norm_g)]
        args += [of, pcq, norm_g]
    kern = functools.partial(_hgrn_kernel, reverse=reverse, readout=readout, dv=dv, gb=gb)
    return pl.pallas_call(
        kern,
        grid=(b // gb, nch),
        in_specs=in_specs,
        out_specs=tok(0, vw),
        out_shape=jax.ShapeDtypeStruct((b, tt, vw), BF16),
        scratch_shapes=[pltpu.VMEM((gb * C_HEADS, dv, C_KEY_DIM), F32),
                        pltpu.VMEM((2 * gb * C_HEADS, L, C_KEY_DIM), F32),
                        pltpu.VMEM((gb * C_HEADS, L, L), F32)],
        compiler_params=_params("parallel", "arbitrary"),
        name="hgrn_bwd_readout" if readout else "hgrn_fwd",
    )(*args)


def _out_kernel(ya_ref, yb_ref, yc_ref, w_ref, h_ref, mod_ref, *rest, final, wa, wb):
    d = h_ref.shape[-1]
    w = w_ref
    y = (_dot(ya_ref[0], w[0:wa, :]) + _dot(yb_ref[0], w[wa:wa + wb, :]) + _dot(yc_ref[0], w[wa + wb:, :]))
    h_new = h_ref[0] + mod_ref[0][:, 2 * d:3 * d] * y
    if final:
        g_ref, out_ref = rest
        out_ref[0] = h_new * lax.rsqrt(jnp.mean(h_new * h_new, axis=-1, keepdims=True) + EPS) * g_ref[...]
    else:
        g_ref, modn_ref, h_out_ref, n_ref = rest
        h_out_ref[0] = h_new
        n_ref[0] = _modulated_norm(h_new, g_ref[...], modn_ref[0], d).astype(n_ref.dtype)


def _out_proj(ya, yb, yc, w_out, h, mod, g_next, mod_next, t_lat, final, tm=256):
    b, tt, d = h.shape
    wa, wb, wc = ya.shape[-1], yb.shape[-1], yc.shape[-1]
    if final:
        tm = _tile(t_lat, 512)
    tok = lambda w: pl.BlockSpec((1, tm, w), lambda i, j: (i, j, 0))
    mod_spec = pl.BlockSpec((1, 1, 3 * d), _mod_index(t_lat // tm))
    in_specs = [tok(wa), tok(wb), tok(wc), pl.BlockSpec(w_out.shape, lambda i, j: (0, 0)), tok(d), mod_spec,
                pl.BlockSpec((1, d), lambda i, j: (0, 0))]
    args = [ya, yb, yc, w_out, h, mod, g_next]
    kern = functools.partial(_out_kernel, final=final, wa=wa, wb=wb)
    if final:
        return pl.pallas_call(
            kern, grid=(b, t_lat // tm), in_specs=in_specs, out_specs=tok(d),
            out_shape=jax.ShapeDtypeStruct((b, t_lat, d), F32),
            compiler_params=_params("parallel", "parallel"), name="out_proj_final",
        )(*args)
    return pl.pallas_call(
        kern, grid=(b, tt // tm), in_specs=in_specs + [mod_spec], out_specs=(tok(d), tok(d)),
        out_shape=(jax.ShapeDtypeStruct((b, tt, d), F32), jax.ShapeDtypeStruct((b, tt, d), BF16)),
        compiler_params=_params("parallel", "parallel"), name="out_proj",
    )(*args, mod_next)


def _packed_layout(d):
    a, bw, c, kq = d // 4, d // 2, d // 4, C_HEADS * C_KEY_DIM
    ref_order = (("a_u", a), ("a_v", a), ("a_z", a), ("b_q", bw), ("b_k", bw), ("b_v", bw), ("b_o", bw),
                 ("b_z", bw), ("gates", 4 * B_HEADS), ("c_q", kq), ("c_f_fwd", kq), ("c_f_bwd", kq),
                 ("c_i", c), ("c_g", c))
    src, start = {}, 0
    for name, w in ref_order:
        src[name] = (start, w)
        start += w
    groups = {"ab": ("b_q", "b_k", "b_v", "b_o", "b_z", "a_u", "a_v", "a_z"),
              "cq": ("c_q", "c_i", "c_g"), "cf": ("c_f_fwd", "c_f_bwd")}
    cols = {}
    for names in groups.values():
        pos = 0
        for name in names:
            cols[name] = pos
            pos += src[name][1]
    return src, groups, cols


def _pack_cols(w, bias, src, names):
    spans = []
    for k in names:
        lo, width = src[k]
        if spans and spans[-1][1] == lo:
            spans[-1][1] = lo + width
        else:
            spans.append([lo, lo + width])
    pick = lambda a: jnp.concatenate([a[..., lo:hi] for lo, hi in spans], axis=-1)
    return pick(w).astype(BF16), pick(bias)[None, :]


BF16_ROWS = 16
LANES = 128


def _grid_transpose_kernel(x_ref, o_ref, scr, *, width):
    lanes = scr.shape[-1]
    pitch = scr.shape[1] // BF16_ROWS
    for c in range(scr.shape[0]):
        for r in range(BF16_ROWS):
            scr[c, r * pitch:r * pitch + width, :] = x_ref[0, r * width:(r + 1) * width,
                                                           c * lanes:(c + 1) * lanes].astype(F32)
    for w in range(width):
        rows = [scr[c, pl.ds(w, BF16_ROWS, stride=pitch), :] for c in range(scr.shape[0])]
        o_ref[0, w] = jnp.concatenate(rows, axis=1).astype(o_ref.dtype)


def _copy_kernel(x_ref, dst_ref, o_ref):
    del dst_ref
    o_ref[...] = x_ref[...]


def _grid_transpose(x, t_lat, rows, width):
    b, tt, f = x.shape
    t_ctx = tt - t_lat
    assert rows % BF16_ROWS == 0 and tt % rows == 0 and t_lat % t_ctx == 0
    tile = BF16_ROWS * width
    lat = pl.pallas_call(
        functools.partial(_grid_transpose_kernel, width=width),
        grid=(b, rows // BF16_ROWS),
        in_specs=[pl.BlockSpec((1, tile, f), lambda i, j: (i, j, 0))],
        out_specs=pl.BlockSpec((1, width, BF16_ROWS, f), lambda i, j: (i, 0, j, 0)),
        out_shape=jax.ShapeDtypeStruct((b, tt // rows, rows, f), x.dtype),
        scratch_shapes=[pltpu.VMEM((f // LANES, BF16_ROWS * (width + 8), LANES), F32)],
        compiler_params=_params("parallel", "parallel"),
        name="grid_transpose",
    )(x).reshape(b, tt, f)
    ctx_spec = pl.BlockSpec((1, t_ctx, f), lambda i: (i, t_lat // t_ctx, 0))
    return pl.pallas_call(
        _copy_kernel,
        grid=(b,),
        in_specs=[ctx_spec, pl.BlockSpec(memory_space=pl.ANY)],
        out_specs=ctx_spec,
        out_shape=jax.ShapeDtypeStruct((b, tt, f), x.dtype),
        input_output_aliases={1: 0},
        compiler_params=_params("parallel"),
        name="context_rows_copy",
    )(x, lat)


def kernel(x, c, ctx, c_ctx, w_ada, b_ada, norm_g, w_in, b_in, w_spatial, b_spatial, conv_qk, mlstm_norm,
           hgrn_lb_logits, hgrn_norm, w_out, final_norm):
    b, t_lat, d = x.shape
    t_ctx = ctx.shape[1]
    tt = t_lat + t_ctx
    depth = w_ada.shape[0]
    src, groups, cols = _packed_layout(d)
    a_width, b_width = d // 4, d // 2
    rows = t_lat // GRID_W

    r_pad = -(-(b + 1) // 8) * 8
    cond = jnp.concatenate([c, c_ctx[None, :], jnp.zeros((r_pad - b - 1, d), F32)], axis=0)
    mod_all = _ada_mod(cond, w_ada, b_ada)
    mods = [jnp.stack([mod_all[l, :b], jnp.broadcast_to(mod_all[l, b], (b, 3 * d))], axis=1).reshape(2 * b, 1, 3 * d)
            for l in range(depth)]

    lbf, oml = _lower_bounds(hgrn_lb_logits.astype(F32))

    e_rows = lax.broadcasted_iota(jnp.int32, (C_SUB * C_KEY_DIM, C_CHUNK), 0) // C_KEY_DIM
    e_cols = lax.broadcasted_iota(jnp.int32, (C_SUB * C_KEY_DIM, C_CHUNK), 1) % C_SUB
    e_mat = (e_rows == e_cols).astype(BF16)

    h, n = _join_norm_mod(x, ctx, norm_g[0:1], mods[0])
    out = None
    for l in range(depth):
        last = l == depth - 1
        w_ab, b_ab = _pack_cols(w_in[l], b_in[l], src, groups["ab"])
        w_cq, b_cq = _pack_cols(w_in[l], b_in[l], src, groups["cq"])
        w_cf, b_cf = _pack_cols(w_in[l], b_in[l], src, groups["cf"])
        g0, gw = src["gates"]
        w_gate_t = w_in[l][:, g0:g0 + gw].T.astype(BF16)
        b_gate = b_in[l][g0:g0 + gw][:, None]

        n_cm = _grid_transpose(n, t_lat, rows, GRID_W)
        n2d, n_cm2d = n.reshape(b * tt, d), n_cm.reshape(b * tt, d)
        p2d = _in_proj(n2d, w_ab, b_ab, 2048, 1664, "in_proj_ab", BF16)
        pcq = _in_proj(n_cm2d, w_cq, b_cq, 1024, 1536, "in_proj_cq", BF16).reshape(b, tt, -1)
        pcf = _in_proj(n_cm2d, w_cf, b_cf, 1024, 1024, "in_proj_cf").reshape(b, tt, -1)
        gates_row = _in_proj_gates_rows(n, w_gate_t, b_gate)
        p3d = p2d.reshape(b, tt, -1)

        ya = _chunk_mlp(p2d, w_spatial[l].astype(BF16), b_spatial[l].T, cols["a_u"], a_width).reshape(b, tt, a_width)

        q = _conv_silu(p3d, conv_qk[l], t_lat, cols["b_q"], 0, b_width, 1.0, transpose=False)
        k_t = _conv_silu(p3d, conv_qk[l], t_lat, cols["b_k"], b_width, b_width,
                         (b_width // B_HEADS) ** -0.5, transpose=True)
        hf = _mlstm_dir(q, k_t, p3d, gates_row, t_lat, cols, reverse=False)
        yb = _mlstm_dir(q, k_t, p3d, gates_row, t_lat, cols, reverse=True, hf=hf, norm_g=mlstm_norm[l:l + 1])

        lbf_l, oml_l = lbf[l:l + 1], oml[l:l + 1]
        of = _hgrn_dir(pcq, pcf, lbf_l, oml_l, e_mat, cols, t_lat, False)
        yc_cm = _hgrn_dir(pcq, pcf, lbf_l, oml_l, e_mat, cols, t_lat, True, of=of, norm_g=hgrn_norm[l:l + 1])
        yc = _grid_transpose(yc_cm, t_lat, GRID_W, rows)

        w_o = w_out[l].astype(BF16)
        if last:
            out = _out_proj(ya, yb, yc, w_o, h, mods[l], final_norm[None, :], None, t_lat, final=True)
        else:
            h, n = _out_proj(ya, yb, yc, w_o, h, mods[l], norm_g[l + 1:l + 2], mods[l + 1], t_lat, final=False)
    return out
```

```python
import functools

import jax
import jax.numpy as jnp
from jax import lax
from jax.experimental import pallas as pl
from jax.experimental.pallas import tpu as pltpu

EPS = 1e-6
NEG_BIG = -1e30
LB_FLOOR = 1e-30
GRID_W = 64
CONV_W = 3

A_GROUPS = 4
A_CHUNK = 128
B_HEADS = 4
B_CHUNK = 256
C_HEADS = 4
C_KEY_DIM = 128
C_CHUNK = 64
C_SUB = 32
GATE_PAD = 128
HGRN_SAFE_LOG2 = 100.0
MLSTM_AUG = 128

V7X_VMEM_LIMIT = 56 * 1024 * 1024

F32 = jnp.float32
BF16 = jnp.bfloat16


def _params(*sem):
    return pltpu.CompilerParams(dimension_semantics=sem, vmem_limit_bytes=V7X_VMEM_LIMIT)


def _sigmoid(x):
    return 1.0 / (1.0 + jnp.exp(-x))


def _silu(x):
    return x * _sigmoid(x)


def _log_sigmoid(x):
    return jnp.minimum(x, 0.0) - jnp.log1p(jnp.exp(-jnp.abs(x)))


def _dot(a, b):
    return jnp.dot(a, b, preferred_element_type=F32)


def _dot_nt(a, b):
    return lax.dot_general(a, b, (((1,), (1,)), ((), ())), preferred_element_type=F32)


def _dot_tn(a, b):
    return lax.dot_general(a, b, (((0,), (0,)), ((), ())), preferred_element_type=F32)


def _lb_kernel(x_ref, lbf_ref, oml_ref):
    x = x_ref[...]
    depth = x.shape[0]
    e = jnp.exp(x - jnp.max(x, axis=0, keepdims=True))
    p = e / jnp.sum(e, axis=0, keepdims=True)
    rows = lax.broadcasted_iota(jnp.int32, x.shape, 0)
    lb = jnp.zeros_like(x)
    for j in range(1, depth):
        lb = lb + jnp.where(rows >= j, p[j:j + 1, :], 0.0)
    lbf_ref[...] = jnp.maximum(lb, LB_FLOOR)
    oml_ref[...] = 1.0 - lb


def _lower_bounds(logits):
    shp = jax.ShapeDtypeStruct(logits.shape, F32)
    return pl.pallas_call(_lb_kernel, out_shape=(shp, shp), name="hgrn_lower_bounds")(logits)


def _ada_kernel(c_ref, w_ref, b_ref, o_ref):
    s = _silu(c_ref[...]).astype(BF16)
    o_ref[0] = _dot(s, w_ref[0].astype(BF16)) + b_ref[0]


def _ada_mod(cond, w_ada, b_ada, tn=768):
    depth, d, n3 = w_ada.shape
    r = cond.shape[0]
    return pl.pallas_call(
        _ada_kernel,
        grid=(depth, n3 // tn),
        in_specs=[pl.BlockSpec((r, d), lambda l, j: (0, 0)),
                  pl.BlockSpec((1, d, tn), lambda l, j: (l, 0, j)),
                  pl.BlockSpec((1, 1, tn), lambda l, j: (l, 0, j))],
        out_specs=pl.BlockSpec((1, r, tn), lambda l, j: (l, 0, j)),
        out_shape=jax.ShapeDtypeStruct((depth, r, n3), F32),
        compiler_params=_params("parallel", "parallel"),
        name="adaln_mod",
    )(cond, w_ada, b_ada.reshape(depth, 1, n3))


def _modulated_norm(h, g, mod, d):
    y = h * lax.rsqrt(jnp.mean(h * h, axis=-1, keepdims=True) + EPS) * g
    return y * (1.0 + mod[:, d:2 * d]) + mod[:, 0:d]


def _norm_kernel(x_ref, ctx_ref, g_ref, mod_ref, h_ref, n_ref, *, tiles_lat):
    d = x_ref.shape[-1]

    def emit(src_ref):
        h = src_ref[0]
        h_ref[0] = h
        n_ref[0] = _modulated_norm(h, g_ref[...], mod_ref[0], d).astype(n_ref.dtype)

    is_lat = pl.program_id(1) < tiles_lat
    pl.when(is_lat)(lambda: emit(x_ref))
    pl.when(jnp.logical_not(is_lat))(lambda: emit(ctx_ref))


def _mod_index(tiles_lat):
    return lambda b, j: (2 * b + jnp.where(j >= tiles_lat, 1, 0), 0, 0)


def _join_norm_mod(x, ctx, g, mod, tm=256):
    b, t_lat, d = x.shape
    tt = t_lat + ctx.shape[1]
    tl = t_lat // tm
    tok = pl.BlockSpec((1, tm, d), lambda i, j: (i, j, 0))
    return pl.pallas_call(
        functools.partial(_norm_kernel, tiles_lat=tl),
        grid=(b, tt // tm),
        in_specs=[pl.BlockSpec((1, tm, d), lambda i, j: (i, jnp.minimum(j, tl - 1), 0)),
                  pl.BlockSpec((1, tm, d), lambda i, j: (i, jnp.maximum(j - tl, 0), 0)),
                  pl.BlockSpec((1, d), lambda i, j: (0, 0)),
                  pl.BlockSpec((1, 1, 3 * d), _mod_index(tl))],
        out_specs=(tok, tok),
        out_shape=(jax.ShapeDtypeStruct((b, tt, d), F32), jax.ShapeDtypeStruct((b, tt, d), BF16)),
        compiler_params=_params("parallel", "arbitrary"),
        name="join_norm_mod",
    )(x, ctx, g, mod)


def _matmul_bias_kernel(x_ref, w_ref, b_ref, o_ref):
    o_ref[...] = (_dot(x_ref[...], w_ref[...]) + b_ref[...]).astype(o_ref.dtype)


def _tile(m, pref, unit=128):
    t = min(pref, m) // unit * unit
    while m % t:
        t -= unit
    return t


def _in_proj(n2d, w, bias, tm, tn, name, out_dtype=F32):
    m, d = n2d.shape
    n = w.shape[1]
    tm = _tile(m, tm)
    return pl.pallas_call(
        _matmul_bias_kernel,
        grid=(n // tn, m // tm),
        in_specs=[pl.BlockSpec((tm, d), lambda j, i: (i, 0)),
                  pl.BlockSpec((d, tn), lambda j, i: (0, j)),
                  pl.BlockSpec((1, tn), lambda j, i: (0, j))],
        out_specs=pl.BlockSpec((tm, tn), lambda j, i: (i, j)),
        out_shape=jax.ShapeDtypeStruct((m, n), out_dtype),
        compiler_params=_params("parallel", "parallel"),
        name=name,
    )(n2d, w, bias)


def _gates_kernel(x_ref, w_ref, b_ref, o_ref):
    o_ref[0] = _dot_nt(w_ref[...], x_ref[0]) + b_ref[...]


def _in_proj_gates_rows(n, w_t, bias_col):
    b, tt, d = n.shape
    ng = w_t.shape[0]
    tm = _tile(tt, 2560)
    return pl.pallas_call(
        _gates_kernel,
        grid=(b, tt // tm),
        in_specs=[pl.BlockSpec((1, tm, d), lambda i, j: (i, j, 0)),
                  pl.BlockSpec((ng, d), lambda i, j: (0, 0)),
                  pl.BlockSpec((ng, 1), lambda i, j: (0, 0))],
        out_specs=pl.BlockSpec((1, ng, tm), lambda i, j: (i, 0, j)),
        out_shape=jax.ShapeDtypeStruct((b, ng, tt), F32),
        compiler_params=_params("parallel", "parallel"),
        name="in_proj_gates",
    )(n, w_t, bias_col)


def _chunk_mlp_kernel(u_ref, v_ref, z_ref, ws_ref, bs_ref, y_ref):
    tm, width = v_ref.shape
    gd = width // A_GROUPS
    for c in range(tm // A_CHUNK):
        rows = slice(c * A_CHUNK, (c + 1) * A_CHUNK)
        for g in range(A_GROUPS):
            cols = slice(g * gd, (g + 1) * gd)
            v = v_ref[rows, cols].astype(F32)
            mu = jnp.mean(v, axis=-1, keepdims=True)
            vc = v - mu
            var = jnp.mean(vc * vc, axis=-1, keepdims=True)
            vn = (vc * lax.rsqrt(var + EPS)).astype(BF16)
            mixed = _dot(ws_ref[g], vn) + bs_ref[:, g:g + 1]
            gate = _silu(z_ref[rows, cols].astype(F32))
            y_ref[rows, cols] = (u_ref[rows, cols].astype(F32) * mixed * gate).astype(y_ref.dtype)


def _chunk_mlp(p2d, ws, bs_t, col_u, width, tm=1024):
    m = p2d.shape[0]
    tm = _tile(m, tm)
    cb = col_u // width
    spec = lambda k: pl.BlockSpec((tm, width), lambda i, k=k: (i, cb + k))
    return pl.pallas_call(
        _chunk_mlp_kernel,
        grid=(m // tm,),
        in_specs=[spec(0), spec(1), spec(2),
                  pl.BlockSpec(ws.shape, lambda i: (0, 0, 0)),
                  pl.BlockSpec(bs_t.shape, lambda i: (0, 0))],
        out_specs=pl.BlockSpec((tm, width), lambda i: (i, 0)),
        out_shape=jax.ShapeDtypeStruct((m, width), BF16),
        compiler_params=_params("parallel"),
        name="chunk_mlp",
    )(p2d, p2d, p2d, ws, bs_t)


def _conv_kernel(x_ref, w_ref, o_ref, *, t_lat, scale, transpose):
    x = x_ref[0].astype(F32)
    tt = x.shape[0]
    rows = lax.broadcasted_iota(jnp.int32, x.shape, 0)
    prev = jnp.where((rows == 0) | (rows == t_lat), 0.0, pltpu.roll(x, 1, axis=0))
    nxt = jnp.where((rows == t_lat - 1) | (rows == tt - 1), 0.0, pltpu.roll(x, tt - 1, axis=0))
    w = w_ref[...]
    y = _silu(w[0:1] * prev + w[1:2] * x + w[2:3] * nxt) * scale
    o_ref[0] = (y.T if transpose else y).astype(o_ref.dtype)


def _conv_silu(p3d, conv_w, t_lat, col0, wcol0, width, scale, transpose, tc=256):
    b, tt, _ = p3d.shape
    kern = functools.partial(_conv_kernel, t_lat=t_lat, scale=scale, transpose=transpose)
    if transpose:
        out_spec = pl.BlockSpec((1, tc, tt), lambda i, j: (i, j, 0))
        out_shape = jax.ShapeDtypeStruct((b, width, tt), BF16)
    else:
        out_spec = pl.BlockSpec((1, tt, tc), lambda i, j: (i, 0, j))
        out_shape = jax.ShapeDtypeStruct((b, tt, width), BF16)
    return pl.pallas_call(
        kern,
        grid=(b, width // tc),
        in_specs=[pl.BlockSpec((1, tt, tc), lambda i, j: (i, 0, col0 // tc + j)),
                  pl.BlockSpec((CONV_W, tc), lambda i, j: (0, wcol0 // tc + j))],
        out_specs=out_spec,
        out_shape=out_shape,
        compiler_params=_params("parallel", "parallel"),
        name="conv_k_t" if transpose else "conv_q",
    )(p3d, conv_w)


def _mlstm_kernel(*refs, reverse, readout, hd, gb):
    units = [(g, h) for g in range(gb) for h in range(B_HEADS)]
    nu = len(units)
    refs, c_scr, m_scr = refs[:-2 * nu], refs[-2 * nu:-nu], refs[-nu:]
    if readout:
        q_ref, kt_ref, v_ref, gr_ref, hf_ref, o_ref, z_ref, g_ref, out_ref = refs
    else:
        q_ref, kt_ref, v_ref, gr_ref, out_ref = refs

    @pl.when(pl.program_id(1) == 0)
    def _():
        for scr in c_scr + m_scr:
            scr[...] = jnp.zeros_like(scr)

    L = B_CHUNK
    row = lax.broadcasted_iota(jnp.int32, (L, L), 0)
    col = lax.broadcasted_iota(jnp.int32, (L, L), 1)
    seen = (col >= row) if reverse else (col <= row)
    tri_t = jnp.where((row >= col) if reverse else (row <= col), 1.0, 0.0).astype(BF16)
    last = 0 if reverse else L - 1
    gi, gf = (2 * B_HEADS, 3 * B_HEADS) if reverse else (0, B_HEADS)
    cols = [slice(h * hd, (h + 1) * hd) for h in range(B_HEADS)]
    ones_blk = jnp.ones((L, MLSTM_AUG), BF16)
    lane_tile = lambda x, width: jnp.concatenate([x] * (width // MLSTM_AUG), axis=1)
    gates = lambda lo_row: jnp.concatenate([gr_ref[g, lo_row:lo_row + B_HEADS, :] for g in range(gb)], axis=0)

    i_rows = gates(gi)
    f_rows = _log_sigmoid(gates(gf))
    hi = f_rows.astype(BF16).astype(F32)
    r1 = f_rows - hi
    mid = r1.astype(BF16).astype(F32)
    lo = (r1 - mid).astype(BF16).astype(F32)
    terms = jnp.concatenate([hi, mid, lo, jnp.zeros_like(hi)], axis=0).astype(BF16)
    sums = _dot(terms, tri_t)
    cb_rows = sums[0:nu] + sums[nu:2 * nu] + sums[2 * nu:3 * nu]
    tri = jnp.where(seen, 1.0, 0.0).astype(BF16)
    rep = lambda x, u: jnp.broadcast_to(x[u:u + 1], (MLSTM_AUG, L))
    cb_cols = []
    for u in range(nu):
        c3 = _dot_nt(tri, jnp.concatenate([rep(hi, u), rep(mid, u), rep(lo, u)], axis=0).astype(BF16))
        cb_cols.append(c3[:, 0:MLSTM_AUG] + c3[:, MLSTM_AUG:2 * MLSTM_AUG] + c3[:, 2 * MLSTM_AUG:])

    s_bf, e_inter, emt, ws, ec, m_new = [], [], [], [], [], []
    for u, (g, h) in enumerate(units):
        i_row, cb_row = i_rows[u:u + 1], cb_rows[u:u + 1]
        dmat = jnp.where(seen, lane_tile(cb_cols[u], L) - cb_row + i_row, NEG_BIG)
        m = m_scr[u][0:1, 0:1]
        inter = cb_cols[u] + m
        mt = jnp.maximum(inter, jnp.max(dmat, axis=1, keepdims=True))
        decay = jnp.exp(dmat - lane_tile(mt, L))
        s_bf.append((_dot(q_ref[g, :, cols[h]], kt_ref[g, cols[h], :]) * decay).astype(BF16))
        e_inter.append(jnp.exp(inter - mt))
        emt.append(jnp.exp(-mt))
        cl = cb_row[:, last:last + 1]
        w_log = cl - cb_row + i_row
        m_new.append(jnp.maximum(cl + m, jnp.max(w_log, axis=1, keepdims=True)))
        ec.append(jnp.exp(cl + m - m_new[u]))
        ws.append(jnp.exp(w_log - m_new[u]))

    sv, qc, upd = [], [], []
    for u, (g, h) in enumerate(units):
        v_aug = jnp.concatenate([v_ref[g, :, cols[h]].astype(BF16), ones_blk], axis=1)
        sv.append(_dot(s_bf[u], v_aug))
        qc.append(_dot(q_ref[g, :, cols[h]], c_scr[u][...].astype(BF16)))
        kw_t = (kt_ref[g, cols[h], :].astype(F32) * ws[u]).astype(BF16)
        upd.append(_dot(kw_t, v_aug))
    for u, (g, h) in enumerate(units):
        tot = sv[u] + lane_tile(e_inter[u], hd + MLSTM_AUG) * qc[u]
        inv = 1.0 / jnp.maximum(jnp.abs(tot[:, hd:hd + MLSTM_AUG]), emt[u])
        hc = jnp.concatenate([tot[:, c0:c0 + MLSTM_AUG] * inv for c0 in range(0, hd, MLSTM_AUG)], axis=1)
        c_scr[u][...] = ec[u] * c_scr[u][...] + upd[u]
        m_scr[u][...] = jnp.broadcast_to(m_new[u], m_scr[u].shape)
        if readout:
            hs = hf_ref[g, :, cols[h]] + hc
            hn = hs * lax.rsqrt(jnp.mean(hs * hs, axis=-1, keepdims=True) + EPS) * g_ref[:, cols[h]]
            y = hn * _sigmoid(o_ref[g, :, cols[h]].astype(F32)) * _silu(z_ref[g, :, cols[h]].astype(F32))
            out_ref[g, :, cols[h]] = y.astype(out_ref.dtype)
        else:
            out_ref[g, :, cols[h]] = hc.astype(out_ref.dtype)


def _mlstm_dir(q, k_t, p3d, gates_row, t_lat, cols, reverse, hf=None, norm_g=None, gb=2):
    b, tt, width = q.shape
    gb = max(g for g in range(1, gb + 1) if b % g == 0)
    hd = width // B_HEADS
    L = B_CHUNK
    nch, nlat = tt // L, t_lat // L
    if reverse:
        chunk = lambda i: nch - 1 - i
    else:
        chunk = lambda i: lax.rem(i + nlat, nch)
    wb = lambda name: cols[name] // width
    tok = lambda cb: pl.BlockSpec((gb, L, width), lambda bi, i, cb=cb: (bi, chunk(i), cb))
    in_specs = [tok(0), pl.BlockSpec((gb, width, L), lambda bi, i: (bi, 0, chunk(i))), tok(wb("b_v")),
                pl.BlockSpec((gb, gates_row.shape[1], L), lambda bi, i: (bi, 0, chunk(i)))]
    args = [q, k_t, p3d, gates_row]
    readout = hf is not None
    if readout:
        in_specs += [tok(0), tok(wb("b_o")), tok(wb("b_z")), pl.BlockSpec((1, width), lambda bi, i: (0, 0))]
        args += [hf, p3d, p3d, norm_g]
    kern = functools.partial(_mlstm_kernel, reverse=reverse, readout=readout, hd=hd, gb=gb)
    nu = gb * B_HEADS
    return pl.pallas_call(
        kern,
        grid=(b // gb, nch),
        in_specs=in_specs,
        out_specs=tok(0),
        out_shape=jax.ShapeDtypeStruct((b, tt, width), BF16),
        scratch_shapes=[pltpu.VMEM((hd, hd + MLSTM_AUG), F32)] * nu + [pltpu.VMEM((8, 128), F32)] * nu,
        compiler_params=_params("parallel", "arbitrary"),
        name="mlstm_bwd_readout" if readout else "mlstm_fwd",
    )(*args)


def _hgrn_kernel(*refs, reverse, readout, dv, gb):
    if readout:
        q_ref, z_ref, v_ref, lbf_ref, oml_ref, e_ref, of_ref, cg_ref, g_ref, out_ref, s_scr, row_scr, diag_scr = refs
    else:
        q_ref, z_ref, v_ref, lbf_ref, oml_ref, e_ref, out_ref, s_scr, row_scr, diag_scr = refs

    @pl.when(pl.program_id(1) == 0)
    def _():
        s_scr[...] = jnp.zeros_like(s_scr)

    L, dk, nb = C_CHUNK, C_KEY_DIM, C_CHUNK // C_SUB
    row = lax.broadcasted_iota(jnp.int32, (L, L), 0)
    col = lax.broadcasted_iota(jnp.int32, (L, L), 1)
    rb, cbk = row // C_SUB, col // C_SUB
    if reverse:
        seen, blk_before, last = col >= row, cbk > rb, 0
    else:
        seen, blk_before, last = col <= row, cbk < rb, L - 1
    sum_mat = jnp.concatenate([jnp.where(seen, 1.0, 0.0), jnp.where(blk_before, 1.0, 0.0)], axis=0).astype(BF16)
    same_blk = rb == cbk
    sub = lax.broadcasted_iota(jnp.int32, (L, dk), 0) % C_SUB
    pair_ok = [(sub <= j) if reverse else (sub >= j) for j in range(C_SUB)]

    def bcast_sub(ref, j):
        return jnp.concatenate([jnp.broadcast_to(ref[i * C_SUB + j:i * C_SUB + j + 1, :], (C_SUB, dk))
                                for i in range(nb)], axis=0)

    units = [(g, h) for g in range(gb) for h in range(C_HEADS)]
    kcs = [slice(h * dk, (h + 1) * dk) for h in range(C_HEADS)]
    vcs = [slice(h * dv, (h + 1) * dv) for h in range(C_HEADS)]

    q, k, v, c3 = [], [], [], []
    for u, (g, h) in enumerate(units):
        q.append(_silu(q_ref[g, :, kcs[h]].astype(F32)))
        z = z_ref[g, :, kcs[h]]
        v.append(v_ref[g, :, vcs[h]].astype(BF16))
        a = jnp.exp(-jnp.abs(z))
        r = 1.0 / (1.0 + a)
        pos = z >= 0.0
        oml = oml_ref[:, kcs[h]]
        f = lbf_ref[:, kcs[h]] + oml * jnp.where(pos, r, a * r)
        k.append(oml * jnp.where(pos, a * r, r))
        lf = jnp.log2(f)
        hi = lf.astype(BF16)
        r1 = lf - hi.astype(F32)
        mid = r1.astype(BF16)
        lo = (r1 - mid.astype(F32)).astype(BF16)
        c3.append(_dot(sum_mat, jnp.concatenate([hi, mid, lo], axis=1)))

    cb, entries, qds, a_off = [], [], [], []
    span = jnp.zeros((L, dk), F32)
    for u, (g, h) in enumerate(units):
        c1 = c3[u][:, 0:dk] + c3[u][:, dk:2 * dk] + c3[u][:, 2 * dk:3 * dk]
        cbh, entry = c1[0:L], c1[L:2 * L]
        cb.append(cbh)
        entries.append(entry)
        span = jnp.maximum(span, entry - cbh)
        qd = (q[u] * jnp.exp2(cbh - entry)).astype(BF16)
        qds.append(qd)
        parts = []
        for i in range(nb):
            lo_r, hi_r = ((i + 1) * C_SUB, L) if reverse else (0, i * C_SUB)
            if hi_r == lo_r:
                parts.append(jnp.zeros((C_SUB, L), F32))
                continue
            ent = entry[i * C_SUB:i * C_SUB + 1, :]
            kd = (k[u][lo_r:hi_r] * jnp.exp2(ent - cbh[lo_r:hi_r])).astype(BF16)
            pad = [jnp.zeros((n, dk), BF16) for n in (lo_r, L - hi_r)]
            kd = jnp.concatenate([p for p in (pad[0], kd, pad[1]) if p.shape[0]], axis=0)
            parts.append(_dot_nt(qd[i * C_SUB:(i + 1) * C_SUB], kd))
        a_off.append(jnp.concatenate(parts, axis=0))

    factorised_ok = jnp.max(span) <= HGRN_SAFE_LOG2

    @pl.when(factorised_ok)
    def _():
        for u in range(len(units)):
            kd = (k[u] * jnp.exp2(entries[u] - cb[u])).astype(BF16)
            diag_scr[u] = _dot_nt(qds[u], kd)

    @pl.when(jnp.logical_not(factorised_ok))
    def _():
        for u in range(len(units)):
            qk_parts = []
            cb_rows, k_rows = row_scr.at[2 * u], row_scr.at[2 * u + 1]
            cb_rows[...] = cb[u]
            k_rows[...] = k[u]
            for j in range(C_SUB):
                dec = jnp.exp2(jnp.where(pair_ok[j], cb[u] - bcast_sub(cb_rows, j), NEG_BIG))
                qk_parts.append((q[u] * bcast_sub(k_rows, j) * dec).astype(BF16))
            diag_scr[u] = _dot(jnp.concatenate(qk_parts, axis=1), e_ref[...])

    diag_mask = same_blk & seen
    for u, (g, h) in enumerate(units):
        scores = a_off[u] + jnp.where(diag_mask, diag_scr[u], 0.0)
        st = s_scr[u]
        o = _dot(scores.astype(BF16), v[u]) + _dot_nt((q[u] * jnp.exp2(cb[u])).astype(BF16), st.astype(BF16))
        cl = cb[u][last:last + 1, :]
        kdec = (k[u] * jnp.exp2(cl - cb[u])).astype(BF16)
        s_scr[u] = st * jnp.exp2(cl) + _dot_tn(v[u], kdec)
        if readout:
            os_ = of_ref[g, :, vcs[h]] + o
            on = os_ * lax.rsqrt(jnp.mean(os_ * os_, axis=-1, keepdims=True) + EPS) * g_ref[:, vcs[h]]
            out_ref[g, :, vcs[h]] = (on * _silu(cg_ref[g, :, vcs[h]].astype(F32))).astype(out_ref.dtype)
        else:
            out_ref[g, :, vcs[h]] = o.astype(out_ref.dtype)


def _hgrn_dir(pcq, pcf, lbf, oml, e_mat, cols, t_lat, reverse, of=None, norm_g=None, gb=8):
    b, tt, _ = pcq.shape
    kw, L = C_HEADS * C_KEY_DIM, C_CHUNK
    vw = cols["c_g"] - cols["c_i"]
    dv = vw // C_HEADS
    nch, nlat = tt // L, t_lat // L
    if reverse:
        chunk = lambda i: nch - 1 - i
    else:
        chunk = lambda i: lax.rem(i + nlat, nch)
    gb = max(g for g in range(1, gb + 1) if b % g == 0)
    tok = lambda c0, w: pl.BlockSpec((gb, L, w), lambda bi, i: (bi, chunk(i), c0 // w))
    const2 = lambda x: pl.BlockSpec(x.shape, lambda bi, i: (0, 0))
    f_name = "c_f_bwd" if reverse else "c_f_fwd"
    in_specs = [tok(cols["c_q"], kw), tok(cols[f_name], kw), tok(cols["c_i"], vw),
                const2(lbf), const2(oml), const2(e_mat)]
    args = [pcq, pcf, pcq, lbf, oml, e_mat]
    readout = of is not None
    if readout:
        in_specs += [tok(0, vw), tok(cols["c_g"], vw), const2(norm_g)]
        args += [of, pcq, norm_g]
    kern = functools.partial(_hgrn_kernel, reverse=reverse, readout=readout, dv=dv, gb=gb)
    return pl.pallas_call(
        kern,
        grid=(b // gb, nch),
        in_specs=in_specs,
        out_specs=tok(0, vw),
        out_shape=jax.ShapeDtypeStruct((b, tt, vw), BF16),
        scratch_shapes=[pltpu.VMEM((gb * C_HEADS, dv, C_KEY_DIM), F32),
                        pltpu.VMEM((2 * gb * C_HEADS, L, C_KEY_DIM), F32),
                        pltpu.VMEM((gb * C_HEADS, L, L), F32)],
        compiler_params=_params("parallel", "arbitrary"),
        name="hgrn_bwd_readout" if readout else "hgrn_fwd",
    )(*args)


def _out_kernel(ya_ref, yb_ref, yc_ref, w_ref, h_ref, mod_ref, *rest, final, wa, wb):
    d = h_ref.shape[-1]
    w = w_ref
    y = (_dot(ya_ref[0], w[0:wa, :]) + _dot(yb_ref[0], w[wa:wa + wb, :]) + _dot(yc_ref[0], w[wa + wb:, :]))
    h_new = h_ref[0] + mod_ref[0][:, 2 * d:3 * d] * y
    if final:
        g_ref, out_ref = rest
        out_ref[0] = h_new * lax.rsqrt(jnp.mean(h_new * h_new, axis=-1, keepdims=True) + EPS) * g_ref[...]
    else:
        g_ref, modn_ref, h_out_ref, n_ref = rest
        h_out_ref[0] = h_new
        n_ref[0] = _modulated_norm(h_new, g_ref[...], modn_ref[0], d).astype(n_ref.dtype)


def _out_proj(ya, yb, yc, w_out, h, mod, g_next, mod_next, t_lat, final, tm=256):
    b, tt, d = h.shape
    wa, wb, wc = ya.shape[-1], yb.shape[-1], yc.shape[-1]
    if final:
        tm = _tile(t_lat, 512)
    tok = lambda w: pl.BlockSpec((1, tm, w), lambda i, j: (i, j, 0))
    mod_spec = pl.BlockSpec((1, 1, 3 * d), _mod_index(t_lat // tm))
    in_specs = [tok(wa), tok(wb), tok(wc), pl.BlockSpec(w_out.shape, lambda i, j: (0, 0)), tok(d), mod_spec,
                pl.BlockSpec((1, d), lambda i, j: (0, 0))]
    args = [ya, yb, yc, w_out, h, mod, g_next]
    kern = functools.partial(_out_kernel, final=final, wa=wa, wb=wb)
    if final:
        return pl.pallas_call(
            kern, grid=(b, t_lat // tm), in_specs=in_specs, out_specs=tok(d),
            out_shape=jax.ShapeDtypeStruct((b, t_lat, d), F32),
            compiler_params=_params("parallel", "parallel"), name="out_proj_final",
        )(*args)
    return pl.pallas_call(
        kern, grid=(b, tt // tm), in_specs=in_specs + [mod_spec], out_specs=(tok(d), tok(d)),
        out_shape=(jax.ShapeDtypeStruct((b, tt, d), F32), jax.ShapeDtypeStruct((b, tt, d), BF16)),
        compiler_params=_params("parallel", "parallel"), name="out_proj",
    )(*args, mod_next)


def _packed_layout(d):
    a, bw, c, kq = d // 4, d // 2, d // 4, C_HEADS * C_KEY_DIM
    ref_order = (("a_u", a), ("a_v", a), ("a_z", a), ("b_q", bw), ("b_k", bw), ("b_v", bw), ("b_o", bw),
                 ("b_z", bw), ("gates", 4 * B_HEADS), ("c_q", kq), ("c_f_fwd", kq), ("c_f_bwd", kq),
                 ("c_i", c), ("c_g", c))
    src, start = {}, 0
    for name, w in ref_order:
        src[name] = (start, w)
        start += w
    groups = {"ab": ("b_q", "b_k", "b_v", "b_o", "b_z", "a_u", "a_v", "a_z"),
              "cq": ("c_q", "c_i", "c_g"), "cf": ("c_f_fwd", "c_f_bwd")}
    cols = {}
    for names in groups.values():
        pos = 0
        for name in names:
            cols[name] = pos
            pos += src[name][1]
    return src, groups, cols


def _pack_cols(w, bias, src, names):
    spans = []
    for k in names:
        lo, width = src[k]
        if spans and spans[-1][1] == lo:
            spans[-1][1] = lo + width
        else:
            spans.append([lo, lo + width])
    pick = lambda a: jnp.concatenate([a[..., lo:hi] for lo, hi in spans], axis=-1)
    return pick(w).astype(BF16), pick(bias)[None, :]


BF16_ROWS = 16
LANES = 128


def _grid_transpose_kernel(x_ref, o_ref, scr, *, width):
    lanes = scr.shape[-1]
    pitch = scr.shape[1] // BF16_ROWS
    for c in range(scr.shape[0]):
        for r in range(BF16_ROWS):
            scr[c, r * pitch:r * pitch + width, :] = x_ref[0, r * width:(r + 1) * width,
                                                           c * lanes:(c + 1) * lanes].astype(F32)
    for w in range(width):
        rows = [scr[c, pl.ds(w, BF16_ROWS, stride=pitch), :] for c in range(scr.shape[0])]
        o_ref[0, w] = jnp.concatenate(rows, axis=1).astype(o_ref.dtype)


def _copy_kernel(x_ref, dst_ref, o_ref):
    del dst_ref
    o_ref[...] = x_ref[...]


def _grid_transpose(x, t_lat, rows, width):
    b, tt, f = x.shape
    t_ctx = tt - t_lat
    assert rows % BF16_ROWS == 0 and tt % rows == 0 and t_lat % t_ctx == 0
    tile = BF16_ROWS * width
    lat = pl.pallas_call(
        functools.partial(_grid_transpose_kernel, width=width),
        grid=(b, rows // BF16_ROWS),
        in_specs=[pl.BlockSpec((1, tile, f), lambda i, j: (i, j, 0))],
        out_specs=pl.BlockSpec((1, width, BF16_ROWS, f), lambda i, j: (i, 0, j, 0)),
        out_shape=jax.ShapeDtypeStruct((b, tt // rows, rows, f), x.dtype),
        scratch_shapes=[pltpu.VMEM((f // LANES, BF16_ROWS * (width + 8), LANES), F32)],
        compiler_params=_params("parallel", "parallel"),
        name="grid_transpose",
    )(x).reshape(b, tt, f)
    ctx_spec = pl.BlockSpec((1, t_ctx, f), lambda i: (i, t_lat // t_ctx, 0))
    return pl.pallas_call(
        _copy_kernel,
        grid=(b,),
        in_specs=[ctx_spec, pl.BlockSpec(memory_space=pl.ANY)],
        out_specs=ctx_spec,
        out_shape=jax.ShapeDtypeStruct((b, tt, f), x.dtype),
        input_output_aliases={1: 0},
        compiler_params=_params("parallel"),
        name="context_rows_copy",
    )(x, lat)


def kernel(x, c, ctx, c_ctx, w_ada, b_ada, norm_g, w_in, b_in, w_spatial, b_spatial, conv_qk, mlstm_norm,
           hgrn_lb_logits, hgrn_norm, w_out, final_norm):
    b, t_lat, d = x.shape
    t_ctx = ctx.shape[1]
    tt = t_lat + t_ctx
    depth = w_ada.shape[0]
    src, groups, cols = _packed_layout(d)
    a_width, b_width = d // 4, d // 2
    rows = t_lat // GRID_W

    r_pad = -(-(b + 1) // 8) * 8
    cond = jnp.concatenate([c, c_ctx[None, :], jnp.zeros((r_pad - b - 1, d), F32)], axis=0)
    mod_all = _ada_mod(cond, w_ada, b_ada)
    mods = [jnp.stack([mod_all[l, :b], jnp.broadcast_to(mod_all[l, b], (b, 3 * d))], axis=1).reshape(2 * b, 1, 3 * d)
            for l in range(depth)]

    lbf, oml = _lower_bounds(hgrn_lb_logits.astype(F32))

    e_rows = lax.broadcasted_iota(jnp.int32, (C_SUB * C_KEY_DIM, C_CHUNK), 0) // C_KEY_DIM
    e_cols = lax.broadcasted_iota(jnp.int32, (C_SUB * C_KEY_DIM, C_CHUNK), 1) % C_SUB
    e_mat = (e_rows == e_cols).astype(BF16)

    h, n = _join_norm_mod(x, ctx, norm_g[0:1], mods[0])
    out = None
    for l in range(depth):
        last = l == depth - 1
        w_ab, b_ab = _pack_cols(w_in[l], b_in[l], src, groups["ab"])
        w_cq, b_cq = _pack_cols(w_in[l], b_in[l], src, groups["cq"])
        w_cf, b_cf = _pack_cols(w_in[l], b_in[l], src, groups["cf"])
        g0, gw = src["gates"]
        w_gate_t = w_in[l][:, g0:g0 + gw].T.astype(BF16)
        b_gate = b_in[l][g0:g0 + gw][:, None]

        n_cm = _grid_transpose(n, t_lat, rows, GRID_W)
        n2d, n_cm2d = n.reshape(b * tt, d), n_cm.reshape(b * tt, d)
        p2d = _in_proj(n2d, w_ab, b_ab, 2048, 1664, "in_proj_ab", BF16)
        pcq = _in_proj(n_cm2d, w_cq, b_cq, 1024, 1536, "in_proj_cq", BF16).reshape(b, tt, -1)
        pcf = _in_proj(n_cm2d, w_cf, b_cf, 1024, 1024, "in_proj_cf").reshape(b, tt, -1)
        gates_row = _in_proj_gates_rows(n, w_gate_t, b_gate)
        p3d = p2d.reshape(b, tt, -1)

        ya = _chunk_mlp(p2d, w_spatial[l].astype(BF16), b_spatial[l].T, cols["a_u"], a_width).reshape(b, tt, a_width)

        q = _conv_silu(p3d, conv_qk[l], t_lat, cols["b_q"], 0, b_width, 1.0, transpose=False)
        k_t = _conv_silu(p3d, conv_qk[l], t_lat, cols["b_k"], b_width, b_width,
                         (b_width // B_HEADS) ** -0.5, transpose=True)
        hf = _mlstm_dir(q, k_t, p3d, gates_row, t_lat, cols, reverse=False)
        yb = _mlstm_dir(q, k_t, p3d, gates_row, t_lat, cols, reverse=True, hf=hf, norm_g=mlstm_norm[l:l + 1])

        lbf_l, oml_l = lbf[l:l + 1], oml[l:l + 1]
        of = _hgrn_dir(pcq, pcf, lbf_l, oml_l, e_mat, cols, t_lat, False)
        yc_cm = _hgrn_dir(pcq, pcf, lbf_l, oml_l, e_mat, cols, t_lat, True, of=of, norm_g=hgrn_norm[l:l + 1])
        yc = _grid_transpose(yc_cm, t_lat, GRID_W, rows)

        w_o = w_out[l].astype(BF16)
        if last:
            out = _out_proj(ya, yb, yc, w_o, h, mods[l], final_norm[None, :], None, t_lat, final=True)
        else:
            h, n = _out_proj(ya, yb, yc, w_o, h, mods[l], norm_g[l + 1:l + 2], mods[l + 1], t_lat, final=False)
    return out
```

```python
import functools

import jax
import jax.numpy as jnp
from jax import lax
from jax.experimental import pallas as pl
from jax.experimental.pallas import tpu as pltpu

EPS = 1e-6
NEG_BIG = -1e30
LB_FLOOR = 1e-30
GRID_W = 64
CONV_W = 3

A_GROUPS = 4
A_CHUNK = 128
B_HEADS = 4
B_CHUNK = 256
C_HEADS = 4
C_KEY_DIM = 128
C_CHUNK = 64
C_SUB = 32
HGRN_SAFE_LOG2 = 100.0
MLSTM_AUG = 128

V7X_VMEM_LIMIT = 56 * 1024 * 1024

F32 = jnp.float32
BF16 = jnp.bfloat16


def _params(*sem):
    return pltpu.CompilerParams(dimension_semantics=sem, vmem_limit_bytes=V7X_VMEM_LIMIT)


def _sigmoid(x):
    return 1.0 / (1.0 + jnp.exp(-x))


def _silu(x):
    return x * _sigmoid(x)


def _log_sigmoid(x):
    return jnp.minimum(x, 0.0) - jnp.log1p(jnp.exp(-jnp.abs(x)))


def _dot(a, b):
    return jnp.dot(a, b, preferred_element_type=F32)


def _dot_nt(a, b):
    return lax.dot_general(a, b, (((1,), (1,)), ((), ())), preferred_element_type=F32)


def _dot_tn(a, b):
    return lax.dot_general(a, b, (((0,), (0,)), ((), ())), preferred_element_type=F32)


def _lb_kernel(x_ref, lbf_ref, oml_ref):
    x = x_ref[...]
    depth = x.shape[0]
    e = jnp.exp(x - jnp.max(x, axis=0, keepdims=True))
    p = e / jnp.sum(e, axis=0, keepdims=True)
    rows = lax.broadcasted_iota(jnp.int32, x.shape, 0)
    lb = jnp.zeros_like(x)
    for j in range(1, depth):
        lb = lb + jnp.where(rows >= j, p[j:j + 1, :], 0.0)
    lbf_ref[...] = jnp.maximum(lb, LB_FLOOR)
    oml_ref[...] = 1.0 - lb


def _lower_bounds(logits):
    shp = jax.ShapeDtypeStruct(logits.shape, F32)
    return pl.pallas_call(_lb_kernel, out_shape=(shp, shp), name="hgrn_lower_bounds")(logits)


def _ada_kernel(c_ref, w_ref, b_ref, o_ref):
    s = _silu(c_ref[...]).astype(BF16)
    o_ref[0] = _dot(s, w_ref[0].astype(BF16)) + b_ref[0]


def _ada_mod(cond, w_ada, b_ada, tn=768):
    depth, d, n3 = w_ada.shape
    r = cond.shape[0]
    return pl.pallas_call(
        _ada_kernel,
        grid=(depth, n3 // tn),
        in_specs=[pl.BlockSpec((r, d), lambda l, j: (0, 0)),
                  pl.BlockSpec((1, d, tn), lambda l, j: (l, 0, j)),
                  pl.BlockSpec((1, 1, tn), lambda l, j: (l, 0, j))],
        out_specs=pl.BlockSpec((1, r, tn), lambda l, j: (l, 0, j)),
        out_shape=jax.ShapeDtypeStruct((depth, r, n3), F32),
        compiler_params=_params("parallel", "parallel"),
        name="adaln_mod",
    )(cond, w_ada, b_ada.reshape(depth, 1, n3))


def _modulated_norm(h, g, mod, d):
    y = h * lax.rsqrt(jnp.mean(h * h, axis=-1, keepdims=True) + EPS) * g
    return y * (1.0 + mod[:, d:2 * d]) + mod[:, 0:d]


def _norm_kernel(x_ref, ctx_ref, g_ref, mod_ref, h_ref, n_ref, *, tiles_lat):
    d = x_ref.shape[-1]

    def emit(src_ref):
        h = src_ref[0]
        h_ref[0] = h
        n_ref[0] = _modulated_norm(h, g_ref[...], mod_ref[0], d).astype(n_ref.dtype)

    is_lat = pl.program_id(1) < tiles_lat
    pl.when(is_lat)(lambda: emit(x_ref))
    pl.when(jnp.logical_not(is_lat))(lambda: emit(ctx_ref))


def _mod_index(tiles_lat):
    return lambda b, j: (2 * b + jnp.where(j >= tiles_lat, 1, 0), 0, 0)


def _join_norm_mod(x, ctx, g, mod, tm=256):
    b, t_lat, d = x.shape
    tt = t_lat + ctx.shape[1]
    tl = t_lat // tm
    tok = pl.BlockSpec((1, tm, d), lambda i, j: (i, j, 0))
    return pl.pallas_call(
        functools.partial(_norm_kernel, tiles_lat=tl),
        grid=(b, tt // tm),
        in_specs=[pl.BlockSpec((1, tm, d), lambda i, j: (i, jnp.minimum(j, tl - 1), 0)),
                  pl.BlockSpec((1, tm, d), lambda i, j: (i, jnp.maximum(j - tl, 0), 0)),
                  pl.BlockSpec((1, d), lambda i, j: (0, 0)),
                  pl.BlockSpec((1, 1, 3 * d), _mod_index(tl))],
        out_specs=(tok, tok),
        out_shape=(jax.ShapeDtypeStruct((b, tt, d), F32), jax.ShapeDtypeStruct((b, tt, d), BF16)),
        compiler_params=_params("parallel", "arbitrary"),
        name="join_norm_mod",
    )(x, ctx, g, mod)


def _matmul_bias_kernel(x_ref, w_ref, b_ref, o_ref):
    o_ref[...] = (_dot(x_ref[...], w_ref[...]) + b_ref[...]).astype(o_ref.dtype)


def _tile(m, pref, unit=128):
    t = min(pref, m) // unit * unit
    while m % t:
        t -= unit
    return t


def _in_proj(n2d, w, bias, tm, tn, name, out_dtype=F32):
    m, d = n2d.shape
    n = w.shape[1]
    tm = _tile(m, tm)
    return pl.pallas_call(
        _matmul_bias_kernel,
        grid=(n // tn, m // tm),
        in_specs=[pl.BlockSpec((tm, d), lambda j, i: (i, 0)),
                  pl.BlockSpec((d, tn), lambda j, i: (0, j)),
                  pl.BlockSpec((1, tn), lambda j, i: (0, j))],
        out_specs=pl.BlockSpec((tm, tn), lambda j, i: (i, j)),
        out_shape=jax.ShapeDtypeStruct((m, n), out_dtype),
        compiler_params=_params("parallel", "parallel"),
        name=name,
    )(n2d, w, bias)


def _gates_kernel(x_ref, w_ref, b_ref, o_ref):
    o_ref[0] = _dot_nt(w_ref[...], x_ref[0]) + b_ref[...]


def _in_proj_gates_rows(n, w_t, bias_col):
    b, tt, d = n.shape
    ng = w_t.shape[0]
    tm = _tile(tt, 2560)
    return pl.pallas_call(
        _gates_kernel,
        grid=(b, tt // tm),
        in_specs=[pl.BlockSpec((1, tm, d), lambda i, j: (i, j, 0)),
                  pl.BlockSpec((ng, d), lambda i, j: (0, 0)),
                  pl.BlockSpec((ng, 1), lambda i, j: (0, 0))],
        out_specs=pl.BlockSpec((1, ng, tm), lambda i, j: (i, 0, j)),
        out_shape=jax.ShapeDtypeStruct((b, ng, tt), F32),
        compiler_params=_params("parallel", "parallel"),
        name="in_proj_gates",
    )(n, w_t, bias_col)


def _chunk_mlp_kernel(u_ref, v_ref, z_ref, ws_ref, bs_ref, y_ref):
    tm, width = v_ref.shape
    gd = width // A_GROUPS
    for c in range(tm // A_CHUNK):
        rows = slice(c * A_CHUNK, (c + 1) * A_CHUNK)
        for g in range(A_GROUPS):
            cols = slice(g * gd, (g + 1) * gd)
            v = v_ref[rows, cols].astype(F32)
            mu = jnp.mean(v, axis=-1, keepdims=True)
            vc = v - mu
            var = jnp.mean(vc * vc, axis=-1, keepdims=True)
            vn = (vc * lax.rsqrt(var + EPS)).astype(BF16)
            mixed = _dot(ws_ref[g], vn) + bs_ref[:, g:g + 1]
            gate = _silu(z_ref[rows, cols].astype(F32))
            y_ref[rows, cols] = (u_ref[rows, cols].astype(F32) * mixed * gate).astype(y_ref.dtype)


def _chunk_mlp(p2d, ws, bs_t, col_u, width, tm=1024):
    m = p2d.shape[0]
    tm = _tile(m, tm)
    cb = col_u // width
    spec = lambda k: pl.BlockSpec((tm, width), lambda i, k=k: (i, cb + k))
    return pl.pallas_call(
        _chunk_mlp_kernel,
        grid=(m // tm,),
        in_specs=[spec(0), spec(1), spec(2),
                  pl.BlockSpec(ws.shape, lambda i: (0, 0, 0)),
                  pl.BlockSpec(bs_t.shape, lambda i: (0, 0))],
        out_specs=pl.BlockSpec((tm, width), lambda i: (i, 0)),
        out_shape=jax.ShapeDtypeStruct((m, width), BF16),
        compiler_params=_params("parallel"),
        name="chunk_mlp",
    )(p2d, p2d, p2d, ws, bs_t)


def _conv_kernel(x_ref, w_ref, o_ref, *, t_lat, scale, transpose):
    x = x_ref[0].astype(F32)
    tt = x.shape[0]
    rows = lax.broadcasted_iota(jnp.int32, x.shape, 0)
    prev = jnp.where((rows == 0) | (rows == t_lat), 0.0, pltpu.roll(x, 1, axis=0))
    nxt = jnp.where((rows == t_lat - 1) | (rows == tt - 1), 0.0, pltpu.roll(x, tt - 1, axis=0))
    w = w_ref[...]
    y = _silu(w[0:1] * prev + w[1:2] * x + w[2:3] * nxt) * scale
    o_ref[0] = (y.T if transpose else y).astype(o_ref.dtype)


def _conv_silu(p3d, conv_w, t_lat, col0, wcol0, width, scale, transpose, tc=256):
    b, tt, _ = p3d.shape
    kern = functools.partial(_conv_kernel, t_lat=t_lat, scale=scale, transpose=transpose)
    if transpose:
        out_spec = pl.BlockSpec((1, tc, tt), lambda i, j: (i, j, 0))
        out_shape = jax.ShapeDtypeStruct((b, width, tt), BF16)
    else:
        out_spec = pl.BlockSpec((1, tt, tc), lambda i, j: (i, 0, j))
        out_shape = jax.ShapeDtypeStruct((b, tt, width), BF16)
    return pl.pallas_call(
        kern,
        grid=(b, width // tc),
        in_specs=[pl.BlockSpec((1, tt, tc), lambda i, j: (i, 0, col0 // tc + j)),
                  pl.BlockSpec((CONV_W, tc), lambda i, j: (0, wcol0 // tc + j))],
        out_specs=out_spec,
        out_shape=out_shape,
        compiler_params=_params("parallel", "parallel"),
        name="conv_k_t" if transpose else "conv_q",
    )(p3d, conv_w)


def _mlstm_kernel(*refs, reverse, readout, hd, gb):
    units = [(g, h) for g in range(gb) for h in range(B_HEADS)]
    nu = len(units)
    refs, c_scr, m_scr = refs[:-2 * nu], refs[-2 * nu:-nu], refs[-nu:]
    if readout:
        q_ref, kt_ref, v_ref, gr_ref, hf_ref, o_ref, z_ref, g_ref, out_ref = refs
    else:
        q_ref, kt_ref, v_ref, gr_ref, out_ref = refs

    @pl.when(pl.program_id(1) == 0)
    def _():
        for scr in c_scr + m_scr:
            scr[...] = jnp.zeros_like(scr)

    L = B_CHUNK
    row = lax.broadcasted_iota(jnp.int32, (L, L), 0)
    col = lax.broadcasted_iota(jnp.int32, (L, L), 1)
    seen = (col >= row) if reverse else (col <= row)
    tri_t = jnp.where((row >= col) if reverse else (row <= col), 1.0, 0.0).astype(BF16)
    last = 0 if reverse else L - 1
    gi, gf = (2 * B_HEADS, 3 * B_HEADS) if reverse else (0, B_HEADS)
    cols = [slice(h * hd, (h + 1) * hd) for h in range(B_HEADS)]
    ones_blk = jnp.ones((L, MLSTM_AUG), BF16)
    lane_tile = lambda x, width: jnp.concatenate([x] * (width // MLSTM_AUG), axis=1)
    gates = lambda lo_row: jnp.concatenate([gr_ref[g, lo_row:lo_row + B_HEADS, :] for g in range(gb)], axis=0)

    i_rows = gates(gi)
    f_rows = _log_sigmoid(gates(gf))
    hi = f_rows.astype(BF16).astype(F32)
    r1 = f_rows - hi
    mid = r1.astype(BF16).astype(F32)
    lo = (r1 - mid).astype(BF16).astype(F32)
    terms = jnp.concatenate([hi, mid, lo, jnp.zeros_like(hi)], axis=0).astype(BF16)
    sums = _dot(terms, tri_t)
    cb_rows = sums[0:nu] + sums[nu:2 * nu] + sums[2 * nu:3 * nu]
    tri = jnp.where(seen, 1.0, 0.0).astype(BF16)
    rep = lambda x, u: jnp.broadcast_to(x[u:u + 1], (MLSTM_AUG, L))
    cb_cols = []
    for u in range(nu):
        c3 = _dot_nt(tri, jnp.concatenate([rep(hi, u), rep(mid, u), rep(lo, u)], axis=0).astype(BF16))
        cb_cols.append(c3[:, 0:MLSTM_AUG] + c3[:, MLSTM_AUG:2 * MLSTM_AUG] + c3[:, 2 * MLSTM_AUG:])

    s_bf, e_inter, emt, ws, ec, m_new = [], [], [], [], [], []
    for u, (g, h) in enumerate(units):
        i_row, cb_row = i_rows[u:u + 1], cb_rows[u:u + 1]
        dmat = jnp.where(seen, lane_tile(cb_cols[u], L) - cb_row + i_row, NEG_BIG)
        m = m_scr[u][0:1, 0:1]
        inter = cb_cols[u] + m
        mt = jnp.maximum(inter, jnp.max(dmat, axis=1, keepdims=True))
        decay = jnp.exp(dmat - lane_tile(mt, L))
        s_bf.append((_dot(q_ref[g, :, cols[h]], kt_ref[g, cols[h], :]) * decay).astype(BF16))
        e_inter.append(jnp.exp(inter - mt))
        emt.append(jnp.exp(-mt))
        cl = cb_row[:, last:last + 1]
        w_log = cl - cb_row + i_row
        m_new.append(jnp.maximum(cl + m, jnp.max(w_log, axis=1, keepdims=True)))
        ec.append(jnp.exp(cl + m - m_new[u]))
        ws.append(jnp.exp(w_log - m_new[u]))

    sv, qc, upd = [], [], []
    for u, (g, h) in enumerate(units):
        v_aug = jnp.concatenate([v_ref[g, :, cols[h]].astype(BF16), ones_blk], axis=1)
        sv.append(_dot(s_bf[u], v_aug))
        qc.append(_dot(q_ref[g, :, cols[h]], c_scr[u][...].astype(BF16)))
        kw_t = (kt_ref[g, cols[h], :].astype(F32) * ws[u]).astype(BF16)
        upd.append(_dot(kw_t, v_aug))
    for u, (g, h) in enumerate(units):
        tot = sv[u] + lane_tile(e_inter[u], hd + MLSTM_AUG) * qc[u]
        inv = 1.0 / jnp.maximum(jnp.abs(tot[:, hd:hd + MLSTM_AUG]), emt[u])
        hc = jnp.concatenate([tot[:, c0:c0 + MLSTM_AUG] * inv for c0 in range(0, hd, MLSTM_AUG)], axis=1)
        c_scr[u][...] = ec[u] * c_scr[u][...] + upd[u]
        m_scr[u][...] = jnp.broadcast_to(m_new[u], m_scr[u].shape)
        if readout:
            hs = hf_ref[g, :, cols[h]] + hc
            hn = hs * lax.rsqrt(jnp.mean(hs * hs, axis=-1, keepdims=True) + EPS) * g_ref[:, cols[h]]
            y = hn * _sigmoid(o_ref[g, :, cols[h]].astype(F32)) * _silu(z_ref[g, :, cols[h]].astype(F32))
            out_ref[g, :, cols[h]] = y.astype(out_ref.dtype)
        else:
            out_ref[g, :, cols[h]] = hc.astype(out_ref.dtype)


def _mlstm_dir(q, k_t, p3d, gates_row, t_lat, cols, reverse, hf=None, norm_g=None, gb=4):
    b, tt, width = q.shape
    gb = max(g for g in range(1, gb + 1) if b % g == 0)
    hd = width // B_HEADS
    L = B_CHUNK
    nch, nlat = tt // L, t_lat // L
    if reverse:
        chunk = lambda i: nch - 1 - i
    else:
        chunk = lambda i: lax.rem(i + nlat, nch)
    wb = lambda name: cols[name] // width
    tok = lambda cb: pl.BlockSpec((gb, L, width), lambda bi, i, cb=cb: (bi, chunk(i), cb))
    in_specs = [tok(0), pl.BlockSpec((gb, width, L), lambda bi, i: (bi, 0, chunk(i))), tok(wb("b_v")),
                pl.BlockSpec((gb, gates_row.shape[1], L), lambda bi, i: (bi, 0, chunk(i)))]
    args = [q, k_t, p3d, gates_row]
    readout = hf is not None
    if readout:
        in_specs += [tok(0), tok(wb("b_o")), tok(wb("b_z")), pl.BlockSpec((1, width), lambda bi, i: (0, 0))]
        args += [hf, p3d, p3d, norm_g]
    kern = functools.partial(_mlstm_kernel, reverse=reverse, readout=readout, hd=hd, gb=gb)
    nu = gb * B_HEADS
    return pl.pallas_call(
        kern,
        grid=(b // gb, nch),
        in_specs=in_specs,
        out_specs=tok(0),
        out_shape=jax.ShapeDtypeStruct((b, tt, width), BF16),
        scratch_shapes=[pltpu.VMEM((hd, hd + MLSTM_AUG), F32)] * nu + [pltpu.VMEM((8, 128), F32)] * nu,
        compiler_params=_params("parallel", "arbitrary"),
        name="mlstm_bwd_readout" if readout else "mlstm_fwd",
    )(*args)


def _hgrn_kernel(*refs, reverse, readout, dv, gb):
    if readout:
        q_ref, z_ref, v_ref, lbf_ref, oml_ref, e_ref, of_ref, cg_ref, g_ref, out_ref, s_scr, row_scr, diag_scr = refs
    else:
        q_ref, z_ref, v_ref, lbf_ref, oml_ref, e_ref, out_ref, s_scr, row_scr, diag_scr = refs

    @pl.when(pl.program_id(1) == 0)
    def _():
        s_scr[...] = jnp.zeros_like(s_scr)

    L, dk, nb = C_CHUNK, C_KEY_DIM, C_CHUNK // C_SUB
    row = lax.broadcasted_iota(jnp.int32, (L, L), 0)
    col = lax.broadcasted_iota(jnp.int32, (L, L), 1)
    rb, cbk = row // C_SUB, col // C_SUB
    if reverse:
        seen, blk_before, last = col >= row, cbk > rb, 0
    else:
        seen, blk_before, last = col <= row, cbk < rb, L - 1
    sum_mat = jnp.concatenate([jnp.where(seen, 1.0, 0.0), jnp.where(blk_before, 1.0, 0.0)], axis=0).astype(BF16)
    same_blk = rb == cbk
    sub = lax.broadcasted_iota(jnp.int32, (L, dk), 0) % C_SUB
    pair_ok = [(sub <= j) if reverse else (sub >= j) for j in range(C_SUB)]

    def bcast_sub(ref, j):
        return jnp.concatenate([jnp.broadcast_to(ref[i * C_SUB + j:i * C_SUB + j + 1, :], (C_SUB, dk))
                                for i in range(nb)], axis=0)

    units = [(g, h) for g in range(gb) for h in range(C_HEADS)]
    kcs = [slice(h * dk, (h + 1) * dk) for h in range(C_HEADS)]
    vcs = [slice(h * dv, (h + 1) * dv) for h in range(C_HEADS)]

    q, k, v, c3 = [], [], [], []
    for u, (g, h) in enumerate(units):
        q.append(_silu(q_ref[g, :, kcs[h]].astype(F32)))
        z = z_ref[g, :, kcs[h]]
        v.append(v_ref[g, :, vcs[h]].astype(BF16))
        a = jnp.exp(-jnp.abs(z))
        r = 1.0 / (1.0 + a)
        pos = z >= 0.0
        oml = oml_ref[:, kcs[h]]
        f = lbf_ref[:, kcs[h]] + oml * jnp.where(pos, r, a * r)
        k.append(oml * jnp.where(pos, a * r, r))
        lf = jnp.log2(f)
        hi = lf.astype(BF16)
        r1 = lf - hi.astype(F32)
        mid = r1.astype(BF16)
        lo = (r1 - mid.astype(F32)).astype(BF16)
        c3.append(_dot(sum_mat, jnp.concatenate([hi, mid, lo], axis=1)))

    cb, entries, qds, a_off = [], [], [], []
    span = jnp.zeros((L, dk), F32)
    for u, (g, h) in enumerate(units):
        c1 = c3[u][:, 0:dk] + c3[u][:, dk:2 * dk] + c3[u][:, 2 * dk:3 * dk]
        cbh, entry = c1[0:L], c1[L:2 * L]
        cb.append(cbh)
        entries.append(entry)
        span = jnp.maximum(span, entry - cbh)
        qd = (q[u] * jnp.exp2(cbh - entry)).astype(BF16)
        qds.append(qd)
        parts = []
        for i in range(nb):
            lo_r, hi_r = ((i + 1) * C_SUB, L) if reverse else (0, i * C_SUB)
            if hi_r == lo_r:
                parts.append(jnp.zeros((C_SUB, L), F32))
                continue
            ent = entry[i * C_SUB:i * C_SUB + 1, :]
            kd = (k[u][lo_r:hi_r] * jnp.exp2(ent - cbh[lo_r:hi_r])).astype(BF16)
            pad = [jnp.zeros((n, dk), BF16) for n in (lo_r, L - hi_r)]
            kd = jnp.concatenate([p for p in (pad[0], kd, pad[1]) if p.shape[0]], axis=0)
            parts.append(_dot_nt(qd[i * C_SUB:(i + 1) * C_SUB], kd))
        a_off.append(jnp.concatenate(parts, axis=0))

    factorised_ok = jnp.max(span) <= HGRN_SAFE_LOG2

    @pl.when(factorised_ok)
    def _():
        for u in range(len(units)):
            kd = (k[u] * jnp.exp2(entries[u] - cb[u])).astype(BF16)
            diag_scr[u] = _dot_nt(qds[u], kd)

    @pl.when(jnp.logical_not(factorised_ok))
    def _():
        for u in range(len(units)):
            qk_parts = []
            cb_rows, k_rows = row_scr.at[2 * u], row_scr.at[2 * u + 1]
            cb_rows[...] = cb[u]
            k_rows[...] = k[u]
            for j in range(C_SUB):
                dec = jnp.exp2(jnp.where(pair_ok[j], cb[u] - bcast_sub(cb_rows, j), NEG_BIG))
                qk_parts.append((q[u] * bcast_sub(k_rows, j) * dec).astype(BF16))
            diag_scr[u] = _dot(jnp.concatenate(qk_parts, axis=1), e_ref[...])

    diag_mask = same_blk & seen
    for u, (g, h) in enumerate(units):
        scores = a_off[u] + jnp.where(diag_mask, diag_scr[u], 0.0)
        st = s_scr[u]
        o = _dot(scores.astype(BF16), v[u]) + _dot_nt((q[u] * jnp.exp2(cb[u])).astype(BF16), st.astype(BF16))
        cl = cb[u][last:last + 1, :]
        kdec = (k[u] * jnp.exp2(cl - cb[u])).astype(BF16)
        s_scr[u] = st * jnp.exp2(cl) + _dot_tn(v[u], kdec)
        if readout:
            os_ = of_ref[g, :, vcs[h]] + o
            on = os_ * lax.rsqrt(jnp.mean(os_ * os_, axis=-1, keepdims=True) + EPS) * g_ref[:, vcs[h]]
            out_ref[g, :, vcs[h]] = (on * _silu(cg_ref[g, :, vcs[h]].astype(F32))).astype(out_ref.dtype)
        else:
            out_ref[g, :, vcs[h]] = o.astype(out_ref.dtype)


def _hgrn_dir(pcq, pcf, lbf, oml, e_mat, cols, t_lat, reverse, of=None, norm_g=None, gb=8):
    b, tt, _ = pcq.shape
    kw, L = C_HEADS * C_KEY_DIM, C_CHUNK
    vw = cols["c_g"] - cols["c_i"]
    dv = vw // C_HEADS
    nch, nlat = tt // L, t_lat // L
    if reverse:
        chunk = lambda i: nch - 1 - i
    else:
        chunk = lambda i: lax.rem(i + nlat, nch)
    gb = max(g for g in range(1, gb + 1) if b % g == 0)
    tok = lambda c0, w: pl.BlockSpec((gb, L, w), lambda bi, i: (bi, chunk(i), c0 // w))
    const2 = lambda x: pl.BlockSpec(x.shape, lambda bi, i: (0, 0))
    f_name = "c_f_bwd" if reverse else "c_f_fwd"
    in_specs = [tok(cols["c_q"], kw), tok(cols[f_name], kw), tok(cols["c_i"], vw),
                const2(lbf), const2(oml), const2(e_mat)]
    args = [pcq, pcf, pcq, lbf, oml, e_mat]
    readout = of is not None
    if readout:
        in_specs += [tok(0, vw), tok(cols["c_g"], vw), const2(norm_g)]
        args += [of, pcq, norm_g]
    kern = functools.partial(_hgrn_kernel, reverse=reverse, readout=readout, dv=dv, gb=gb)
    return pl.pallas_call(
        kern,
        grid=(b // gb, nch),
        in_specs=in_specs,
        out_specs=tok(0, vw),
        out_shape=jax.ShapeDtypeStruct((b, tt, vw), BF16),
        scratch_shapes=[pltpu.VMEM((gb * C_HEADS, dv, C_KEY_DIM), F32),
                        pltpu.VMEM((2 * gb * C_HEADS, L, C_KEY_DIM), F32),
                        pltpu.VMEM((gb * C_HEADS, L, L), F32)],
        compiler_params=_params("parallel", "arbitrary"),
        name="hgrn_bwd_readout" if readout else "hgrn_fwd",
    )(*args)


def _out_kernel(ya_ref, yb_ref, yc_ref, w_ref, h_ref, mod_ref, *rest, final, wa, wb):
    d = h_ref.shape[-1]
    w = w_ref
    y = (_dot(ya_ref[0], w[0:wa, :]) + _dot(yb_ref[0], w[wa:wa + wb, :]) + _dot(yc_ref[0], w[wa + wb:, :]))
    h_new = h_ref[0] + mod_ref[0][:, 2 * d:3 * d] * y
    if final:
        g_ref, out_ref = rest
        out_ref[0] = h_new * lax.rsqrt(jnp.mean(h_new * h_new, axis=-1, keepdims=True) + EPS) * g_ref[...]
    else:
        g_ref, modn_ref, h_out_ref, n_ref = rest
        h_out_ref[0] = h_new
        n_ref[0] = _modulated_norm(h_new, g_ref[...], modn_ref[0], d).astype(n_ref.dtype)


def _out_proj(ya, yb, yc, w_out, h, mod, g_next, mod_next, t_lat, final, tm=256):
    b, tt, d = h.shape
    wa, wb, wc = ya.shape[-1], yb.shape[-1], yc.shape[-1]
    if final:
        tm = _tile(t_lat, 512)
    tok = lambda w: pl.BlockSpec((1, tm, w), lambda i, j: (i, j, 0))
    mod_spec = pl.BlockSpec((1, 1, 3 * d), _mod_index(t_lat // tm))
    in_specs = [tok(wa), tok(wb), tok(wc), pl.BlockSpec(w_out.shape, lambda i, j: (0, 0)), tok(d), mod_spec,
                pl.BlockSpec((1, d), lambda i, j: (0, 0))]
    args = [ya, yb, yc, w_out, h, mod, g_next]
    kern = functools.partial(_out_kernel, final=final, wa=wa, wb=wb)
    if final:
        return pl.pallas_call(
            kern, grid=(b, t_lat // tm), in_specs=in_specs, out_specs=tok(d),
            out_shape=jax.ShapeDtypeStruct((b, t_lat, d), F32),
            compiler_params=_params("parallel", "parallel"), name="out_proj_final",
        )(*args)
    return pl.pallas_call(
        kern, grid=(b, tt // tm), in_specs=in_specs + [mod_spec], out_specs=(tok(d), tok(d)),
        out_shape=(jax.ShapeDtypeStruct((b, tt, d), F32), jax.ShapeDtypeStruct((b, tt, d), BF16)),
        compiler_params=_params("parallel", "parallel"), name="out_proj",
    )(*args, mod_next)


def _packed_layout(d):
    a, bw, c, kq = d // 4, d // 2, d // 4, C_HEADS * C_KEY_DIM
    ref_order = (("a_u", a), ("a_v", a), ("a_z", a), ("b_q", bw), ("b_k", bw), ("b_v", bw), ("b_o", bw),
                 ("b_z", bw), ("gates", 4 * B_HEADS), ("c_q", kq), ("c_f_fwd", kq), ("c_f_bwd", kq),
                 ("c_i", c), ("c_g", c))
    src, start = {}, 0
    for name, w in ref_order:
        src[name] = (start, w)
        start += w
    groups = {"ab": ("b_q", "b_k", "b_v", "b_o", "b_z", "a_u", "a_v", "a_z"),
              "cq": ("c_q", "c_i", "c_g"), "cf": ("c_f_fwd", "c_f_bwd")}
    cols = {}
    for names in groups.values():
        pos = 0
        for name in names:
            cols[name] = pos
            pos += src[name][1]
    return src, groups, cols


def _pack_cols(w, bias, src, names):
    spans = []
    for k in names:
        lo, width = src[k]
        if spans and spans[-1][1] == lo:
            spans[-1][1] = lo + width
        else:
            spans.append([lo, lo + width])
    pick = lambda a: jnp.concatenate([a[..., lo:hi] for lo, hi in spans], axis=-1)
    return pick(w).astype(BF16), pick(bias)[None, :]


BF16_ROWS = 16
LANES = 128


def _grid_transpose_kernel(x_ref, o_ref, scr, *, width):
    lanes = scr.shape[-1]
    pitch = scr.shape[1] // BF16_ROWS
    for c in range(scr.shape[0]):
        for r in range(BF16_ROWS):
            scr[c, r * pitch:r * pitch + width, :] = x_ref[0, r * width:(r + 1) * width,
                                                           c * lanes:(c + 1) * lanes].astype(F32)
    for w in range(width):
        rows = [scr[c, pl.ds(w, BF16_ROWS, stride=pitch), :] for c in range(scr.shape[0])]
        o_ref[0, w] = jnp.concatenate(rows, axis=1).astype(o_ref.dtype)


def _copy_kernel(x_ref, dst_ref, o_ref):
    del dst_ref
    o_ref[...] = x_ref[...]


def _grid_transpose(x, t_lat, rows, width):
    b, tt, f = x.shape
    t_ctx = tt - t_lat
    assert rows % BF16_ROWS == 0 and tt % rows == 0 and t_lat % t_ctx == 0
    tile = BF16_ROWS * width
    lat = pl.pallas_call(
        functools.partial(_grid_transpose_kernel, width=width),
        grid=(b, rows // BF16_ROWS),
        in_specs=[pl.BlockSpec((1, tile, f), lambda i, j: (i, j, 0))],
        out_specs=pl.BlockSpec((1, width, BF16_ROWS, f), lambda i, j: (i, 0, j, 0)),
        out_shape=jax.ShapeDtypeStruct((b, tt // rows, rows, f), x.dtype),
        scratch_shapes=[pltpu.VMEM((f // LANES, BF16_ROWS * (width + 8), LANES), F32)],
        compiler_params=_params("parallel", "parallel"),
        name="grid_transpose",
    )(x).reshape(b, tt, f)
    ctx_spec = pl.BlockSpec((1, t_ctx, f), lambda i: (i, t_lat // t_ctx, 0))
    return pl.pallas_call(
        _copy_kernel,
        grid=(b,),
        in_specs=[ctx_spec, pl.BlockSpec(memory_space=pl.ANY)],
        out_specs=ctx_spec,
        out_shape=jax.ShapeDtypeStruct((b, tt, f), x.dtype),
        input_output_aliases={1: 0},
        compiler_params=_params("parallel"),
        name="context_rows_copy",
    )(x, lat)


def kernel(x, c, ctx, c_ctx, w_ada, b_ada, norm_g, w_in, b_in, w_spatial, b_spatial, conv_qk, mlstm_norm,
           hgrn_lb_logits, hgrn_norm, w_out, final_norm):
    b, t_lat, d = x.shape
    t_ctx = ctx.shape[1]
    tt = t_lat + t_ctx
    depth = w_ada.shape[0]
    src, groups, cols = _packed_layout(d)
    a_width, b_width = d // 4, d // 2
    rows = t_lat // GRID_W

    r_pad = -(-(b + 1) // 8) * 8
    cond = jnp.concatenate([c, c_ctx[None, :], jnp.zeros((r_pad - b - 1, d), F32)], axis=0)
    mod_all = _ada_mod(cond, w_ada, b_ada)
    mods = [jnp.stack([mod_all[l, :b], jnp.broadcast_to(mod_all[l, b], (b, 3 * d))], axis=1).reshape(2 * b, 1, 3 * d)
            for l in range(depth)]

    lbf, oml = _lower_bounds(hgrn_lb_logits.astype(F32))

    e_rows = lax.broadcasted_iota(jnp.int32, (C_SUB * C_KEY_DIM, C_CHUNK), 0) // C_KEY_DIM
    e_cols = lax.broadcasted_iota(jnp.int32, (C_SUB * C_KEY_DIM, C_CHUNK), 1) % C_SUB
    e_mat = (e_rows == e_cols).astype(BF16)

    h, n = _join_norm_mod(x, ctx, norm_g[0:1], mods[0])
    out = None
    for l in range(depth):
        last = l == depth - 1
        w_ab, b_ab = _pack_cols(w_in[l], b_in[l], src, groups["ab"])
        w_cq, b_cq = _pack_cols(w_in[l], b_in[l], src, groups["cq"])
        w_cf, b_cf = _pack_cols(w_in[l], b_in[l], src, groups["cf"])
        g0, gw = src["gates"]
        w_gate_t = w_in[l][:, g0:g0 + gw].T.astype(BF16)
        b_gate = b_in[l][g0:g0 + gw][:, None]

        n_cm = _grid_transpose(n, t_lat, rows, GRID_W)
        n2d, n_cm2d = n.reshape(b * tt, d), n_cm.reshape(b * tt, d)
        p2d = _in_proj(n2d, w_ab, b_ab, 2048, 1664, "in_proj_ab", BF16)
        pcq = _in_proj(n_cm2d, w_cq, b_cq, 1024, 1536, "in_proj_cq", BF16).reshape(b, tt, -1)
        pcf = _in_proj(n_cm2d, w_cf, b_cf, 1024, 1024, "in_proj_cf").reshape(b, tt, -1)
        gates_row = _in_proj_gates_rows(n, w_gate_t, b_gate)
        p3d = p2d.reshape(b, tt, -1)

        ya = _chunk_mlp(p2d, w_spatial[l].astype(BF16), b_spatial[l].T, cols["a_u"], a_width).reshape(b, tt, a_width)

        q = _conv_silu(p3d, conv_qk[l], t_lat, cols["b_q"], 0, b_width, 1.0, transpose=False)
        k_t = _conv_silu(p3d, conv_qk[l], t_lat, cols["b_k"], b_width, b_width,
                         (b_width // B_HEADS) ** -0.5, transpose=True)
        hf = _mlstm_dir(q, k_t, p3d, gates_row, t_lat, cols, reverse=False)
        yb = _mlstm_dir(q, k_t, p3d, gates_row, t_lat, cols, reverse=True, hf=hf, norm_g=mlstm_norm[l:l + 1])

        lbf_l, oml_l = lbf[l:l + 1], oml[l:l + 1]
        of = _hgrn_dir(pcq, pcf, lbf_l, oml_l, e_mat, cols, t_lat, False)
        yc_cm = _hgrn_dir(pcq, pcf, lbf_l, oml_l, e_mat, cols, t_lat, True, of=of, norm_g=hgrn_norm[l:l + 1])
        yc = _grid_transpose(yc_cm, t_lat, GRID_W, rows)

        w_o = w_out[l].astype(BF16)
        if last:
            out = _out_proj(ya, yb, yc, w_o, h, mods[l], final_norm[None, :], None, t_lat, final=True)
        else:
            h, n = _out_proj(ya, yb, yc, w_o, h, mods[l], norm_g[l + 1:l + 2], mods[l + 1], t_lat, final=False)
    return out
```

```python
import functools

import jax
import jax.numpy as jnp
from jax import lax
from jax.experimental import pallas as pl
from jax.experimental.pallas import tpu as pltpu

EPS = 1e-6
NEG_BIG = -1e30
LB_FLOOR = 1e-30
GRID_W = 64
CONV_W = 3

A_GROUPS = 4
A_CHUNK = 128
B_HEADS = 4
B_CHUNK = 256
C_HEADS = 4
C_KEY_DIM = 128
C_CHUNK = 64
C_SUB = 32
HGRN_SAFE_LOG2 = 100.0
MLSTM_AUG = 128

V7X_VMEM_LIMIT = 56 * 1024 * 1024

F32 = jnp.float32
BF16 = jnp.bfloat16


def _params(*sem):
    return pltpu.CompilerParams(dimension_semantics=sem, vmem_limit_bytes=V7X_VMEM_LIMIT)


def _sigmoid(x):
    return 1.0 / (1.0 + jnp.exp(-x))


def _silu(x):
    return x * _sigmoid(x)


def _log_sigmoid(x):
    return jnp.minimum(x, 0.0) - jnp.log1p(jnp.exp(-jnp.abs(x)))


def _dot(a, b):
    return jnp.dot(a, b, preferred_element_type=F32)


def _dot_nt(a, b):
    return lax.dot_general(a, b, (((1,), (1,)), ((), ())), preferred_element_type=F32)


def _dot_tn(a, b):
    return lax.dot_general(a, b, (((0,), (0,)), ((), ())), preferred_element_type=F32)


def _lb_kernel(x_ref, lbf_ref, oml_ref):
    x = x_ref[...]
    depth = x.shape[0]
    e = jnp.exp(x - jnp.max(x, axis=0, keepdims=True))
    p = e / jnp.sum(e, axis=0, keepdims=True)
    rows = lax.broadcasted_iota(jnp.int32, x.shape, 0)
    lb = jnp.zeros_like(x)
    for j in range(1, depth):
        lb = lb + jnp.where(rows >= j, p[j:j + 1, :], 0.0)
    lbf_ref[...] = jnp.maximum(lb, LB_FLOOR)
    oml_ref[...] = 1.0 - lb


def _lower_bounds(logits):
    shp = jax.ShapeDtypeStruct(logits.shape, F32)
    return pl.pallas_call(_lb_kernel, out_shape=(shp, shp), name="hgrn_lower_bounds")(logits)


def _ada_kernel(c_ref, w_ref, b_ref, o_ref):
    s = _silu(c_ref[...]).astype(BF16)
    o_ref[0] = _dot(s, w_ref[0].astype(BF16)) + b_ref[0]


def _ada_mod(cond, w_ada, b_ada, tn=768):
    depth, d, n3 = w_ada.shape
    r = cond.shape[0]
    return pl.pallas_call(
        _ada_kernel,
        grid=(depth, n3 // tn),
        in_specs=[pl.BlockSpec((r, d), lambda l, j: (0, 0)),
                  pl.BlockSpec((1, d, tn), lambda l, j: (l, 0, j)),
                  pl.BlockSpec((1, 1, tn), lambda l, j: (l, 0, j))],
        out_specs=pl.BlockSpec((1, r, tn), lambda l, j: (l, 0, j)),
        out_shape=jax.ShapeDtypeStruct((depth, r, n3), F32),
        compiler_params=_params("parallel", "parallel"),
        name="adaln_mod",
    )(cond, w_ada, b_ada.reshape(depth, 1, n3))


def _modulated_norm(h, g, mod, d):
    y = h * lax.rsqrt(jnp.mean(h * h, axis=-1, keepdims=True) + EPS) * g
    return y * (1.0 + mod[:, d:2 * d]) + mod[:, 0:d]


def _norm_kernel(x_ref, ctx_ref, g_ref, mod_ref, h_ref, n_ref, *, tiles_lat):
    d = x_ref.shape[-1]

    def emit(src_ref):
        h = src_ref[0]
        h_ref[0] = h
        n_ref[0] = _modulated_norm(h, g_ref[...], mod_ref[0], d).astype(n_ref.dtype)

    is_lat = pl.program_id(1) < tiles_lat
    pl.when(is_lat)(lambda: emit(x_ref))
    pl.when(jnp.logical_not(is_lat))(lambda: emit(ctx_ref))


def _mod_index(tiles_lat):
    return lambda b, j: (2 * b + jnp.where(j >= tiles_lat, 1, 0), 0, 0)


def _join_norm_mod(x, ctx, g, mod, tm=256):
    b, t_lat, d = x.shape
    tt = t_lat + ctx.shape[1]
    tl = t_lat // tm
    tok = pl.BlockSpec((1, tm, d), lambda i, j: (i, j, 0))
    return pl.pallas_call(
        functools.partial(_norm_kernel, tiles_lat=tl),
        grid=(b, tt // tm),
        in_specs=[pl.BlockSpec((1, tm, d), lambda i, j: (i, jnp.minimum(j, tl - 1), 0)),
                  pl.BlockSpec((1, tm, d), lambda i, j: (i, jnp.maximum(j - tl, 0), 0)),
                  pl.BlockSpec((1, d), lambda i, j: (0, 0)),
                  pl.BlockSpec((1, 1, 3 * d), _mod_index(tl))],
        out_specs=(tok, tok),
        out_shape=(jax.ShapeDtypeStruct((b, tt, d), F32), jax.ShapeDtypeStruct((b, tt, d), BF16)),
        compiler_params=_params("parallel", "arbitrary"),
        name="join_norm_mod",
    )(x, ctx, g, mod)


def _matmul_bias_kernel(x_ref, w_ref, b_ref, o_ref):
    o_ref[...] = (_dot(x_ref[...], w_ref[...]) + b_ref[...]).astype(o_ref.dtype)


def _tile(m, pref, unit=128):
    t = min(pref, m) // unit * unit
    while m % t:
        t -= unit
    return t


def _in_proj(n2d, w, bias, tm, tn, name, out_dtype=F32):
    m, d = n2d.shape
    n = w.shape[1]
    tm = _tile(m, tm)
    return pl.pallas_call(
        _matmul_bias_kernel,
        grid=(n // tn, m // tm),
        in_specs=[pl.BlockSpec((tm, d), lambda j, i: (i, 0)),
                  pl.BlockSpec((d, tn), lambda j, i: (0, j)),
                  pl.BlockSpec((1, tn), lambda j, i: (0, j))],
        out_specs=pl.BlockSpec((tm, tn), lambda j, i: (i, j)),
        out_shape=jax.ShapeDtypeStruct((m, n), out_dtype),
        compiler_params=_params("parallel", "parallel"),
        name=name,
    )(n2d, w, bias)


def _gates_kernel(x_ref, w_ref, b_ref, o_ref):
    o_ref[0] = _dot_nt(w_ref[...], x_ref[0]) + b_ref[...]


def _in_proj_gates_rows(n, w_t, bias_col):
    b, tt, d = n.shape
    ng = w_t.shape[0]
    tm = _tile(tt, 2560)
    return pl.pallas_call(
        _gates_kernel,
        grid=(b, tt // tm),
        in_specs=[pl.BlockSpec((1, tm, d), lambda i, j: (i, j, 0)),
                  pl.BlockSpec((ng, d), lambda i, j: (0, 0)),
                  pl.BlockSpec((ng, 1), lambda i, j: (0, 0))],
        out_specs=pl.BlockSpec((1, ng, tm), lambda i, j: (i, 0, j)),
        out_shape=jax.ShapeDtypeStruct((b, ng, tt), F32),
        compiler_params=_params("parallel", "parallel"),
        name="in_proj_gates",
    )(n, w_t, bias_col)


def _chunk_mlp_kernel(u_ref, v_ref, z_ref, ws_ref, bs_ref, y_ref):
    tm, width = v_ref.shape
    gd = width // A_GROUPS
    for c in range(tm // A_CHUNK):
        rows = slice(c * A_CHUNK, (c + 1) * A_CHUNK)
        for g in range(A_GROUPS):
            cols = slice(g * gd, (g + 1) * gd)
            v = v_ref[rows, cols].astype(F32)
            mu = jnp.mean(v, axis=-1, keepdims=True)
            vc = v - mu
            var = jnp.mean(vc * vc, axis=-1, keepdims=True)
            vn = (vc * lax.rsqrt(var + EPS)).astype(BF16)
            mixed = _dot(ws_ref[g], vn) + bs_ref[:, g:g + 1]
            gate = _silu(z_ref[rows, cols].astype(F32))
            y_ref[rows, cols] = (u_ref[rows, cols].astype(F32) * mixed * gate).astype(y_ref.dtype)


def _chunk_mlp(p2d, ws, bs_t, col_u, width, tm=1024):
    m = p2d.shape[0]
    tm = _tile(m, tm)
    cb = col_u // width
    spec = lambda k: pl.BlockSpec((tm, width), lambda i, k=k: (i, cb + k))
    return pl.pallas_call(
        _chunk_mlp_kernel,
        grid=(m // tm,),
        in_specs=[spec(0), spec(1), spec(2),
                  pl.BlockSpec(ws.shape, lambda i: (0, 0, 0)),
                  pl.BlockSpec(bs_t.shape, lambda i: (0, 0))],
        out_specs=pl.BlockSpec((tm, width), lambda i: (i, 0)),
        out_shape=jax.ShapeDtypeStruct((m, width), BF16),
        compiler_params=_params("parallel"),
        name="chunk_mlp",
    )(p2d, p2d, p2d, ws, bs_t)


def _conv_kernel(x_ref, w_ref, o_ref, *, t_lat, scale, transpose):
    x = x_ref[0].astype(F32)
    tt = x.shape[0]
    rows = lax.broadcasted_iota(jnp.int32, x.shape, 0)
    prev = jnp.where((rows == 0) | (rows == t_lat), 0.0, pltpu.roll(x, 1, axis=0))
    nxt = jnp.where((rows == t_lat - 1) | (rows == tt - 1), 0.0, pltpu.roll(x, tt - 1, axis=0))
    w = w_ref[...]
    y = _silu(w[0:1] * prev + w[1:2] * x + w[2:3] * nxt) * scale
    o_ref[0] = (y.T if transpose else y).astype(o_ref.dtype)


def _conv_silu(p3d, conv_w, t_lat, col0, wcol0, width, scale, transpose, tc=256):
    b, tt, _ = p3d.shape
    kern = functools.partial(_conv_kernel, t_lat=t_lat, scale=scale, transpose=transpose)
    if transpose:
        out_spec = pl.BlockSpec((1, tc, tt), lambda i, j: (i, j, 0))
        out_shape = jax.ShapeDtypeStruct((b, width, tt), BF16)
    else:
        out_spec = pl.BlockSpec((1, tt, tc), lambda i, j: (i, 0, j))
        out_shape = jax.ShapeDtypeStruct((b, tt, width), BF16)
    return pl.pallas_call(
        kern,
        grid=(b, width // tc),
        in_specs=[pl.BlockSpec((1, tt, tc), lambda i, j: (i, 0, col0 // tc + j)),
                  pl.BlockSpec((CONV_W, tc), lambda i, j: (0, wcol0 // tc + j))],
        out_specs=out_spec,
        out_shape=out_shape,
        compiler_params=_params("parallel", "parallel"),
        name="conv_k_t" if transpose else "conv_q",
    )(p3d, conv_w)


def _mlstm_kernel(*refs, reverse, readout, hd, gb):
    units = [(g, h) for g in range(gb) for h in range(B_HEADS)]
    nu = len(units)
    refs, c_scr, m_scr = refs[:-2 * nu], refs[-2 * nu:-nu], refs[-nu:]
    if readout:
        q_ref, kt_ref, v_ref, gr_ref, hf_ref, o_ref, z_ref, g_ref, out_ref = refs
    else:
        q_ref, kt_ref, v_ref, gr_ref, out_ref = refs

    @pl.when(pl.program_id(1) == 0)
    def _():
        for scr in c_scr + m_scr:
            scr[...] = jnp.zeros_like(scr)

    L = B_CHUNK
    row = lax.broadcasted_iota(jnp.int32, (L, L), 0)
    col = lax.broadcasted_iota(jnp.int32, (L, L), 1)
    seen = (col >= row) if reverse else (col <= row)
    tri_t = jnp.where((row >= col) if reverse else (row <= col), 1.0, 0.0).astype(BF16)
    last = 0 if reverse else L - 1
    gi, gf = (2 * B_HEADS, 3 * B_HEADS) if reverse else (0, B_HEADS)
    cols = [slice(h * hd, (h + 1) * hd) for h in range(B_HEADS)]
    ones_blk = jnp.ones((L, MLSTM_AUG), BF16)
    lane_tile = lambda x, width: jnp.concatenate([x] * (width // MLSTM_AUG), axis=1)
    gates = lambda lo_row: jnp.concatenate([gr_ref[g, lo_row:lo_row + B_HEADS, :] for g in range(gb)], axis=0)

    i_rows = gates(gi)
    f_rows = _log_sigmoid(gates(gf))
    hi = f_rows.astype(BF16).astype(F32)
    r1 = f_rows - hi
    mid = r1.astype(BF16).astype(F32)
    lo = (r1 - mid).astype(BF16).astype(F32)
    terms = jnp.concatenate([hi, mid, lo, jnp.zeros_like(hi)], axis=0).astype(BF16)
    sums = _dot(terms, tri_t)
    cb_rows = sums[0:nu] + sums[nu:2 * nu] + sums[2 * nu:3 * nu]
    tri = jnp.where(seen, 1.0, 0.0).astype(BF16)
    rep = lambda x, u: jnp.broadcast_to(x[u:u + 1], (MLSTM_AUG, L))
    cb_cols = []
    for u in range(nu):
        c3 = _dot_nt(tri, jnp.concatenate([rep(hi, u), rep(mid, u), rep(lo, u)], axis=0).astype(BF16))
        cb_cols.append(c3[:, 0:MLSTM_AUG] + c3[:, MLSTM_AUG:2 * MLSTM_AUG] + c3[:, 2 * MLSTM_AUG:])

    s_bf, e_inter, emt, ws, ec, m_new = [], [], [], [], [], []
    for u, (g, h) in enumerate(units):
        i_row, cb_row = i_rows[u:u + 1], cb_rows[u:u + 1]
        dmat = jnp.where(seen, lane_tile(cb_cols[u], L) - cb_row + i_row, NEG_BIG)
        m = m_scr[u][0:1, 0:1]
        inter = cb_cols[u] + m
        mt = jnp.maximum(inter, jnp.max(dmat, axis=1, keepdims=True))
        decay = jnp.exp(dmat - lane_tile(mt, L))
        s_bf.append((_dot(q_ref[g, :, cols[h]], kt_ref[g, cols[h], :]) * decay).astype(BF16))
        e_inter.append(jnp.exp(inter - mt))
        emt.append(jnp.exp(-mt))
        cl = cb_row[:, last:last + 1]
        w_log = cl - cb_row + i_row
        m_new.append(jnp.maximum(cl + m, jnp.max(w_log, axis=1, keepdims=True)))
        ec.append(jnp.exp(cl + m - m_new[u]))
        ws.append(jnp.exp(w_log - m_new[u]))

    sv, qc, upd = [], [], []
    for u, (g, h) in enumerate(units):
        v_aug = jnp.concatenate([v_ref[g, :, cols[h]].astype(BF16), ones_blk], axis=1)
        sv.append(_dot(s_bf[u], v_aug))
        qc.append(_dot(q_ref[g, :, cols[h]], c_scr[u][...].astype(BF16)))
        kw_t = (kt_ref[g, cols[h], :].astype(F32) * ws[u]).astype(BF16)
        upd.append(_dot(kw_t, v_aug))
    for u, (g, h) in enumerate(units):
        tot = sv[u] + lane_tile(e_inter[u], hd + MLSTM_AUG) * qc[u]
        inv = 1.0 / jnp.maximum(jnp.abs(tot[:, hd:hd + MLSTM_AUG]), emt[u])
        hc = jnp.concatenate([tot[:, c0:c0 + MLSTM_AUG] * inv for c0 in range(0, hd, MLSTM_AUG)], axis=1)
        c_scr[u][...] = ec[u] * c_scr[u][...] + upd[u]
        m_scr[u][...] = jnp.broadcast_to(m_new[u], m_scr[u].shape)
        if readout:
            hs = hf_ref[g, :, cols[h]] + hc
            hn = hs * lax.rsqrt(jnp.mean(hs * hs, axis=-1, keepdims=True) + EPS) * g_ref[:, cols[h]]
            o, z = o_ref[g, :, cols[h]].astype(F32), z_ref[g, :, cols[h]].astype(F32)
            y = hn * z / ((1.0 + jnp.exp(-o)) * (1.0 + jnp.exp(-z)))
            out_ref[g, :, cols[h]] = y.astype(out_ref.dtype)
        else:
            out_ref[g, :, cols[h]] = hc.astype(out_ref.dtype)


def _mlstm_dir(q, k_t, p3d, gates_row, t_lat, cols, reverse, hf=None, norm_g=None, gb=4):
    b, tt, width = q.shape
    gb = max(g for g in range(1, gb + 1) if b % g == 0)
    hd = width // B_HEADS
    L = B_CHUNK
    nch, nlat = tt // L, t_lat // L
    if reverse:
        chunk = lambda i: nch - 1 - i
    else:
        chunk = lambda i: lax.rem(i + nlat, nch)
    wb = lambda name: cols[name] // width
    tok = lambda cb: pl.BlockSpec((gb, L, width), lambda bi, i, cb=cb: (bi, chunk(i), cb))
    in_specs = [tok(0), pl.BlockSpec((gb, width, L), lambda bi, i: (bi, 0, chunk(i))), tok(wb("b_v")),
                pl.BlockSpec((gb, gates_row.shape[1], L), lambda bi, i: (bi, 0, chunk(i)))]
    args = [q, k_t, p3d, gates_row]
    readout = hf is not None
    if readout:
        in_specs += [tok(0), tok(wb("b_o")), tok(wb("b_z")), pl.BlockSpec((1, width), lambda bi, i: (0, 0))]
        args += [hf, p3d, p3d, norm_g]
    kern = functools.partial(_mlstm_kernel, reverse=reverse, readout=readout, hd=hd, gb=gb)
    nu = gb * B_HEADS
    return pl.pallas_call(
        kern,
        grid=(b // gb, nch),
        in_specs=in_specs,
        out_specs=tok(0),
        out_shape=jax.ShapeDtypeStruct((b, tt, width), BF16),
        scratch_shapes=[pltpu.VMEM((hd, hd + MLSTM_AUG), F32)] * nu + [pltpu.VMEM((8, 128), F32)] * nu,
        compiler_params=_params("parallel", "arbitrary"),
        name="mlstm_bwd_readout" if readout else "mlstm_fwd",
    )(*args)


def _hgrn_kernel(*refs, reverse, readout, dv, gb):
    if readout:
        q_ref, z_ref, v_ref, lbf_ref, oml_ref, e_ref, of_ref, cg_ref, g_ref, out_ref, s_scr, row_scr, diag_scr = refs
    else:
        q_ref, z_ref, v_ref, lbf_ref, oml_ref, e_ref, out_ref, s_scr, row_scr, diag_scr = refs

    @pl.when(pl.program_id(1) == 0)
    def _():
        s_scr[...] = jnp.zeros_like(s_scr)

    L, dk, nb = C_CHUNK, C_KEY_DIM, C_CHUNK // C_SUB
    row = lax.broadcasted_iota(jnp.int32, (L, L), 0)
    col = lax.broadcasted_iota(jnp.int32, (L, L), 1)
    rb, cbk = row // C_SUB, col // C_SUB
    if reverse:
        seen, blk_before, last = col >= row, cbk > rb, 0
    else:
        seen, blk_before, last = col <= row, cbk < rb, L - 1
    sum_mat = jnp.concatenate([jnp.where(seen, 1.0, 0.0), jnp.where(blk_before, 1.0, 0.0)], axis=0).astype(BF16)
    same_blk = rb == cbk
    sub = lax.broadcasted_iota(jnp.int32, (L, dk), 0) % C_SUB
    pair_ok = [(sub <= j) if reverse else (sub >= j) for j in range(C_SUB)]

    def bcast_sub(ref, j):
        return jnp.concatenate([jnp.broadcast_to(ref[i * C_SUB + j:i * C_SUB + j + 1, :], (C_SUB, dk))
                                for i in range(nb)], axis=0)

    units = [(g, h) for g in range(gb) for h in range(C_HEADS)]
    kcs = [slice(h * dk, (h + 1) * dk) for h in range(C_HEADS)]
    vcs = [slice(h * dv, (h + 1) * dv) for h in range(C_HEADS)]

    q, k, v, c3 = [], [], [], []
    for u, (g, h) in enumerate(units):
        q.append(_silu(q_ref[g, :, kcs[h]].astype(F32)))
        z = z_ref[g, :, kcs[h]]
        v.append(v_ref[g, :, vcs[h]].astype(BF16))
        a = jnp.exp(-jnp.abs(z))
        r = 1.0 / (1.0 + a)
        pos = z >= 0.0
        oml = oml_ref[:, kcs[h]]
        f = lbf_ref[:, kcs[h]] + oml * jnp.where(pos, r, a * r)
        k.append(oml * jnp.where(pos, a * r, r))
        lf = jnp.log2(f)
        hi = lf.astype(BF16)
        r1 = lf - hi.astype(F32)
        mid = r1.astype(BF16)
        lo = (r1 - mid.astype(F32)).astype(BF16)
        c3.append(_dot(sum_mat, jnp.concatenate([hi, mid, lo], axis=1)))

    cb, entries, qds, a_off = [], [], [], []
    span = jnp.zeros((L, dk), F32)
    for u, (g, h) in enumerate(units):
        c1 = c3[u][:, 0:dk] + c3[u][:, dk:2 * dk] + c3[u][:, 2 * dk:3 * dk]
        cbh, entry = c1[0:L], c1[L:2 * L]
        cb.append(cbh)
        entries.append(entry)
        span = jnp.maximum(span, entry - cbh)
        qd = (q[u] * jnp.exp2(cbh - entry)).astype(BF16)
        qds.append(qd)
        parts = []
        for i in range(nb):
            lo_r, hi_r = ((i + 1) * C_SUB, L) if reverse else (0, i * C_SUB)
            if hi_r == lo_r:
                parts.append(jnp.zeros((C_SUB, L), F32))
                continue
            ent = entry[i * C_SUB:i * C_SUB + 1, :]
            kd = (k[u][lo_r:hi_r] * jnp.exp2(ent - cbh[lo_r:hi_r])).astype(BF16)
            pad = [jnp.zeros((n, dk), BF16) for n in (lo_r, L - hi_r)]
            kd = jnp.concatenate([p for p in (pad[0], kd, pad[1]) if p.shape[0]], axis=0)
            parts.append(_dot_nt(qd[i * C_SUB:(i + 1) * C_SUB], kd))
        a_off.append(jnp.concatenate(parts, axis=0))

    factorised_ok = jnp.max(span) <= HGRN_SAFE_LOG2

    @pl.when(factorised_ok)
    def _():
        for u in range(len(units)):
            kd = (k[u] * jnp.exp2(entries[u] - cb[u])).astype(BF16)
            diag_scr[u] = _dot_nt(qds[u], kd)

    @pl.when(jnp.logical_not(factorised_ok))
    def _():
        for u in range(len(units)):
            qk_parts = []
            cb_rows, k_rows = row_scr.at[2 * u], row_scr.at[2 * u + 1]
            cb_rows[...] = cb[u]
            k_rows[...] = k[u]
            for j in range(C_SUB):
                dec = jnp.exp2(jnp.where(pair_ok[j], cb[u] - bcast_sub(cb_rows, j), NEG_BIG))
                qk_parts.append((q[u] * bcast_sub(k_rows, j) * dec).astype(BF16))
            diag_scr[u] = _dot(jnp.concatenate(qk_parts, axis=1), e_ref[...])

    diag_mask = same_blk & seen
    for u, (g, h) in enumerate(units):
        scores = a_off[u] + jnp.where(diag_mask, diag_scr[u], 0.0)
        st = s_scr[u]
        o = _dot(scores.astype(BF16), v[u]) + _dot_nt((q[u] * jnp.exp2(cb[u])).astype(BF16), st.astype(BF16))
        cl = cb[u][last:last + 1, :]
        kdec = (k[u] * jnp.exp2(cl - cb[u])).astype(BF16)
        s_scr[u] = st * jnp.exp2(cl) + _dot_tn(v[u], kdec)
        if readout:
            os_ = of_ref[g, :, vcs[h]] + o
            on = os_ * lax.rsqrt(jnp.mean(os_ * os_, axis=-1, keepdims=True) + EPS) * g_ref[:, vcs[h]]
            out_ref[g, :, vcs[h]] = (on * _silu(cg_ref[g, :, vcs[h]].astype(F32))).astype(out_ref.dtype)
        else:
            out_ref[g, :, vcs[h]] = o.astype(out_ref.dtype)


def _hgrn_dir(pcq, pcf, lbf, oml, e_mat, cols, t_lat, reverse, of=None, norm_g=None, gb=8):
    b, tt, _ = pcq.shape
    kw, L = C_HEADS * C_KEY_DIM, C_CHUNK
    vw = cols["c_g"] - cols["c_i"]
    dv = vw // C_HEADS
    nch, nlat = tt // L, t_lat // L
    if reverse:
        chunk = lambda i: nch - 1 - i
    else:
        chunk = lambda i: lax.rem(i + nlat, nch)
    gb = max(g for g in range(1, gb + 1) if b % g == 0)
    tok = lambda c0, w: pl.BlockSpec((gb, L, w), lambda bi, i: (bi, chunk(i), c0 // w))
    const2 = lambda x: pl.BlockSpec(x.shape, lambda bi, i: (0, 0))
    f_name = "c_f_bwd" if reverse else "c_f_fwd"
    in_specs = [tok(cols["c_q"], kw), tok(cols[f_name], kw), tok(cols["c_i"], vw),
                const2(lbf), const2(oml), const2(e_mat)]
    args = [pcq, pcf, pcq, lbf, oml, e_mat]
    readout = of is not None
    if readout:
        in_specs += [tok(0, vw), tok(cols["c_g"], vw), const2(norm_g)]
        args += [of, pcq, norm_g]
    kern = functools.partial(_hgrn_kernel, reverse=reverse, readout=readout, dv=dv, gb=gb)
    return pl.pallas_call(
        kern,
        grid=(b // gb, nch),
        in_specs=in_specs,
        out_specs=tok(0, vw),
        out_shape=jax.ShapeDtypeStruct((b, tt, vw), BF16),
        scratch_shapes=[pltpu.VMEM((gb * C_HEADS, dv, C_KEY_DIM), F32),
                        pltpu.VMEM((2 * gb * C_HEADS, L, C_KEY_DIM), F32),
                        pltpu.VMEM((gb * C_HEADS, L, L), F32)],
        compiler_params=_params("parallel", "arbitrary"),
        name="hgrn_bwd_readout" if readout else "hgrn_fwd",
    )(*args)


def _out_kernel(ya_ref, yb_ref, yc_ref, w_ref, h_ref, mod_ref, *rest, final, wa, wb):
    d = h_ref.shape[-1]
    w = w_ref
    y = (_dot(ya_ref[0], w[0:wa, :]) + _dot(yb_ref[0], w[wa:wa + wb, :]) + _dot(yc_ref[0], w[wa + wb:, :]))
    h_new = h_ref[0] + mod_ref[0][:, 2 * d:3 * d] * y
    if final:
        g_ref, out_ref = rest
        out_ref[0] = h_new * lax.rsqrt(jnp.mean(h_new * h_new, axis=-1, keepdims=True) + EPS) * g_ref[...]
    else:
        g_ref, modn_ref, h_out_ref, n_ref = rest
        h_out_ref[0] = h_new
        n_ref[0] = _modulated_norm(h_new, g_ref[...], modn_ref[0], d).astype(n_ref.dtype)


def _out_proj(ya, yb, yc, w_out, h, mod, g_next, mod_next, t_lat, final, tm=256):
    b, tt, d = h.shape
    wa, wb, wc = ya.shape[-1], yb.shape[-1], yc.shape[-1]
    if final:
        tm = _tile(t_lat, 512)
    tok = lambda w: pl.BlockSpec((1, tm, w), lambda i, j: (i, j, 0))
    mod_spec = pl.BlockSpec((1, 1, 3 * d), _mod_index(t_lat // tm))
    in_specs = [tok(wa), tok(wb), tok(wc), pl.BlockSpec(w_out.shape, lambda i, j: (0, 0)), tok(d), mod_spec,
                pl.BlockSpec((1, d), lambda i, j: (0, 0))]
    args = [ya, yb, yc, w_out, h, mod, g_next]
    kern = functools.partial(_out_kernel, final=final, wa=wa, wb=wb)
    if final:
        return pl.pallas_call(
            kern, grid=(b, t_lat // tm), in_specs=in_specs, out_specs=tok(d),
            out_shape=jax.ShapeDtypeStruct((b, t_lat, d), F32),
            compiler_params=_params("parallel", "parallel"), name="out_proj_final",
        )(*args)
    return pl.pallas_call(
        kern, grid=(b, tt // tm), in_specs=in_specs + [mod_spec], out_specs=(tok(d), tok(d)),
        out_shape=(jax.ShapeDtypeStruct((b, tt, d), F32), jax.ShapeDtypeStruct((b, tt, d), BF16)),
        compiler_params=_params("parallel", "parallel"), name="out_proj",
    )(*args, mod_next)


def _packed_layout(d):
    a, bw, c, kq = d // 4, d // 2, d // 4, C_HEADS * C_KEY_DIM
    ref_order = (("a_u", a), ("a_v", a), ("a_z", a), ("b_q", bw), ("b_k", bw), ("b_v", bw), ("b_o", bw),
                 ("b_z", bw), ("gates", 4 * B_HEADS), ("c_q", kq), ("c_f_fwd", kq), ("c_f_bwd", kq),
                 ("c_i", c), ("c_g", c))
    src, start = {}, 0
    for name, w in ref_order:
        src[name] = (start, w)
        start += w
    groups = {"ab": ("b_q", "b_k", "b_v", "b_o", "b_z", "a_u", "a_v", "a_z"),
              "cq": ("c_q", "c_i", "c_g"), "cf": ("c_f_fwd", "c_f_bwd")}
    cols = {}
    for names in groups.values():
        pos = 0
        for name in names:
            cols[name] = pos
            pos += src[name][1]
    return src, groups, cols


def _pack_cols(w, bias, src, names):
    spans = []
    for k in names:
        lo, width = src[k]
        if spans and spans[-1][1] == lo:
            spans[-1][1] = lo + width
        else:
            spans.append([lo, lo + width])
    pick = lambda a: jnp.concatenate([a[..., lo:hi] for lo, hi in spans], axis=-1)
    return pick(w).astype(BF16), pick(bias)[None, :]


BF16_ROWS = 16
LANES = 128


def _grid_transpose_kernel(x_ref, o_ref, scr, *, width):
    lanes = scr.shape[-1]
    pitch = scr.shape[1] // BF16_ROWS
    for c in range(scr.shape[0]):
        for r in range(BF16_ROWS):
            scr[c, r * pitch:r * pitch + width, :] = x_ref[0, r * width:(r + 1) * width,
                                                           c * lanes:(c + 1) * lanes].astype(F32)
    for w in range(width):
        rows = [scr[c, pl.ds(w, BF16_ROWS, stride=pitch), :] for c in range(scr.shape[0])]
        o_ref[0, w] = jnp.concatenate(rows, axis=1).astype(o_ref.dtype)


def _copy_kernel(x_ref, dst_ref, o_ref):
    del dst_ref
    o_ref[...] = x_ref[...]


def _grid_transpose(x, t_lat, rows, width):
    b, tt, f = x.shape
    t_ctx = tt - t_lat
    assert rows % BF16_ROWS == 0 and tt % rows == 0 and t_lat % t_ctx == 0
    tile = BF16_ROWS * width
    lat = pl.pallas_call(
        functools.partial(_grid_transpose_kernel, width=width),
        grid=(b, rows // BF16_ROWS),
        in_specs=[pl.BlockSpec((1, tile, f), lambda i, j: (i, j, 0))],
        out_specs=pl.BlockSpec((1, width, BF16_ROWS, f), lambda i, j: (i, 0, j, 0)),
        out_shape=jax.ShapeDtypeStruct((b, tt // rows, rows, f), x.dtype),
        scratch_shapes=[pltpu.VMEM((f // LANES, BF16_ROWS * (width + 8), LANES), F32)],
        compiler_params=_params("parallel", "parallel"),
        name="grid_transpose",
    )(x).reshape(b, tt, f)
    ctx_spec = pl.BlockSpec((1, t_ctx, f), lambda i: (i, t_lat // t_ctx, 0))
    return pl.pallas_call(
        _copy_kernel,
        grid=(b,),
        in_specs=[ctx_spec, pl.BlockSpec(memory_space=pl.ANY)],
        out_specs=ctx_spec,
        out_shape=jax.ShapeDtypeStruct((b, tt, f), x.dtype),
        input_output_aliases={1: 0},
        compiler_params=_params("parallel"),
        name="context_rows_copy",
    )(x, lat)


def kernel(x, c, ctx, c_ctx, w_ada, b_ada, norm_g, w_in, b_in, w_spatial, b_spatial, conv_qk, mlstm_norm,
           hgrn_lb_logits, hgrn_norm, w_out, final_norm):
    b, t_lat, d = x.shape
    t_ctx = ctx.shape[1]
    tt = t_lat + t_ctx
    depth = w_ada.shape[0]
    src, groups, cols = _packed_layout(d)
    a_width, b_width = d // 4, d // 2
    rows = t_lat // GRID_W

    r_pad = -(-(b + 1) // 8) * 8
    cond = jnp.concatenate([c, c_ctx[None, :], jnp.zeros((r_pad - b - 1, d), F32)], axis=0)
    mod_all = _ada_mod(cond, w_ada, b_ada)
    mods = [jnp.stack([mod_all[l, :b], jnp.broadcast_to(mod_all[l, b], (b, 3 * d))], axis=1).reshape(2 * b, 1, 3 * d)
            for l in range(depth)]

    lbf, oml = _lower_bounds(hgrn_lb_logits.astype(F32))

    e_rows = lax.broadcasted_iota(jnp.int32, (C_SUB * C_KEY_DIM, C_CHUNK), 0) // C_KEY_DIM
    e_cols = lax.broadcasted_iota(jnp.int32, (C_SUB * C_KEY_DIM, C_CHUNK), 1) % C_SUB
    e_mat = (e_rows == e_cols).astype(BF16)

    h, n = _join_norm_mod(x, ctx, norm_g[0:1], mods[0])
    out = None
    for l in range(depth):
        last = l == depth - 1
        w_ab, b_ab = _pack_cols(w_in[l], b_in[l], src, groups["ab"])
        w_cq, b_cq = _pack_cols(w_in[l], b_in[l], src, groups["cq"])
        w_cf, b_cf = _pack_cols(w_in[l], b_in[l], src, groups["cf"])
        g0, gw = src["gates"]
        w_gate_t = w_in[l][:, g0:g0 + gw].T.astype(BF16)
        b_gate = b_in[l][g0:g0 + gw][:, None]

        n_cm = _grid_transpose(n, t_lat, rows, GRID_W)
        n2d, n_cm2d = n.reshape(b * tt, d), n_cm.reshape(b * tt, d)
        p2d = _in_proj(n2d, w_ab, b_ab, 2048, 1664, "in_proj_ab", BF16)
        pcq = _in_proj(n_cm2d, w_cq, b_cq, 2048, 1536, "in_proj_cq", BF16).reshape(b, tt, -1)
        pcf = _in_proj(n_cm2d, w_cf, b_cf, 2048, 1024, "in_proj_cf").reshape(b, tt, -1)
        gates_row = _in_proj_gates_rows(n, w_gate_t, b_gate)
        p3d = p2d.reshape(b, tt, -1)

        ya = _chunk_mlp(p2d, w_spatial[l].astype(BF16), b_spatial[l].T, cols["a_u"], a_width).reshape(b, tt, a_width)

        q = _conv_silu(p3d, conv_qk[l], t_lat, cols["b_q"], 0, b_width, 1.0, transpose=False)
        k_t = _conv_silu(p3d, conv_qk[l], t_lat, cols["b_k"], b_width, b_width,
                         (b_width // B_HEADS) ** -0.5, transpose=True)
        hf = _mlstm_dir(q, k_t, p3d, gates_row, t_lat, cols, reverse=False)
        yb = _mlstm_dir(q, k_t, p3d, gates_row, t_lat, cols, reverse=True, hf=hf, norm_g=mlstm_norm[l:l + 1])

        lbf_l, oml_l = lbf[l:l + 1], oml[l:l + 1]
        of = _hgrn_dir(pcq, pcf, lbf_l, oml_l, e_mat, cols, t_lat, False)
        yc_cm = _hgrn_dir(pcq, pcf, lbf_l, oml_l, e_mat, cols, t_lat, True, of=of, norm_g=hgrn_norm[l:l + 1])
        yc = _grid_transpose(yc_cm, t_lat, GRID_W, rows)

        w_o = w_out[l].astype(BF16)
        if last:
            out = _out_proj(ya, yb, yc, w_o, h, mods[l], final_norm[None, :], None, t_lat, final=True)
        else:
            h, n = _out_proj(ya, yb, yc, w_o, h, mods[l], norm_g[l + 1:l + 2], mods[l + 1], t_lat, final=False)
    return out
```

```python
import functools

import jax
import jax.numpy as jnp
from jax import lax
from jax.experimental import pallas as pl
from jax.experimental.pallas import tpu as pltpu

EPS = 1e-6
NEG_BIG = -1e30
LB_FLOOR = 1e-30
GRID_W = 64
CONV_W = 3

A_GROUPS = 4
A_CHUNK = 128
B_HEADS = 4
B_CHUNK = 256
C_HEADS = 4
C_KEY_DIM = 128
C_CHUNK = 128
C_SUB = 32
HGRN_SAFE_LOG2 = 100.0
MLSTM_AUG = 128

V7X_VMEM_LIMIT = 56 * 1024 * 1024

F32 = jnp.float32
BF16 = jnp.bfloat16


def _params(*sem):
    return pltpu.CompilerParams(dimension_semantics=sem, vmem_limit_bytes=V7X_VMEM_LIMIT)


def _sigmoid(x):
    return 1.0 / (1.0 + jnp.exp(-x))


def _silu(x):
    return x * _sigmoid(x)


def _log_sigmoid(x):
    return jnp.minimum(x, 0.0) - jnp.log1p(jnp.exp(-jnp.abs(x)))


def _dot(a, b):
    return jnp.dot(a, b, preferred_element_type=F32)


def _dot_nt(a, b):
    return lax.dot_general(a, b, (((1,), (1,)), ((), ())), preferred_element_type=F32)


def _dot_tn(a, b):
    return lax.dot_general(a, b, (((0,), (0,)), ((), ())), preferred_element_type=F32)


def _lb_kernel(x_ref, lbf_ref, oml_ref):
    x = x_ref[...]
    depth = x.shape[0]
    e = jnp.exp(x - jnp.max(x, axis=0, keepdims=True))
    p = e / jnp.sum(e, axis=0, keepdims=True)
    rows = lax.broadcasted_iota(jnp.int32, x.shape, 0)
    lb = jnp.zeros_like(x)
    for j in range(1, depth):
        lb = lb + jnp.where(rows >= j, p[j:j + 1, :], 0.0)
    lbf_ref[...] = jnp.maximum(lb, LB_FLOOR)
    oml_ref[...] = 1.0 - lb


def _lower_bounds(logits):
    shp = jax.ShapeDtypeStruct(logits.shape, F32)
    return pl.pallas_call(_lb_kernel, out_shape=(shp, shp), name="hgrn_lower_bounds")(logits)


def _ada_kernel(c_ref, w_ref, b_ref, o_ref):
    s = _silu(c_ref[...]).astype(BF16)
    o_ref[0] = _dot(s, w_ref[0].astype(BF16)) + b_ref[0]


def _ada_mod(cond, w_ada, b_ada, tn=768):
    depth, d, n3 = w_ada.shape
    r = cond.shape[0]
    return pl.pallas_call(
        _ada_kernel,
        grid=(depth, n3 // tn),
        in_specs=[pl.BlockSpec((r, d), lambda l, j: (0, 0)),
                  pl.BlockSpec((1, d, tn), lambda l, j: (l, 0, j)),
                  pl.BlockSpec((1, 1, tn), lambda l, j: (l, 0, j))],
        out_specs=pl.BlockSpec((1, r, tn), lambda l, j: (l, 0, j)),
        out_shape=jax.ShapeDtypeStruct((depth, r, n3), F32),
        compiler_params=_params("parallel", "parallel"),
        name="adaln_mod",
    )(cond, w_ada, b_ada.reshape(depth, 1, n3))


def _modulated_norm(h, g, mod, d):
    y = h * lax.rsqrt(jnp.mean(h * h, axis=-1, keepdims=True) + EPS) * g
    return y * (1.0 + mod[:, d:2 * d]) + mod[:, 0:d]


def _norm_kernel(x_ref, ctx_ref, g_ref, mod_ref, h_ref, n_ref, *, tiles_lat):
    d = x_ref.shape[-1]

    def emit(src_ref):
        h = src_ref[0]
        h_ref[0] = h
        n_ref[0] = _modulated_norm(h, g_ref[...], mod_ref[0], d).astype(n_ref.dtype)

    is_lat = pl.program_id(1) < tiles_lat
    pl.when(is_lat)(lambda: emit(x_ref))
    pl.when(jnp.logical_not(is_lat))(lambda: emit(ctx_ref))


def _mod_index(tiles_lat):
    return lambda b, j: (2 * b + jnp.where(j >= tiles_lat, 1, 0), 0, 0)


def _join_norm_mod(x, ctx, g, mod, tm=256):
    b, t_lat, d = x.shape
    tt = t_lat + ctx.shape[1]
    tl = t_lat // tm
    tok = pl.BlockSpec((1, tm, d), lambda i, j: (i, j, 0))
    return pl.pallas_call(
        functools.partial(_norm_kernel, tiles_lat=tl),
        grid=(b, tt // tm),
        in_specs=[pl.BlockSpec((1, tm, d), lambda i, j: (i, jnp.minimum(j, tl - 1), 0)),
                  pl.BlockSpec((1, tm, d), lambda i, j: (i, jnp.maximum(j - tl, 0), 0)),
                  pl.BlockSpec((1, d), lambda i, j: (0, 0)),
                  pl.BlockSpec((1, 1, 3 * d), _mod_index(tl))],
        out_specs=(tok, tok),
        out_shape=(jax.ShapeDtypeStruct((b, tt, d), F32), jax.ShapeDtypeStruct((b, tt, d), BF16)),
        compiler_params=_params("parallel", "arbitrary"),
        name="join_norm_mod",
    )(x, ctx, g, mod)


def _matmul_bias_kernel(x_ref, w_ref, b_ref, o_ref):
    o_ref[...] = (_dot(x_ref[...], w_ref[...]) + b_ref[...]).astype(o_ref.dtype)


def _tile(m, pref, unit=128):
    t = min(pref, m) // unit * unit
    while m % t:
        t -= unit
    return t


def _in_proj(n2d, w, bias, tm, tn, name, out_dtype=F32):
    m, d = n2d.shape
    n = w.shape[1]
    tm = _tile(m, tm)
    return pl.pallas_call(
        _matmul_bias_kernel,
        grid=(n // tn, m // tm),
        in_specs=[pl.BlockSpec((tm, d), lambda j, i: (i, 0)),
                  pl.BlockSpec((d, tn), lambda j, i: (0, j)),
                  pl.BlockSpec((1, tn), lambda j, i: (0, j))],
        out_specs=pl.BlockSpec((tm, tn), lambda j, i: (i, j)),
        out_shape=jax.ShapeDtypeStruct((m, n), out_dtype),
        compiler_params=_params("parallel", "parallel"),
        name=name,
    )(n2d, w, bias)


def _gates_kernel(x_ref, w_ref, b_ref, o_ref):
    o_ref[0] = _dot_nt(w_ref[...], x_ref[0]) + b_ref[...]


def _in_proj_gates_rows(n, w_t, bias_col):
    b, tt, d = n.shape
    ng = w_t.shape[0]
    tm = _tile(tt, 2560)
    return pl.pallas_call(
        _gates_kernel,
        grid=(b, tt // tm),
        in_specs=[pl.BlockSpec((1, tm, d), lambda i, j: (i, j, 0)),
                  pl.BlockSpec((ng, d), lambda i, j: (0, 0)),
                  pl.BlockSpec((ng, 1), lambda i, j: (0, 0))],
        out_specs=pl.BlockSpec((1, ng, tm), lambda i, j: (i, 0, j)),
        out_shape=jax.ShapeDtypeStruct((b, ng, tt), F32),
        compiler_params=_params("parallel", "parallel"),
        name="in_proj_gates",
    )(n, w_t, bias_col)


def _chunk_mlp_kernel(u_ref, v_ref, z_ref, ws_ref, bs_ref, y_ref):
    tm, width = v_ref.shape
    gd = width // A_GROUPS
    for c in range(tm // A_CHUNK):
        rows = slice(c * A_CHUNK, (c + 1) * A_CHUNK)
        for g in range(A_GROUPS):
            cols = slice(g * gd, (g + 1) * gd)
            v = v_ref[rows, cols].astype(F32)
            mu = jnp.mean(v, axis=-1, keepdims=True)
            vc = v - mu
            var = jnp.mean(vc * vc, axis=-1, keepdims=True)
            vn = (vc * lax.rsqrt(var + EPS)).astype(BF16)
            mixed = _dot(ws_ref[g], vn) + bs_ref[:, g:g + 1]
            gate = _silu(z_ref[rows, cols].astype(F32))
            y_ref[rows, cols] = (u_ref[rows, cols].astype(F32) * mixed * gate).astype(y_ref.dtype)


def _chunk_mlp(p2d, ws, bs_t, col_u, width, tm=1024):
    m = p2d.shape[0]
    tm = _tile(m, tm)
    cb = col_u // width
    spec = lambda k: pl.BlockSpec((tm, width), lambda i, k=k: (i, cb + k))
    return pl.pallas_call(
        _chunk_mlp_kernel,
        grid=(m // tm,),
        in_specs=[spec(0), spec(1), spec(2),
                  pl.BlockSpec(ws.shape, lambda i: (0, 0, 0)),
                  pl.BlockSpec(bs_t.shape, lambda i: (0, 0))],
        out_specs=pl.BlockSpec((tm, width), lambda i: (i, 0)),
        out_shape=jax.ShapeDtypeStruct((m, width), BF16),
        compiler_params=_params("parallel"),
        name="chunk_mlp",
    )(p2d, p2d, p2d, ws, bs_t)


def _conv_kernel(x_ref, w_ref, o_ref, *, t_lat, scale, transpose):
    x = x_ref[0].astype(F32)
    tt = x.shape[0]
    rows = lax.broadcasted_iota(jnp.int32, x.shape, 0)
    prev = jnp.where((rows == 0) | (rows == t_lat), 0.0, pltpu.roll(x, 1, axis=0))
    nxt = jnp.where((rows == t_lat - 1) | (rows == tt - 1), 0.0, pltpu.roll(x, tt - 1, axis=0))
    w = w_ref[...]
    y = _silu(w[0:1] * prev + w[1:2] * x + w[2:3] * nxt) * scale
    o_ref[0] = (y.T if transpose else y).astype(o_ref.dtype)


def _conv_silu(p3d, conv_w, t_lat, col0, wcol0, width, scale, transpose, tc=256):
    b, tt, _ = p3d.shape
    kern = functools.partial(_conv_kernel, t_lat=t_lat, scale=scale, transpose=transpose)
    if transpose:
        out_spec = pl.BlockSpec((1, tc, tt), lambda i, j: (i, j, 0))
        out_shape = jax.ShapeDtypeStruct((b, width, tt), BF16)
    else:
        out_spec = pl.BlockSpec((1, tt, tc), lambda i, j: (i, 0, j))
        out_shape = jax.ShapeDtypeStruct((b, tt, width), BF16)
    return pl.pallas_call(
        kern,
        grid=(b, width // tc),
        in_specs=[pl.BlockSpec((1, tt, tc), lambda i, j: (i, 0, col0 // tc + j)),
                  pl.BlockSpec((CONV_W, tc), lambda i, j: (0, wcol0 // tc + j))],
        out_specs=out_spec,
        out_shape=out_shape,
        compiler_params=_params("parallel", "parallel"),
        name="conv_k_t" if transpose else "conv_q",
    )(p3d, conv_w)


def _mlstm_kernel(*refs, reverse, readout, hd, gb):
    units = [(g, h) for g in range(gb) for h in range(B_HEADS)]
    nu = len(units)
    refs, c_scr, m_scr = refs[:-2 * nu], refs[-2 * nu:-nu], refs[-nu:]
    if readout:
        q_ref, kt_ref, v_ref, gr_ref, hf_ref, o_ref, z_ref, g_ref, out_ref = refs
    else:
        q_ref, kt_ref, v_ref, gr_ref, out_ref = refs

    @pl.when(pl.program_id(1) == 0)
    def _():
        for scr in c_scr + m_scr:
            scr[...] = jnp.zeros_like(scr)

    L = B_CHUNK
    row = lax.broadcasted_iota(jnp.int32, (L, L), 0)
    col = lax.broadcasted_iota(jnp.int32, (L, L), 1)
    seen = (col >= row) if reverse else (col <= row)
    tri_t = jnp.where((row >= col) if reverse else (row <= col), 1.0, 0.0).astype(BF16)
    last = 0 if reverse else L - 1
    gi, gf = (2 * B_HEADS, 3 * B_HEADS) if reverse else (0, B_HEADS)
    cols = [slice(h * hd, (h + 1) * hd) for h in range(B_HEADS)]
    ones_blk = jnp.ones((L, MLSTM_AUG), BF16)
    lane_tile = lambda x, width: jnp.concatenate([x] * (width // MLSTM_AUG), axis=1)
    gates = lambda lo_row: jnp.concatenate([gr_ref[g, lo_row:lo_row + B_HEADS, :] for g in range(gb)], axis=0)

    i_rows = gates(gi)
    f_rows = _log_sigmoid(gates(gf))
    hi = f_rows.astype(BF16).astype(F32)
    r1 = f_rows - hi
    mid = r1.astype(BF16).astype(F32)
    lo = (r1 - mid).astype(BF16).astype(F32)
    terms = jnp.concatenate([hi, mid, lo, jnp.zeros_like(hi)], axis=0).astype(BF16)
    sums = _dot(terms, tri_t)
    cb_rows = sums[0:nu] + sums[nu:2 * nu] + sums[2 * nu:3 * nu]
    tri = jnp.where(seen, 1.0, 0.0).astype(BF16)
    rep = lambda x, u: jnp.broadcast_to(x[u:u + 1], (MLSTM_AUG, L))
    cb_cols = []
    for u in range(nu):
        c3 = _dot_nt(tri, jnp.concatenate([rep(hi, u), rep(mid, u), rep(lo, u)], axis=0).astype(BF16))
        cb_cols.append(c3[:, 0:MLSTM_AUG] + c3[:, MLSTM_AUG:2 * MLSTM_AUG] + c3[:, 2 * MLSTM_AUG:])

    s_bf, e_inter, emt, ws, ec, m_new = [], [], [], [], [], []
    for u, (g, h) in enumerate(units):
        i_row, cb_row = i_rows[u:u + 1], cb_rows[u:u + 1]
        dmat = jnp.where(seen, lane_tile(cb_cols[u], L) - cb_row + i_row, NEG_BIG)
        m = m_scr[u][0:1, 0:1]
        inter = cb_cols[u] + m
        mt = jnp.maximum(inter, jnp.max(dmat, axis=1, keepdims=True))
        decay = jnp.exp(dmat - lane_tile(mt, L))
        s_bf.append((_dot(q_ref[g, :, cols[h]], kt_ref[g, cols[h], :]) * decay).astype(BF16))
        e_inter.append(jnp.exp(inter - mt))
        emt.append(jnp.exp(-mt))
        cl = cb_row[:, last:last + 1]
        w_log = cl - cb_row + i_row
        m_new.append(jnp.maximum(cl + m, jnp.max(w_log, axis=1, keepdims=True)))
        ec.append(jnp.exp(cl + m - m_new[u]))
        ws.append(jnp.exp(w_log - m_new[u]))

    sv, qc, upd = [], [], []
    for u, (g, h) in enumerate(units):
        v_aug = jnp.concatenate([v_ref[g, :, cols[h]].astype(BF16), ones_blk], axis=1)
        sv.append(_dot(s_bf[u], v_aug))
        qc.append(_dot(q_ref[g, :, cols[h]], c_scr[u][...].astype(BF16)))
        kw_t = (kt_ref[g, cols[h], :].astype(F32) * ws[u]).astype(BF16)
        upd.append(_dot(kw_t, v_aug))
    for u, (g, h) in enumerate(units):
        tot = sv[u] + lane_tile(e_inter[u], hd + MLSTM_AUG) * qc[u]
        inv = 1.0 / jnp.maximum(jnp.abs(tot[:, hd:hd + MLSTM_AUG]), emt[u])
        hc = jnp.concatenate([tot[:, c0:c0 + MLSTM_AUG] * inv for c0 in range(0, hd, MLSTM_AUG)], axis=1)
        c_scr[u][...] = ec[u] * c_scr[u][...] + upd[u]
        m_scr[u][...] = jnp.broadcast_to(m_new[u], m_scr[u].shape)
        if readout:
            hs = hf_ref[g, :, cols[h]] + hc
            hn = hs * lax.rsqrt(jnp.mean(hs * hs, axis=-1, keepdims=True) + EPS) * g_ref[:, cols[h]]
            o, z = o_ref[g, :, cols[h]].astype(F32), z_ref[g, :, cols[h]].astype(F32)
            y = hn * z / ((1.0 + jnp.exp(-o)) * (1.0 + jnp.exp(-z)))
            out_ref[g, :, cols[h]] = y.astype(out_ref.dtype)
        else:
            out_ref[g, :, cols[h]] = hc.astype(out_ref.dtype)


def _mlstm_dir(q, k_t, p3d, gates_row, t_lat, cols, reverse, hf=None, norm_g=None, gb=4):
    b, tt, width = q.shape
    gb = max(g for g in range(1, gb + 1) if b % g == 0)
    hd = width // B_HEADS
    L = B_CHUNK
    nch, nlat = tt // L, t_lat // L
    if reverse:
        chunk = lambda i: nch - 1 - i
    else:
        chunk = lambda i: lax.rem(i + nlat, nch)
    wb = lambda name: cols[name] // width
    tok = lambda cb: pl.BlockSpec((gb, L, width), lambda bi, i, cb=cb: (bi, chunk(i), cb))
    in_specs = [tok(0), pl.BlockSpec((gb, width, L), lambda bi, i: (bi, 0, chunk(i))), tok(wb("b_v")),
                pl.BlockSpec((gb, gates_row.shape[1], L), lambda bi, i: (bi, 0, chunk(i)))]
    args = [q, k_t, p3d, gates_row]
    readout = hf is not None
    if readout:
        in_specs += [tok(0), tok(wb("b_o")), tok(wb("b_z")), pl.BlockSpec((1, width), lambda bi, i: (0, 0))]
        args += [hf, p3d, p3d, norm_g]
    kern = functools.partial(_mlstm_kernel, reverse=reverse, readout=readout, hd=hd, gb=gb)
    nu = gb * B_HEADS
    return pl.pallas_call(
        kern,
        grid=(b // gb, nch),
        in_specs=in_specs,
        out_specs=tok(0),
        out_shape=jax.ShapeDtypeStruct((b, tt, width), BF16),
        scratch_shapes=[pltpu.VMEM((hd, hd + MLSTM_AUG), F32)] * nu + [pltpu.VMEM((8, 128), F32)] * nu,
        compiler_params=_params("parallel", "arbitrary"),
        name="mlstm_bwd_readout" if readout else "mlstm_fwd",
    )(*args)


def _hgrn_kernel(*refs, reverse, readout, dv, gb):
    if readout:
        q_ref, z_ref, v_ref, lbf_ref, oml_ref, e_ref, of_ref, cg_ref, g_ref, out_ref, s_scr, row_scr, diag_scr = refs
    else:
        q_ref, z_ref, v_ref, lbf_ref, oml_ref, e_ref, out_ref, s_scr, row_scr, diag_scr = refs

    @pl.when(pl.program_id(1) == 0)
    def _():
        s_scr[...] = jnp.zeros_like(s_scr)

    L, dk, nb = C_CHUNK, C_KEY_DIM, C_CHUNK // C_SUB
    row = lax.broadcasted_iota(jnp.int32, (L, L), 0)
    col = lax.broadcasted_iota(jnp.int32, (L, L), 1)
    rb, cbk = row // C_SUB, col // C_SUB
    if reverse:
        seen, blk_before, last = col >= row, cbk > rb, 0
    else:
        seen, blk_before, last = col <= row, cbk < rb, L - 1
    sum_mat = jnp.concatenate([jnp.where(seen, 1.0, 0.0), jnp.where(blk_before, 1.0, 0.0)], axis=0).astype(BF16)
    same_blk = rb == cbk
    sub = lax.broadcasted_iota(jnp.int32, (L, dk), 0) % C_SUB
    pair_ok = [(sub <= j) if reverse else (sub >= j) for j in range(C_SUB)]

    def bcast_sub(ref, j):
        return jnp.concatenate([jnp.broadcast_to(ref[i * C_SUB + j:i * C_SUB + j + 1, :], (C_SUB, dk))
                                for i in range(nb)], axis=0)

    units = [(g, h) for g in range(gb) for h in range(C_HEADS)]
    kcs = [slice(h * dk, (h + 1) * dk) for h in range(C_HEADS)]
    vcs = [slice(h * dv, (h + 1) * dv) for h in range(C_HEADS)]

    q, k, v, c3 = [], [], [], []
    for u, (g, h) in enumerate(units):
        q.append(_silu(q_ref[g, :, kcs[h]].astype(F32)))
        z = z_ref[g, :, kcs[h]]
        v.append(v_ref[g, :, vcs[h]].astype(BF16))
        a = jnp.exp(-jnp.abs(z))
        r = 1.0 / (1.0 + a)
        pos = z >= 0.0
        oml = oml_ref[:, kcs[h]]
        f = lbf_ref[:, kcs[h]] + oml * jnp.where(pos, r, a * r)
        k.append(oml * jnp.where(pos, a * r, r))
        lf = jnp.log2(f)
        hi = lf.astype(BF16)
        r1 = lf - hi.astype(F32)
        mid = r1.astype(BF16)
        lo = (r1 - mid.astype(F32)).astype(BF16)
        c3.append(_dot(sum_mat, jnp.concatenate([hi, mid, lo], axis=1)))

    cb, entries, qds, a_off = [], [], [], []
    span = jnp.zeros((L, dk), F32)
    for u, (g, h) in enumerate(units):
        c1 = c3[u][:, 0:dk] + c3[u][:, dk:2 * dk] + c3[u][:, 2 * dk:3 * dk]
        cbh, entry = c1[0:L], c1[L:2 * L]
        cb.append(cbh)
        entries.append(entry)
        span = jnp.maximum(span, entry - cbh)
        qd = (q[u] * jnp.exp2(cbh - entry)).astype(BF16)
        qds.append(qd)
        parts = []
        for i in range(nb):
            lo_r, hi_r = ((i + 1) * C_SUB, L) if reverse else (0, i * C_SUB)
            if hi_r == lo_r:
                parts.append(jnp.zeros((C_SUB, L), F32))
                continue
            ent = entry[i * C_SUB:i * C_SUB + 1, :]
            kd = (k[u][lo_r:hi_r] * jnp.exp2(ent - cbh[lo_r:hi_r])).astype(BF16)
            pad = [jnp.zeros((n, dk), BF16) for n in (lo_r, L - hi_r)]
            kd = jnp.concatenate([p for p in (pad[0], kd, pad[1]) if p.shape[0]], axis=0)
            parts.append(_dot_nt(qd[i * C_SUB:(i + 1) * C_SUB], kd))
        a_off.append(jnp.concatenate(parts, axis=0))

    factorised_ok = jnp.max(span) <= HGRN_SAFE_LOG2

    @pl.when(factorised_ok)
    def _():
        for u in range(len(units)):
            kd = (k[u] * jnp.exp2(entries[u] - cb[u])).astype(BF16)
            diag_scr[u] = _dot_nt(qds[u], kd)

    @pl.when(jnp.logical_not(factorised_ok))
    def _():
        for u in range(len(units)):
            qk_parts = []
            cb_rows, k_rows = row_scr.at[2 * u], row_scr.at[2 * u + 1]
            cb_rows[...] = cb[u]
            k_rows[...] = k[u]
            for j in range(C_SUB):
                dec = jnp.exp2(jnp.where(pair_ok[j], cb[u] - bcast_sub(cb_rows, j), NEG_BIG))
                qk_parts.append((q[u] * bcast_sub(k_rows, j) * dec).astype(BF16))
            diag_scr[u] = _dot(jnp.concatenate(qk_parts, axis=1), e_ref[...])

    diag_mask = same_blk & seen
    for u, (g, h) in enumerate(units):
        scores = a_off[u] + jnp.where(diag_mask, diag_scr[u], 0.0)
        st = s_scr[u]
        o = _dot(scores.astype(BF16), v[u]) + _dot_nt((q[u] * jnp.exp2(cb[u])).astype(BF16), st.astype(BF16))
        cl = cb[u][last:last + 1, :]
        kdec = (k[u] * jnp.exp2(cl - cb[u])).astype(BF16)
        s_scr[u] = st * jnp.exp2(cl) + _dot_tn(v[u], kdec)
        if readout:
            os_ = of_ref[g, :, vcs[h]] + o
            on = os_ * lax.rsqrt(jnp.mean(os_ * os_, axis=-1, keepdims=True) + EPS) * g_ref[:, vcs[h]]
            out_ref[g, :, vcs[h]] = (on * _silu(cg_ref[g, :, vcs[h]].astype(F32))).astype(out_ref.dtype)
        else:
            out_ref[g, :, vcs[h]] = o.astype(out_ref.dtype)


def _hgrn_dir(pcq, pcf, lbf, oml, e_mat, cols, t_lat, reverse, of=None, norm_g=None, gb=8):
    b, tt, _ = pcq.shape
    kw, L = C_HEADS * C_KEY_DIM, C_CHUNK
    vw = cols["c_g"] - cols["c_i"]
    dv = vw // C_HEADS
    nch, nlat = tt // L, t_lat // L
    if reverse:
        chunk = lambda i: nch - 1 - i
    else:
        chunk = lambda i: lax.rem(i + nlat, nch)
    gb = max(g for g in range(1, gb + 1) if b % g == 0)
    tok = lambda c0, w: pl.BlockSpec((gb, L, w), lambda bi, i: (bi, chunk(i), c0 // w))
    const2 = lambda x: pl.BlockSpec(x.shape, lambda bi, i: (0, 0))
    f_name = "c_f_bwd" if reverse else "c_f_fwd"
    in_specs = [tok(cols["c_q"], kw), tok(cols[f_name], kw), tok(cols["c_i"], vw),
                const2(lbf), const2(oml), const2(e_mat)]
    args = [pcq, pcf, pcq, lbf, oml, e_mat]
    readout = of is not None
    if readout:
        in_specs += [tok(0, vw), tok(cols["c_g"], vw), const2(norm_g)]
        args += [of, pcq, norm_g]
    kern = functools.partial(_hgrn_kernel, reverse=reverse, readout=readout, dv=dv, gb=gb)
    return pl.pallas_call(
        kern,
        grid=(b // gb, nch),
        in_specs=in_specs,
        out_specs=tok(0, vw),
        out_shape=jax.ShapeDtypeStruct((b, tt, vw), BF16),
        scratch_shapes=[pltpu.VMEM((gb * C_HEADS, dv, C_KEY_DIM), F32),
                        pltpu.VMEM((2 * gb * C_HEADS, L, C_KEY_DIM), F32),
                        pltpu.VMEM((gb * C_HEADS, L, L), F32)],
        compiler_params=_params("parallel", "arbitrary"),
        name="hgrn_bwd_readout" if readout else "hgrn_fwd",
    )(*args)


def _out_kernel(ya_ref, yb_ref, yc_ref, w_ref, h_ref, mod_ref, *rest, final, wa, wb):
    d = h_ref.shape[-1]
    w = w_ref
    y = (_dot(ya_ref[0], w[0:wa, :]) + _dot(yb_ref[0], w[wa:wa + wb, :]) + _dot(yc_ref[0], w[wa + wb:, :]))
    h_new = h_ref[0] + mod_ref[0][:, 2 * d:3 * d] * y
    if final:
        g_ref, out_ref = rest
        out_ref[0] = h_new * lax.rsqrt(jnp.mean(h_new * h_new, axis=-1, keepdims=True) + EPS) * g_ref[...]
    else:
        g_ref, modn_ref, h_out_ref, n_ref = rest
        h_out_ref[0] = h_new
        n_ref[0] = _modulated_norm(h_new, g_ref[...], modn_ref[0], d).astype(n_ref.dtype)


def _out_proj(ya, yb, yc, w_out, h, mod, g_next, mod_next, t_lat, final, tm=256):
    b, tt, d = h.shape
    wa, wb, wc = ya.shape[-1], yb.shape[-1], yc.shape[-1]
    if final:
        tm = _tile(t_lat, 512)
    tok = lambda w: pl.BlockSpec((1, tm, w), lambda i, j: (i, j, 0))
    mod_spec = pl.BlockSpec((1, 1, 3 * d), _mod_index(t_lat // tm))
    in_specs = [tok(wa), tok(wb), tok(wc), pl.BlockSpec(w_out.shape, lambda i, j: (0, 0)), tok(d), mod_spec,
                pl.BlockSpec((1, d), lambda i, j: (0, 0))]
    args = [ya, yb, yc, w_out, h, mod, g_next]
    kern = functools.partial(_out_kernel, final=final, wa=wa, wb=wb)
    if final:
        return pl.pallas_call(
            kern, grid=(b, t_lat // tm), in_specs=in_specs, out_specs=tok(d),
            out_shape=jax.ShapeDtypeStruct((b, t_lat, d), F32),
            compiler_params=_params("parallel", "parallel"), name="out_proj_final",
        )(*args)
    return pl.pallas_call(
        kern, grid=(b, tt // tm), in_specs=in_specs + [mod_spec], out_specs=(tok(d), tok(d)),
        out_shape=(jax.ShapeDtypeStruct((b, tt, d), F32), jax.ShapeDtypeStruct((b, tt, d), BF16)),
        compiler_params=_params("parallel", "parallel"), name="out_proj",
    )(*args, mod_next)


def _packed_layout(d):
    a, bw, c, kq = d // 4, d // 2, d // 4, C_HEADS * C_KEY_DIM
    ref_order = (("a_u", a), ("a_v", a), ("a_z", a), ("b_q", bw), ("b_k", bw), ("b_v", bw), ("b_o", bw),
                 ("b_z", bw), ("gates", 4 * B_HEADS), ("c_q", kq), ("c_f_fwd", kq), ("c_f_bwd", kq),
                 ("c_i", c), ("c_g", c))
    src, start = {}, 0
    for name, w in ref_order:
        src[name] = (start, w)
        start += w
    groups = {"ab": ("b_q", "b_k", "b_v", "b_o", "b_z", "a_u", "a_v", "a_z"),
              "cq": ("c_q", "c_i", "c_g"), "cf": ("c_f_fwd", "c_f_bwd")}
    cols = {}
    for names in groups.values():
        pos = 0
        for name in names:
            cols[name] = pos
            pos += src[name][1]
    return src, groups, cols


def _pack_cols(w, bias, src, names):
    spans = []
    for k in names:
        lo, width = src[k]
        if spans and spans[-1][1] == lo:
            spans[-1][1] = lo + width
        else:
            spans.append([lo, lo + width])
    pick = lambda a: jnp.concatenate([a[..., lo:hi] for lo, hi in spans], axis=-1)
    return pick(w).astype(BF16), pick(bias)[None, :]


BF16_ROWS = 16
LANES = 128


def _grid_transpose_kernel(x_ref, o_ref, scr, *, width):
    lanes = scr.shape[-1]
    pitch = scr.shape[1] // BF16_ROWS
    for c in range(scr.shape[0]):
        for r in range(BF16_ROWS):
            scr[c, r * pitch:r * pitch + width, :] = x_ref[0, r * width:(r + 1) * width,
                                                           c * lanes:(c + 1) * lanes].astype(F32)
    for w in range(width):
        rows = [scr[c, pl.ds(w, BF16_ROWS, stride=pitch), :] for c in range(scr.shape[0])]
        o_ref[0, w] = jnp.concatenate(rows, axis=1).astype(o_ref.dtype)


def _copy_kernel(x_ref, dst_ref, o_ref):
    del dst_ref
    o_ref[...] = x_ref[...]


def _grid_transpose(x, t_lat, rows, width):
    b, tt, f = x.shape
    t_ctx = tt - t_lat
    assert rows % BF16_ROWS == 0 and tt % rows == 0 and t_lat % t_ctx == 0
    tile = BF16_ROWS * width
    lat = pl.pallas_call(
        functools.partial(_grid_transpose_kernel, width=width),
        grid=(b, rows // BF16_ROWS),
        in_specs=[pl.BlockSpec((1, tile, f), lambda i, j: (i, j, 0))],
        out_specs=pl.BlockSpec((1, width, BF16_ROWS, f), lambda i, j: (i, 0, j, 0)),
        out_shape=jax.ShapeDtypeStruct((b, tt // rows, rows, f), x.dtype),
        scratch_shapes=[pltpu.VMEM((f // LANES, BF16_ROWS * (width + 8), LANES), F32)],
        compiler_params=_params("parallel", "parallel"),
        name="grid_transpose",
    )(x).reshape(b, tt, f)
    ctx_spec = pl.BlockSpec((1, t_ctx, f), lambda i: (i, t_lat // t_ctx, 0))
    return pl.pallas_call(
        _copy_kernel,
        grid=(b,),
        in_specs=[ctx_spec, pl.BlockSpec(memory_space=pl.ANY)],
        out_specs=ctx_spec,
        out_shape=jax.ShapeDtypeStruct((b, tt, f), x.dtype),
        input_output_aliases={1: 0},
        compiler_params=_params("parallel"),
        name="context_rows_copy",
    )(x, lat)


def kernel(x, c, ctx, c_ctx, w_ada, b_ada, norm_g, w_in, b_in, w_spatial, b_spatial, conv_qk, mlstm_norm,
           hgrn_lb_logits, hgrn_norm, w_out, final_norm):
    b, t_lat, d = x.shape
    t_ctx = ctx.shape[1]
    tt = t_lat + t_ctx
    depth = w_ada.shape[0]
    src, groups, cols = _packed_layout(d)
    a_width, b_width = d // 4, d // 2
    rows = t_lat // GRID_W

    r_pad = -(-(b + 1) // 8) * 8
    cond = jnp.concatenate([c, c_ctx[None, :], jnp.zeros((r_pad - b - 1, d), F32)], axis=0)
    mod_all = _ada_mod(cond, w_ada, b_ada)
    mods = [jnp.stack([mod_all[l, :b], jnp.broadcast_to(mod_all[l, b], (b, 3 * d))], axis=1).reshape(2 * b, 1, 3 * d)
            for l in range(depth)]

    lbf, oml = _lower_bounds(hgrn_lb_logits.astype(F32))

    e_rows = lax.broadcasted_iota(jnp.int32, (C_SUB * C_KEY_DIM, C_CHUNK), 0) // C_KEY_DIM
    e_cols = lax.broadcasted_iota(jnp.int32, (C_SUB * C_KEY_DIM, C_CHUNK), 1) % C_SUB
    e_mat = (e_rows == e_cols).astype(BF16)

    h, n = _join_norm_mod(x, ctx, norm_g[0:1], mods[0])
    out = None
    for l in range(depth):
        last = l == depth - 1
        w_ab, b_ab = _pack_cols(w_in[l], b_in[l], src, groups["ab"])
        w_cq, b_cq = _pack_cols(w_in[l], b_in[l], src, groups["cq"])
        w_cf, b_cf = _pack_cols(w_in[l], b_in[l], src, groups["cf"])
        g0, gw = src["gates"]
        w_gate_t = w_in[l][:, g0:g0 + gw].T.astype(BF16)
        b_gate = b_in[l][g0:g0 + gw][:, None]

        n_cm = _grid_transpose(n, t_lat, rows, GRID_W)
        n2d, n_cm2d = n.reshape(b * tt, d), n_cm.reshape(b * tt, d)
        p2d = _in_proj(n2d, w_ab, b_ab, 2048, 1664, "in_proj_ab", BF16)
        pcq = _in_proj(n_cm2d, w_cq, b_cq, 2048, 1536, "in_proj_cq", BF16).reshape(b, tt, -1)
        pcf = _in_proj(n_cm2d, w_cf, b_cf, 2048, 1024, "in_proj_cf").reshape(b, tt, -1)
        gates_row = _in_proj_gates_rows(n, w_gate_t, b_gate)
        p3d = p2d.reshape(b, tt, -1)

        ya = _chunk_mlp(p2d, w_spatial[l].astype(BF16), b_spatial[l].T, cols["a_u"], a_width).reshape(b, tt, a_width)

        q = _conv_silu(p3d, conv_qk[l], t_lat, cols["b_q"], 0, b_width, 1.0, transpose=False)
        k_t = _conv_silu(p3d, conv_qk[l], t_lat, cols["b_k"], b_width, b_width,
                         (b_width // B_HEADS) ** -0.5, transpose=True)
        hf = _mlstm_dir(q, k_t, p3d, gates_row, t_lat, cols, reverse=False)
        yb = _mlstm_dir(q, k_t, p3d, gates_row, t_lat, cols, reverse=True, hf=hf, norm_g=mlstm_norm[l:l + 1])

        lbf_l, oml_l = lbf[l:l + 1], oml[l:l + 1]
        of = _hgrn_dir(pcq, pcf, lbf_l, oml_l, e_mat, cols, t_lat, False)
        yc_cm = _hgrn_dir(pcq, pcf, lbf_l, oml_l, e_mat, cols, t_lat, True, of=of, norm_g=hgrn_norm[l:l + 1])
        yc = _grid_transpose(yc_cm, t_lat, GRID_W, rows)

        w_o = w_out[l].astype(BF16)
        if last:
            out = _out_proj(ya, yb, yc, w_o, h, mods[l], final_norm[None, :], None, t_lat, final=True)
        else:
            h, n = _out_proj(ya, yb, yc, w_o, h, mods[l], norm_g[l + 1:l + 2], mods[l + 1], t_lat, final=False)
    return out
```
